```python
import math
import jax, jax.numpy as jnp
from jax import lax
import numpy as np

D_MODEL = 1024
BATCH = 8
SEQ = 2048
DEPTH = 4
DEC_BATCH = 128
DEC_SEQ = 1
PAST_LEN = 16384
PAGE_SIZE = 128

N_EVEN = (DEPTH + 1) // 2
N_ODD = DEPTH // 2
GLA_HEADS = 4
GLA_DK = D_MODEL // 16
GLA_DV = D_MODEL // 8
GLA_RANK = 16
GLA_TAU = 16.0
RET_HEADS = 4
RET_DK = D_MODEL // 8
RET_DV = D_MODEL // 8
CHUNK = 64
CONV_W = 3
D_FF = D_MODEL * 11 // 4
ROPE_BASE = 10000.0
NORM_EPS = 1e-5
ALPHA = (2 * DEPTH) ** 0.25
BETA = (8 * DEPTH) ** -0.25
GLA_QK = GLA_HEADS * GLA_DK
GLA_V = GLA_HEADS * GLA_DV
RET_QK = RET_HEADS * RET_DK
RET_V = RET_HEADS * RET_DV
MIX_SPLITS = (GLA_QK, GLA_QK, GLA_V, GLA_RANK, GLA_V, RET_QK, RET_QK, RET_V, RET_V)
MIX_IN = sum(MIX_SPLITS)
MIX_V = GLA_V + RET_V

kernel_name = "gla_retnet_shortconv_convffn_deepnorm_step"


def _split_points(sizes):
    pts, acc = [], 0
    for s in sizes[:-1]:
        acc += s
        pts.append(acc)
    return pts


def layer_norm(x, g, b):
    xf = x.astype(jnp.float32)
    mu = jnp.mean(xf, axis=-1, keepdims=True)
    var = jnp.mean(jnp.square(xf - mu), axis=-1, keepdims=True)
    return ((xf - mu) * lax.rsqrt(var + NORM_EPS) * g + b).astype(x.dtype)


def rms_head_norm(o, g):
    of = o.astype(jnp.float32)
    return (of * lax.rsqrt(jnp.mean(of * of, axis=-1, keepdims=True) + NORM_EPS) * g).astype(o.dtype)


def group_norm_heads(o, g, b):
    h, dv = o.shape[-2], o.shape[-1]
    of = o.astype(jnp.float32)
    mu = jnp.mean(of, axis=-1, keepdims=True)
    var = jnp.mean(jnp.square(of - mu), axis=-1, keepdims=True)
    return ((of - mu) * lax.rsqrt(var + NORM_EPS) * g.reshape(h, dv) + b.reshape(h, dv)).astype(o.dtype)


def rotary(x, pos):
    d = x.shape[-1]
    inv_freq = 1.0 / (ROPE_BASE ** (jnp.arange(0, d, 2, dtype=jnp.float32) / d))
    ang = pos[:, None] * inv_freq[None, :]
    cos, sin = jnp.cos(ang)[:, None, :], jnp.sin(ang)[:, None, :]
    xf = x.astype(jnp.float32)
    x1, x2 = xf[..., : d // 2], xf[..., d // 2:]
    return jnp.concatenate([x1 * cos - x2 * sin, x2 * cos + x1 * sin], axis=-1).astype(x.dtype)


def chunked_gated_linear_attention(q, k, v, log_a, s0):
    bsz, t_len, h, _ = q.shape
    dv = v.shape[-1]
    c = min(CHUNK, t_len)
    pad = (-t_len) % c

    def prep(a):
        a = jnp.pad(a.astype(jnp.float32), ((0, 0), (0, pad), (0, 0), (0, 0)))
        n = a.shape[1] // c
        return a.reshape(bsz, n, c, h, a.shape[-1]).transpose(1, 0, 3, 2, 4)

    qc, kc, vc, ac = prep(q), prep(k), prep(v), prep(log_a)
    causal = jnp.tril(jnp.ones((c, c), dtype=bool))

    def step(state, xs):
        qb, kb, vb, ab = xs
        cum = jnp.cumsum(ab, axis=2)
        diff = cum[:, :, :, None, :] - cum[:, :, None, :, :]
        decay = jnp.exp(jnp.where(causal[:, :, None], diff, -jnp.inf))
        if ab.shape[-1] == 1:
            scores = jnp.einsum('bhtd,bhsd->bhts', qb, kb) * decay[..., 0]
        else:
            scores = jnp.einsum('bhtd,bhsd,bhtsd->bhts', qb, kb, decay)
        out = (jnp.einsum('bhts,bhsv->bhtv', scores, vb)
               + jnp.einsum('bhtd,bhdv->bhtv', qb * jnp.exp(cum), state))
        last = cum[:, :, -1:, :]
        state = (jnp.swapaxes(jnp.exp(last), -1, -2) * state
                 + jnp.einsum('bhsd,bhsv->bhdv', kb * jnp.exp(last - cum), vb))
        return state, out

    s_fin, oc = lax.scan(step, s0.astype(jnp.float32), (qc, kc, vc, ac))
    o = oc.transpose(1, 0, 3, 2, 4).reshape(bsz, -1, h, dv)[:, :t_len]
    return o.astype(v.dtype), s_fin.astype(s0.dtype)


def causal_dwconv(u, buf, w, b):
    t_len = u.shape[1]
    ub = jnp.concatenate([buf.astype(u.dtype), u], axis=1)
    y = b + sum(w[j] * ub[:, j:j + t_len] for j in range(CONV_W))
    return y, ub[:, -(CONV_W - 1):]


def gla_retention_mixer(x, pos, s_gla, s_ret, w_in, w_gate_up, b_gate, g_gla, g_ret, b_ret, w_out):
    bsz, t_len, _ = x.shape
    z = x @ w_in
    q_g, k_g, v_g, a_lr, r_g, q_r, k_r, v_r, g_r = jnp.split(z, _split_points(MIX_SPLITS), axis=-1)

    def heads(a, h):
        return a.reshape(bsz, t_len, h, -1)

    log_a = jax.nn.log_sigmoid((a_lr @ w_gate_up + b_gate).astype(jnp.float32)) / GLA_TAU
    o_g, s_gla = chunked_gated_linear_attention(
        heads(q_g, GLA_HEADS) * (GLA_DK ** -0.5), heads(k_g, GLA_HEADS),
        heads(v_g, GLA_HEADS), heads(log_a, GLA_HEADS), s_gla)
    o_g = rms_head_norm(o_g, g_gla) * jax.nn.silu(heads(r_g, GLA_HEADS))

    qr = rotary(heads(q_r, RET_HEADS), pos)
    kr = rotary(heads(k_r, RET_HEADS), pos) * (RET_DK ** -0.5)
    log_gamma = jnp.log1p(-jnp.exp2(-5.0 - jnp.arange(RET_HEADS, dtype=jnp.float32)))
    log_g = jnp.broadcast_to(log_gamma[:, None], (bsz, t_len, RET_HEADS, 1))
    o_r, s_ret = chunked_gated_linear_attention(qr, kr, heads(v_r, RET_HEADS), log_g, s_ret)
    o_r = group_norm_heads(o_r, g_ret, b_ret) * jax.nn.silu(heads(g_r, RET_HEADS))

    y = jnp.concatenate([o_g.reshape(bsz, t_len, GLA_V), o_r.reshape(bsz, t_len, RET_V)], axis=-1) @ w_out
    return y, s_gla, s_ret


def short_conv_mixer(x, s_conv, w_in, w_dw, b_dw, w_out):
    gate_b, gate_c, h = jnp.split(x @ w_in, 3, axis=-1)
    conv, s_conv = causal_dwconv(gate_c * h, s_conv, w_dw, b_dw)
    return (gate_b * conv) @ w_out, s_conv


def conv_ffn(x, s_ffn, w_in, w_dw, b_dw, w_out):
    u = x @ w_in
    c, s_ffn = causal_dwconv(u, s_ffn, w_dw, b_dw)
    a, g = jnp.split(c, 2, axis=-1)
    return (jax.nn.silu(g) * a) @ w_out, s_ffn


def run_trunk(x, pos, st_gla, st_ret, st_conv, st_ffn, weights):
    (w_mix_in, w_gla_gate_up, b_gla_gate, g_gla_norm, g_ret_norm, b_ret_norm, w_mix_out,
     w_sc_in, w_sc_dw, b_sc_dw, w_sc_out, ln1_g, ln1_b,
     w_ffn_in, w_ffn_dw, b_ffn_dw, w_ffn_out, ln2_g, ln2_b) = weights
    new_gla, new_ret, new_conv, new_ffn = [], [], [], []
    for layer in range(DEPTH):
        i = layer // 2
        if layer % 2 == 0:
            h, sg, sr = gla_retention_mixer(x, pos, st_gla[i], st_ret[i], w_mix_in[i], w_gla_gate_up[i],
                                            b_gla_gate[i], g_gla_norm[i], g_ret_norm[i], b_ret_norm[i],
                                            w_mix_out[i])
            new_gla.append(sg)
            new_ret.append(sr)
        else:
            h, sc = short_conv_mixer(x, st_conv[i], w_sc_in[i], w_sc_dw[i], b_sc_dw[i], w_sc_out[i])
            new_conv.append(sc)
        x = layer_norm(ALPHA * x + h, ln1_g[layer], ln1_b[layer])
        h, sf = conv_ffn(x, st_ffn[layer], w_ffn_in[layer], w_ffn_dw[layer], b_ffn_dw[layer], w_ffn_out[layer])
        new_ffn.append(sf)
        x = layer_norm(ALPHA * x + h, ln2_g[layer], ln2_b[layer])
    return x, jnp.stack(new_gla), jnp.stack(new_ret), jnp.stack(new_conv), jnp.stack(new_ffn)


def setup_inputs(seed: int = 0) -> dict:
    key = jax.random.key(seed)
    ks = iter(jax.random.split(key, 32))

    def nrm(shape, scale):
        return jax.random.normal(next(ks), shape, jnp.float32) * scale

    return {
        "x_prompt": nrm((BATCH, SEQ, D_MODEL), 1.0),
        "x_sample": nrm((DEC_BATCH, DEC_SEQ, D_MODEL), 1.0),
        "state_gla": nrm((N_EVEN, DEC_BATCH, GLA_HEADS, GLA_DK, GLA_DV), 1.0),
        "state_ret": nrm((N_EVEN, DEC_BATCH, RET_HEADS, RET_DK, RET_DV), 1.0),
        "state_conv": nrm((N_ODD, DEC_BATCH, CONV_W - 1, D_MODEL), 1.0),
        "state_ffn": nrm((DEPTH, DEC_BATCH, CONV_W - 1, 2 * D_FF), 1.0),
        "w_mix_in": nrm((N_EVEN, D_MODEL, MIX_IN), D_MODEL ** -0.5),
        "w_gla_gate_up": nrm((N_EVEN, GLA_RANK, GLA_QK), GLA_RANK ** -0.5),
        "b_gla_gate": nrm((N_EVEN, GLA_QK), 0.01),
        "g_gla_norm": 1.0 + nrm((N_EVEN, GLA_DV), 0.01),
        "g_ret_norm": 1.0 + nrm((N_EVEN, RET_V), 0.01),
        "b_ret_norm": nrm((N_EVEN, RET_V), 0.01),
        "w_mix_out": nrm((N_EVEN, MIX_V, D_MODEL), BETA * MIX_V ** -0.5),
        "w_sc_in": nrm((N_ODD, D_MODEL, 3 * D_MODEL), D_MODEL ** -0.5),
        "w_sc_dw": nrm((N_ODD, CONV_W, D_MODEL), CONV_W ** -0.5),
        "b_sc_dw": nrm((N_ODD, D_MODEL), 0.01),
        "w_sc_out": nrm((N_ODD, D_MODEL, D_MODEL), BETA * D_MODEL ** -0.5),
        "ln1_g": 1.0 + nrm((DEPTH, D_MODEL), 0.01),
        "ln1_b": nrm((DEPTH, D_MODEL), 0.01),
        "w_ffn_in": nrm((DEPTH, D_MODEL, 2 * D_FF), D_MODEL ** -0.5),
        "w_ffn_dw": nrm((DEPTH, CONV_W, 2 * D_FF), CONV_W ** -0.5),
        "b_ffn_dw": nrm((DEPTH, 2 * D_FF), 0.01),
        "w_ffn_out": nrm((DEPTH, D_FF, D_MODEL), BETA * D_FF ** -0.5),
        "ln2_g": 1.0 + nrm((DEPTH, D_MODEL), 0.01),
        "ln2_b": nrm((DEPTH, D_MODEL), 0.01),
    }


def reference(x_prompt, x_sample, state_gla, state_ret, state_conv, state_ffn,
              w_mix_in, w_gla_gate_up, b_gla_gate, g_gla_norm, g_ret_norm, b_ret_norm, w_mix_out,
              w_sc_in, w_sc_dw, b_sc_dw, w_sc_out, ln1_g, ln1_b,
              w_ffn_in, w_ffn_dw, b_ffn_dw, w_ffn_out, ln2_g, ln2_b):
    weights = (w_mix_in, w_gla_gate_up, b_gla_gate, g_gla_norm, g_ret_norm, b_ret_norm, w_mix_out,
               w_sc_in, w_sc_dw, b_sc_dw, w_sc_out, ln1_g, ln1_b,
               w_ffn_in, w_ffn_dw, b_ffn_dw, w_ffn_out, ln2_g, ln2_b)
    bp, tp = x_prompt.shape[0], x_prompt.shape[1]
    dt = x_prompt.dtype
    z_gla = jnp.zeros((N_EVEN, bp, GLA_HEADS, GLA_DK, GLA_DV), dt)
    z_ret = jnp.zeros((N_EVEN, bp, RET_HEADS, RET_DK, RET_DV), dt)
    z_conv = jnp.zeros((N_ODD, bp, CONV_W - 1, D_MODEL), dt)
    z_ffn = jnp.zeros((DEPTH, bp, CONV_W - 1, 2 * D_FF), dt)
    pos_p = jnp.arange(tp, dtype=jnp.float32)
    y_prompt, gla_p, ret_p, conv_p, ffn_p = run_trunk(x_prompt, pos_p, z_gla, z_ret, z_conv, z_ffn, weights)
    pos_s = PAST_LEN + jnp.arange(x_sample.shape[1], dtype=jnp.float32)
    y_sample, gla_s, ret_s, conv_s, ffn_s = run_trunk(x_sample, pos_s, state_gla, state_ret,
                                                      state_conv, state_ffn, weights)
    return (y_prompt, y_sample, gla_p, gla_s, ret_p, ret_s, conv_p, conv_s, ffn_p, ffn_s)
```

```python
import functools

import jax
import jax.numpy as jnp
from jax import lax
from jax.experimental import pallas as pl
from jax.experimental.pallas import tpu as pltpu

F32 = jnp.float32
BF16 = jnp.bfloat16

D_MODEL = 1024
DEPTH = 4
PAST_LEN = 16384
GLA_HEADS = 4
GLA_DK = 64
GLA_DV = 128
GLA_RANK = 16
GLA_TAU = 16.0
RET_HEADS = 4
RET_DK = 128
RET_DV = 128
CONV_W = 3
D_FF = D_MODEL * 11 // 4
ROPE_BASE = 10000.0
NORM_EPS = 1e-5
ALPHA = (2 * DEPTH) ** 0.25
GLA_QK = GLA_HEADS * GLA_DK
GLA_V = GLA_HEADS * GLA_DV
RET_QK = RET_HEADS * RET_DK
RET_V = RET_HEADS * RET_DV

LANES = 128
SUBLANES = 8
VMEM_LIMIT = 56 * 1024 * 1024

GLA_CHUNK = 64
GLA_SUB = 16
MIX_TILE = 256
ROW_TILE = 512
FF_CHUNK = 256
SC_CHUNK = 256
PAD_RANK = LANES

_OFF_QG = 0
_OFF_KG = _OFF_QG + GLA_QK
_OFF_VG = _OFF_KG + GLA_QK
_OFF_RG = _OFF_VG + GLA_V
_OFF_QR = _OFF_RG + GLA_V
_OFF_KR = _OFF_QR + RET_QK
_OFF_VR = _OFF_KR + RET_QK
_OFF_GR = _OFF_VR + RET_V
_OFF_LR = _OFF_GR + RET_V
MIX_COLS = _OFF_LR + PAD_RANK


def _dot(a, b):
    return jnp.dot(a, b, preferred_element_type=F32)


def _dot_nt(a, b):
    return lax.dot_general(a, b, (((1,), (1,)), ((), ())), preferred_element_type=F32)


def _layer_norm(x, g, b):
    mu = jnp.mean(x, axis=-1, keepdims=True)
    xc = x - mu
    var = jnp.mean(xc * xc, axis=-1, keepdims=True)
    return xc * lax.rsqrt(var + NORM_EPS) * g + b


def _silu(x):
    return x * jax.nn.sigmoid(x)


def _log_sigmoid(x):
    return jnp.minimum(x, 0.0) - jnp.log1p(jnp.exp(-jnp.abs(x)))


def _split3(x):
    hi = x.astype(BF16)
    r1 = x - hi.astype(F32)
    mid = r1.astype(BF16)
    lo = (r1 - mid.astype(F32)).astype(BF16)
    return hi, mid, lo


def _causal_conv3(u, prev2, prev1, w, b):
    t, c = u.shape
    u1 = pltpu.roll(u, 1, 0)
    u2 = pltpu.roll(u, 2, 0)
    rid = lax.broadcasted_iota(jnp.int32, (SUBLANES, c), 0)
    h1 = jnp.where(rid == 0, prev1, u1[:SUBLANES])
    h2 = jnp.where(rid == 0, prev2, jnp.where(rid == 1, prev1, u2[:SUBLANES]))
    u1 = jnp.concatenate([h1, u1[SUBLANES:]], axis=0)
    u2 = jnp.concatenate([h2, u2[SUBLANES:]], axis=0)
    return b + w[0:1] * u2 + w[1:2] * u1 + w[2:3] * u


def _ffn_prompt_kernel(x_ref, win_ref, wdw_ref, bdw_ref, wout_ref, g_ref, b_ref,
                       y_ref, st_ref, xb_ref, carry_ref, acc_ref, *, tiles_per_seq, n_chunks):
    i = pl.program_id(0)

    @pl.when(i % tiles_per_seq == 0)
    def _():
        carry_ref[...] = jnp.zeros_like(carry_ref)

    tm = x_ref.shape[0]
    fc = wout_ref.shape[1]
    xb_ref[...] = x_ref[...].astype(BF16)
    acc_ref[...] = jnp.zeros_like(acc_ref)

    def body(f, carry):
        u = _dot(xb_ref[...], win_ref[f])
        prev = carry_ref[f]
        c = _causal_conv3(u, prev[SUBLANES - 2:SUBLANES - 1], prev[SUBLANES - 1:SUBLANES],
                          wdw_ref[f], bdw_ref[f])
        tail = u[tm - SUBLANES:]
        carry_ref[f] = tail
        st_ref[0, f] = tail
        h = _silu(c[:, fc:]) * c[:, :fc]
        acc_ref[...] += _dot(h.astype(BF16), wout_ref[f])
        return carry

    lax.fori_loop(0, n_chunks, body, 0)
    y_ref[...] = _layer_norm(ALPHA * x_ref[...] + acc_ref[...], g_ref[...], b_ref[...])


def _ffn_prompt(x, seq_len, win, wdw, bdw, wout, g, b):
    n, d = x.shape
    tm = min(ROW_TILE, seq_len)
    assert seq_len % tm == 0 and n % seq_len == 0
    n_seq = n // seq_len
    tps = seq_len // tm
    nfc, _, fc2 = win.shape
    fc = fc2 // 2
    const3 = lambda i: (0, 0, 0)
    const2 = lambda i: (0, 0)
    y, st = pl.pallas_call(
        functools.partial(_ffn_prompt_kernel, tiles_per_seq=tps, n_chunks=nfc),
        grid=(n // tm,),
        in_specs=[
            pl.BlockSpec((tm, d), lambda i: (i, 0)),
            pl.BlockSpec((nfc, d, fc2), const3, pipeline_mode=pl.Buffered(1)),
            pl.BlockSpec((nfc, CONV_W, fc2), const3),
            pl.BlockSpec((nfc, 1, fc2), const3),
            pl.BlockSpec((nfc, fc, d), const3, pipeline_mode=pl.Buffered(1)),
            pl.BlockSpec((1, d), const2),
            pl.BlockSpec((1, d), const2),
        ],
        out_specs=[
            pl.BlockSpec((tm, d), lambda i: (i, 0)),
            pl.BlockSpec((1, nfc, SUBLANES, fc2), lambda i: (i // tps, 0, 0, 0)),
        ],
        out_shape=[
            jax.ShapeDtypeStruct((n, d), F32),
            jax.ShapeDtypeStruct((n_seq, nfc, SUBLANES, fc2), F32),
        ],
        scratch_shapes=[
            pltpu.VMEM((tm, d), BF16),
            pltpu.VMEM((nfc, SUBLANES, fc2), F32),
            pltpu.VMEM((tm, d), F32),
        ],
        compiler_params=pltpu.CompilerParams(
            dimension_semantics=("arbitrary",), vmem_limit_bytes=VMEM_LIMIT),
        name="ffn_prompt",
    )(x, win, wdw, bdw, wout, g, b)
    st = st[:, :, SUBLANES - 2:, :].reshape(n_seq, nfc, 2, 2, fc)
    st = st.transpose(0, 2, 3, 1, 4).reshape(n_seq, 2, 2 * nfc * fc)
    return y, st


def _sc_prompt_kernel(x_ref, win_ref, wdw_ref, bdw_ref, wout_ref, g_ref, b_ref,
                      y_ref, st_ref, xb_ref, carry_ref, acc_ref, *, tiles_per_seq, n_chunks):
    i = pl.program_id(0)

    @pl.when(i % tiles_per_seq == 0)
    def _():
        carry_ref[...] = jnp.zeros_like(carry_ref)

    tm = x_ref.shape[0]
    cw = wout_ref.shape[1]
    xb_ref[...] = x_ref[...].astype(BF16)
    acc_ref[...] = jnp.zeros_like(acc_ref)

    def body(f, carry):
        z = _dot(xb_ref[...], win_ref[f])
        ch = z[:, cw:2 * cw] * z[:, 2 * cw:]
        prev = carry_ref[f]
        conv = _causal_conv3(ch, prev[SUBLANES - 2:SUBLANES - 1], prev[SUBLANES - 1:SUBLANES],
                             wdw_ref[f], bdw_ref[f])
        tail = ch[tm - SUBLANES:]
        carry_ref[f] = tail
        st_ref[0, f] = tail
        m = z[:, :cw] * conv
        acc_ref[...] += _dot(m.astype(BF16), wout_ref[f])
        return carry

    lax.fori_loop(0, n_chunks, body, 0)
    y_ref[...] = _layer_norm(ALPHA * x_ref[...] + acc_ref[...], g_ref[...], b_ref[...])


def _sc_prompt(x, seq_len, win, wdw, bdw, wout, g, b):
    n, d = x.shape
    tm = min(ROW_TILE, seq_len)
    assert seq_len % tm == 0 and n % seq_len == 0
    n_seq = n // seq_len
    tps = seq_len // tm
    nc, _, cw3 = win.shape
    cw = cw3 // 3
    const3 = lambda i: (0, 0, 0)
    const2 = lambda i: (0, 0)
    y, st = pl.pallas_call(
        functools.partial(_sc_prompt_kernel, tiles_per_seq=tps, n_chunks=nc),
        grid=(n // tm,),
        in_specs=[
            pl.BlockSpec((tm, d), lambda i: (i, 0)),
            pl.BlockSpec((nc, d, cw3), const3, pipeline_mode=pl.Buffered(1)),
            pl.BlockSpec((nc, CONV_W, cw), const3),
            pl.BlockSpec((nc, 1, cw), const3),
            pl.BlockSpec((nc, cw, d), const3, pipeline_mode=pl.Buffered(1)),
            pl.BlockSpec((1, d), const2),
            pl.BlockSpec((1, d), const2),
        ],
        out_specs=[
            pl.BlockSpec((tm, d), lambda i: (i, 0)),
            pl.BlockSpec((1, nc, SUBLANES, cw), lambda i: (i // tps, 0, 0, 0)),
        ],
        out_shape=[
            jax.ShapeDtypeStruct((n, d), F32),
            jax.ShapeDtypeStruct((n_seq, nc, SUBLANES, cw), F32),
        ],
        scratch_shapes=[
            pltpu.VMEM((tm, d), BF16),
            pltpu.VMEM((nc, SUBLANES, cw), F32),
            pltpu.VMEM((tm, d), F32),
        ],
        compiler_params=pltpu.CompilerParams(
            dimension_semantics=("arbitrary",), vmem_limit_bytes=VMEM_LIMIT),
        name="sc_prompt",
    )(x, win, wdw, bdw, wout, g, b)
    st = st[:, :, SUBLANES - 2:, :].transpose(0, 2, 1, 3).reshape(n_seq, 2, nc * cw)
    return y, st


def _rotary(x, cosf, sinf):
    return x * cosf + pltpu.roll(x, RET_DK // 2, 1) * sinf


def _gla_chunk(q_c, k_c, v_c, wn, r, st, mask_q, masks_p, mask_bd, sblock):
    c = q_c.shape[0]
    n_sub = c // GLA_SUB
    cum = r + wn
    tot = cum[c - 1:c, :]
    qw = q_c * jnp.exp(wn)
    qc = q_c * jnp.exp(cum)
    kd = k_c * jnp.exp(tot - cum)
    vst = jnp.concatenate([v_c[:, h * GLA_DV:(h + 1) * GLA_DV] for h in range(GLA_HEADS)], axis=0)
    vst_b = vst.astype(BF16)
    intra = []
    for i in range(n_sub):
        r_i = r[i * GLA_SUB:i * GLA_SUB + 1, :]
        kt = jnp.where(sblock <= i, k_c * jnp.exp(r_i - cum), 0.0)
        kbig = jnp.concatenate([kt] * GLA_HEADS, axis=0).astype(BF16)
        q16 = qw[i * GLA_SUB:(i + 1) * GLA_SUB]
        qs = jnp.where(mask_q, jnp.concatenate([q16] * GLA_HEADS, axis=0), 0.0).astype(BF16)
        sc = _dot_nt(qs, kbig)
        p = jnp.where(masks_p[i], sc, 0.0).astype(BF16)
        intra.append(_dot(p, vst_b))
    qcs = jnp.where(mask_bd, jnp.concatenate([qc] * GLA_HEADS, axis=0), 0.0).astype(BF16)
    inter = _dot_nt(qcs, st.astype(BF16))
    heads = []
    for h in range(GLA_HEADS):
        intra_h = jnp.concatenate([intra[i][h * GLA_SUB:(h + 1) * GLA_SUB] for i in range(n_sub)], axis=0)
        heads.append(intra_h + inter[h * c:(h + 1) * c])
    o = jnp.concatenate(heads, axis=1)
    kdbig = jnp.where(mask_bd, jnp.concatenate([kd] * GLA_HEADS, axis=0), 0.0).astype(BF16)
    st_new = jnp.exp(tot) * st + _dot(vst.T.astype(BF16), kdbig)
    return o, st_new


def _mix_prompt_kernel(x_ref, win_ref, wup_ref, bgate_ref, ggla_ref, gret_ref, bret_ref, wout_ref,
                       lng_ref, lnb_ref, cos_ref, sin_ref, mcat_ref, dmask_ref, qdec_ref, kdec_ref,
                       y_ref, sg_ref, sr_ref, stg_ref, str_ref):
    j = pl.program_id(1)
    nj = pl.num_programs(1)

    @pl.when(j == 0)
    def _():
        stg_ref[...] = jnp.zeros_like(stg_ref)
        str_ref[...] = jnp.zeros_like(str_ref)

    tc = x_ref.shape[0]
    x = x_ref[...]
    xb = x.astype(BF16)

    def proj(off, width):
        return _dot(xb, win_ref[:, off:off + width])

    qg = proj(_OFF_QG, GLA_QK) * (GLA_DK ** -0.5)
    kg = proj(_OFF_KG, GLA_QK)
    vg = proj(_OFF_VG, GLA_V)
    lr = proj(_OFF_LR, PAD_RANK)
    gate = _dot(lr.astype(BF16), wup_ref[...]) + bgate_ref[...]
    lg = _log_sigmoid(gate) * (1.0 / GLA_TAU)
    hi, mid, lo = _split3(lg)
    cs = _dot(mcat_ref[...], jnp.concatenate([hi, mid, lo], axis=1))
    cs = cs[:, :GLA_QK] + cs[:, GLA_QK:2 * GLA_QK] + cs[:, 2 * GLA_QK:]
    wn_all = cs[:tc]
    r_all = cs[tc:]

    c = GLA_CHUNK
    hq = GLA_HEADS * GLA_SUB
    row64 = lax.broadcasted_iota(jnp.int32, (hq, GLA_QK), 0)
    lane64 = lax.broadcasted_iota(jnp.int32, (hq, GLA_QK), 1)
    lane_head = lax.shift_right_logical(lane64, 6)
    row_head = lax.shift_right_logical(row64, 4)
    mask_q = lane_head == row_head
    masks_p = [mask_q & ((lane64 & (c - 1)) <= (row64 & (GLA_SUB - 1)) + i * GLA_SUB)
               for i in range(c // GLA_SUB)]
    rowb = lax.broadcasted_iota(jnp.int32, (GLA_HEADS * c, GLA_QK), 0)
    laneb = lax.broadcasted_iota(jnp.int32, (GLA_HEADS * c, GLA_QK), 1)
    mask_bd = lax.shift_right_logical(rowb, 6) == lax.shift_right_logical(laneb, 6)
    sblock = lax.shift_right_logical(lax.broadcasted_iota(jnp.int32, (c, GLA_QK), 0), 4)

    st = stg_ref[...]
    og_chunks = []
    for ci in range(tc // c):
        sl = slice(ci * c, (ci + 1) * c)
        o_c, st = _gla_chunk(qg[sl], kg[sl], vg[sl], wn_all[sl], r_all[sl], st,
                             mask_q, masks_p, mask_bd, sblock)
        og_chunks.append(o_c)
    stg_ref[...] = st
    og = jnp.concatenate(og_chunks, axis=0)
    rg = proj(_OFF_RG, GLA_V)
    ggla = ggla_ref[...]
    og_heads = []
    for h in range(GLA_HEADS):
        sl = slice(h * GLA_DV, (h + 1) * GLA_DV)
        o = og[:, sl]
        o = o * lax.rsqrt(jnp.mean(o * o, axis=-1, keepdims=True) + NORM_EPS) * ggla
        og_heads.append(o * _silu(rg[:, sl]))

    cosf = cos_ref[...]
    sinf = sin_ref[...]
    or_heads = []
    for h in range(RET_HEADS):
        sl = slice(h * RET_DK, (h + 1) * RET_DK)
        qh = _rotary(proj(_OFF_QR + h * RET_DK, RET_DK), cosf, sinf)
        kh = _rotary(proj(_OFF_KR + h * RET_DK, RET_DK), cosf, sinf) * (RET_DK ** -0.5)
        vh = proj(_OFF_VR + h * RET_DV, RET_DV).astype(BF16)
        gh = proj(_OFF_GR + h * RET_DV, RET_DV)
        qdec = qdec_ref[h]
        s_h = str_ref[h]
        sc = _dot_nt(qh.astype(BF16), kh.astype(BF16)) * dmask_ref[h]
        o = _dot(sc.astype(BF16), vh) + _dot((qh * qdec).astype(BF16), s_h.astype(BF16))
        str_ref[h] = qdec[tc - 1:tc, :] * s_h + _dot((kh * kdec_ref[h]).T.astype(BF16), vh)
        mu = jnp.mean(o, axis=-1, keepdims=True)
        oc = o - mu
        var = jnp.mean(oc * oc, axis=-1, keepdims=True)
        o = oc * lax.rsqrt(var + NORM_EPS) * gret_ref[:, sl] + bret_ref[:, sl]
        or_heads.append(o * _silu(gh))

    mixed = jnp.concatenate(og_heads + or_heads, axis=1).astype(BF16)
    y = _dot(mixed, wout_ref[...])
    y_ref[...] = _layer_norm(ALPHA * x + y, lng_ref[...], lnb_ref[...])

    @pl.when(j == nj - 1)
    def _():
        sg_ref[0] = stg_ref[...].T
        sr_ref[0] = str_ref[...]


def _mix_prompt(x, n_seq, seq_len, win, wup, bgate, ggla, gret, bret, wout, lng, lnb, tables):
    n, d = x.shape
    tc = min(MIX_TILE, seq_len)
    assert seq_len % tc == 0 and tc % GLA_CHUNK == 0
    nt = seq_len // tc
    cosf, sinf, mcat, dmask, qdec, kdec = tables
    c2 = lambda b, j: (0, 0)
    c3 = lambda b, j: (0, 0, 0)
    y, sg, sr = pl.pallas_call(
        _mix_prompt_kernel,
        grid=(n_seq, nt),
        in_specs=[
            pl.BlockSpec((tc, d), lambda b, j: (b * nt + j, 0)),
            pl.BlockSpec((d, MIX_COLS), c2, pipeline_mode=pl.Buffered(1)),
            pl.BlockSpec((PAD_RANK, GLA_QK), c2),
            pl.BlockSpec((1, GLA_QK), c2),
            pl.BlockSpec((1, GLA_DV), c2),
            pl.BlockSpec((1, RET_V), c2),
            pl.BlockSpec((1, RET_V), c2),
            pl.BlockSpec((GLA_V + RET_V, d), c2, pipeline_mode=pl.Buffered(1)),
            pl.BlockSpec((1, d), c2),
            pl.BlockSpec((1, d), c2),
            pl.BlockSpec((tc, RET_DK), lambda b, j: (j, 0)),
            pl.BlockSpec((tc, RET_DK), lambda b, j: (j, 0)),
            pl.BlockSpec((2 * tc, tc), c2),
            pl.BlockSpec((RET_HEADS, tc, tc), c3),
            pl.BlockSpec((RET_HEADS, tc, RET_DK), c3),
            pl.BlockSpec((RET_HEADS, tc, RET_DK), c3),
        ],
        out_specs=[
            pl.BlockSpec((tc, d), lambda b, j: (b * nt + j, 0)),
            pl.BlockSpec((1, GLA_QK, GLA_DV), lambda b, j: (b, 0, 0)),
            pl.BlockSpec((1, RET_HEADS, RET_DK, RET_DV), lambda b, j: (b, 0, 0, 0)),
        ],
        out_shape=[
            jax.ShapeDtypeStruct((n, d), F32),
            jax.ShapeDtypeStruct((n_seq, GLA_QK, GLA_DV), F32),
            jax.ShapeDtypeStruct((n_seq, RET_HEADS, RET_DK, RET_DV), F32),
        ],
        scratch_shapes=[
            pltpu.VMEM((GLA_DV, GLA_QK), F32),
            pltpu.VMEM((RET_HEADS, RET_DK, RET_DV), F32),
        ],
        compiler_params=pltpu.CompilerParams(
            dimension_semantics=("arbitrary", "arbitrary"), vmem_limit_bytes=VMEM_LIMIT),
        name="mix_prompt",
    )(x, win, wup, bgate, ggla, gret, bret, wout, lng, lnb, cosf, sinf, mcat, dmask, qdec, kdec)
    return y, sg.reshape(n_seq, GLA_HEADS, GLA_DK, GLA_DV), sr


def _mix_tables(seq_len):
    tc = min(MIX_TILE, seq_len)
    pos = jnp.arange(seq_len, dtype=F32)
    cosf, sinf = _rotary_tables(pos)
    t = jnp.arange(tc)
    same_sub = (t[:, None] // GLA_SUB) == (t[None, :] // GLA_SUB)
    same_chunk = (t[:, None] // GLA_CHUNK) == (t[None, :] // GLA_CHUNK)
    m_within = same_sub & (t[None, :] <= t[:, None])
    m_before = same_chunk & ((t[None, :] // GLA_SUB) < (t[:, None] // GLA_SUB))
    mcat = jnp.concatenate([m_within, m_before], axis=0).astype(BF16)
    log_gamma = jnp.log1p(-jnp.exp2(-5.0 - jnp.arange(RET_HEADS, dtype=F32)))
    tf = t.astype(F32)
    diff = tf[:, None] - tf[None, :]
    dmask = jnp.where(diff >= 0, jnp.exp(diff[None] * log_gamma[:, None, None]), 0.0)
    qdec = jnp.exp((tf[None, :, None] + 1.0) * log_gamma[:, None, None])
    kdec = jnp.exp((tc - 1.0 - tf[None, :, None]) * log_gamma[:, None, None])
    qdec = jnp.broadcast_to(qdec, (RET_HEADS, tc, RET_DK))
    kdec = jnp.broadcast_to(kdec, (RET_HEADS, tc, RET_DK))
    return cosf, sinf, mcat, dmask, qdec, kdec


def _rotary_tables(pos):
    inv_freq = 1.0 / (ROPE_BASE ** (jnp.arange(0, RET_DK, 2, dtype=F32) / RET_DK))
    ang = pos[:, None] * inv_freq[None, :]
    cos, sin = jnp.cos(ang), jnp.sin(ang)
    return jnp.concatenate([cos, cos], axis=1), jnp.concatenate([-sin, sin], axis=1)


def _ffn_sample_kernel(x_ref, st_ref, win_ref, wdw_ref, bdw_ref, wout_ref, g_ref, b_ref,
                       y_ref, nst_ref):
    x = x_ref[...]
    xb = x.astype(BF16)
    w2 = 2 * D_FF
    u = _dot(xb, win_ref[...])
    buf0 = st_ref[:, :w2]
    buf1 = st_ref[:, w2:]
    wdw = wdw_ref[...]
    c = bdw_ref[...] + wdw[0:1] * buf0 + wdw[1:2] * buf1 + wdw[2:3] * u
    h = _silu(c[:, D_FF:]) * c[:, :D_FF]
    y = _dot(h.astype(BF16), wout_ref[...])
    y_ref[...] = _layer_norm(ALPHA * x + y, g_ref[...], b_ref[...])
    nst_ref[:, :w2] = buf1
    nst_ref[:, w2:] = u


def _ffn_sample(x, st, win, wdw, bdw, wout, g, b):
    s, d = x.shape
    return pl.pallas_call(
        _ffn_sample_kernel,
        out_shape=[jax.ShapeDtypeStruct((s, d), F32), jax.ShapeDtypeStruct(st.shape, F32)],
        compiler_params=pltpu.CompilerParams(vmem_limit_bytes=VMEM_LIMIT),
        name="ffn_sample",
    )(x, st, win, wdw, bdw, wout, g, b)


def _sc_sample_kernel(x_ref, st_ref, win_ref, wdw_ref, bdw_ref, wout_ref, g_ref, b_ref,
                      y_ref, nst_ref):
    x = x_ref[...]
    d = x.shape[1]
    z = _dot(x.astype(BF16), win_ref[...])
    ch = z[:, d:2 * d] * z[:, 2 * d:]
    buf0 = st_ref[:, :d]
    buf1 = st_ref[:, d:]
    wdw = wdw_ref[...]
    conv = bdw_ref[...] + wdw[0:1] * buf0 + wdw[1:2] * buf1 + wdw[2:3] * ch
    y = _dot((z[:, :d] * conv).astype(BF16), wout_ref[...])
    y_ref[...] = _layer_norm(ALPHA * x + y, g_ref[...], b_ref[...])
    nst_ref[:, :d] = buf1
    nst_ref[:, d:] = ch


def _sc_sample(x, st, win, wdw, bdw, wout, g, b):
    s, d = x.shape
    return pl.pallas_call(
        _sc_sample_kernel,
        out_shape=[jax.ShapeDtypeStruct((s, d), F32), jax.ShapeDtypeStruct(st.shape, F32)],
        compiler_params=pltpu.CompilerParams(vmem_limit_bytes=VMEM_LIMIT),
        name="sc_sample",
    )(x, st, win, wdw, bdw, wout, g, b)


def _mix_sample_proj_kernel(x_ref, win_ref, wup_ref, bgate_ref, cos_ref, sin_ref,
                            cols_ref, rows_ref, gates_ref):
    xb = x_ref[...].astype(BF16)

    def proj(off, width):
        return _dot(xb, win_ref[:, off:off + width])

    qg = proj(_OFF_QG, GLA_QK) * (GLA_DK ** -0.5)
    kg = proj(_OFF_KG, GLA_QK)
    lr = proj(_OFF_LR, PAD_RANK)
    gate = _dot(lr.astype(BF16), wup_ref[...]) + bgate_ref[...]
    ag = jnp.exp(_log_sigmoid(gate) * (1.0 / GLA_TAU))
    cosf = cos_ref[...]
    sinf = sin_ref[...]
    qr = [_rotary(proj(_OFF_QR + h * RET_DK, RET_DK), cosf, sinf) for h in range(RET_HEADS)]
    kr = [_rotary(proj(_OFF_KR + h * RET_DK, RET_DK), cosf, sinf) * (RET_DK ** -0.5)
          for h in range(RET_HEADS)]
    cols_ref[0:GLA_QK, :] = ag.T
    cols_ref[GLA_QK:2 * GLA_QK, :] = kg.T
    cols_ref[2 * GLA_QK:3 * GLA_QK, :] = qg.T
    base = 3 * GLA_QK
    for h in range(RET_HEADS):
        cols_ref[base + h * RET_DK:base + (h + 1) * RET_DK, :] = kr[h].T
        cols_ref[base + RET_QK + h * RET_DK:base + RET_QK + (h + 1) * RET_DK, :] = qr[h].T
    rows_ref[:, :GLA_V] = proj(_OFF_VG, GLA_V)
    rows_ref[:, GLA_V:] = proj(_OFF_VR, RET_V)
    gates_ref[:, :GLA_V] = proj(_OFF_RG, GLA_V)
    gates_ref[:, GLA_V:] = proj(_OFF_GR, RET_V)


_COL_ROWS = 3 * GLA_QK + 2 * RET_QK


def _mix_sample_state_kernel(cols_ref, rows_ref, sg_ref, sr_ref, gam_ref, nsg_ref, nsr_ref, o_ref):
    nb = sg_ref.shape[0]
    s = cols_ref.shape[1]
    blk = pl.program_id(0)
    lane = lax.broadcasted_iota(jnp.int32, (1, s), 1)

    def body(bb, carry):
        b = blk * nb + bb
        onehot = (lane == b).astype(F32)
        col = jnp.sum(cols_ref[...] * onehot, axis=1, keepdims=True)
        vrow = rows_ref[b]
        a_g = col[0:GLA_QK]
        k_g = col[GLA_QK:2 * GLA_QK]
        q_g = col[2 * GLA_QK:3 * GLA_QK]
        base = 3 * GLA_QK
        outs = []
        for h in range(GLA_HEADS):
            rs = slice(h * GLA_DK, (h + 1) * GLA_DK)
            v_h = vrow[:, h * GLA_DV:(h + 1) * GLA_DV]
            s_new = a_g[rs] * sg_ref[bb, h] + k_g[rs] * v_h
            nsg_ref[bb, h] = s_new
            outs.append(jnp.sum(q_g[rs] * s_new, axis=0, keepdims=True))
        for h in range(RET_HEADS):
            rs = slice(h * RET_DK, (h + 1) * RET_DK)
            k_h = col[base + h * RET_DK:base + (h + 1) * RET_DK]
            q_h = col[base + RET_QK + h * RET_DK:base + RET_QK + (h + 1) * RET_DK]
            v_h = vrow[:, GLA_V + h * RET_DV:GLA_V + (h + 1) * RET_DV]
            s_new = gam_ref[h] * sr_ref[bb, h] + k_h * v_h
            nsr_ref[bb, h] = s_new
            outs.append(jnp.sum(q_h * s_new, axis=0, keepdims=True))
        o_ref[b] = jnp.concatenate(outs, axis=1)
        return carry

    lax.fori_loop(0, nb, body, 0)


def _mix_sample_out_kernel(x_ref, o_ref, gates_ref, ggla_ref, gret_ref, bret_ref, wout_ref,
                           lng_ref, lnb_ref, y_ref):
    x = x_ref[...]
    o = o_ref[...]
    gates = gates_ref[...]
    ggla = ggla_ref[...]
    heads = []
    for h in range(GLA_HEADS):
        sl = slice(h * GLA_DV, (h + 1) * GLA_DV)
        oh = o[:, sl]
        oh = oh * lax.rsqrt(jnp.mean(oh * oh, axis=-1, keepdims=True) + NORM_EPS) * ggla
        heads.append(oh * _silu(gates[:, sl]))
    for h in range(RET_HEADS):
        sl = slice(GLA_V + h * RET_DV, GLA_V + (h + 1) * RET_DV)
        ps = slice(h * RET_DV, (h + 1) * RET_DV)
        oh = o[:, sl]
        mu = jnp.mean(oh, axis=-1, keepdims=True)
        oc = oh - mu
        var = jnp.mean(oc * oc, axis=-1, keepdims=True)
        oh = oc * lax.rsqrt(var + NORM_EPS) * gret_ref[:, ps] + bret_ref[:, ps]
        heads.append(oh * _silu(gates[:, sl]))
    mixed = jnp.concatenate(heads, axis=1).astype(BF16)
    y = _dot(mixed, wout_ref[...])
    y_ref[...] = _layer_norm(ALPHA * x + y, lng_ref[...], lnb_ref[...])


def _mix_sample(x, sg, sr, win, wup, bgate, ggla, gret, bret, wout, lng, lnb, cos_s, sin_s, gam):
    s, d = x.shape
    cols, rows, gates = pl.pallas_call(
        _mix_sample_proj_kernel,
        out_shape=[
            jax.ShapeDtypeStruct((_COL_ROWS, s), F32),
            jax.ShapeDtypeStruct((s, GLA_V + RET_V), F32),
            jax.ShapeDtypeStruct((s, GLA_V + RET_V), F32),
        ],
        compiler_params=pltpu.CompilerParams(vmem_limit_bytes=VMEM_LIMIT),
        name="mix_sample_proj",
    )(x, win, wup, bgate, cos_s, sin_s)
    rows = rows.reshape(s, 1, GLA_V + RET_V)

    nb = SUBLANES
    assert s % nb == 0
    nsg, nsr, o = pl.pallas_call(
        _mix_sample_state_kernel,
        grid=(s // nb,),
        in_specs=[
            pl.BlockSpec((_COL_ROWS, s), lambda i: (0, 0)),
            pl.BlockSpec((s, 1, GLA_V + RET_V), lambda i: (0, 0, 0)),
            pl.BlockSpec((nb, GLA_HEADS, GLA_DK, GLA_DV), lambda i: (i, 0, 0, 0)),
            pl.BlockSpec((nb, RET_HEADS, RET_DK, RET_DV), lambda i: (i, 0, 0, 0)),
            pl.BlockSpec((RET_HEADS, RET_DK, RET_DV), lambda i: (0, 0, 0)),
        ],
        out_specs=[
            pl.BlockSpec((nb, GLA_HEADS, GLA_DK, GLA_DV), lambda i: (i, 0, 0, 0)),
            pl.BlockSpec((nb, RET_HEADS, RET_DK, RET_DV), lambda i: (i, 0, 0, 0)),
            pl.BlockSpec((s, 1, GLA_V + RET_V), lambda i: (0, 0, 0)),
        ],
        out_shape=[
            jax.ShapeDtypeStruct(sg.shape, F32),
            jax.ShapeDtypeStruct(sr.shape, F32),
            jax.ShapeDtypeStruct((s, 1, GLA_V + RET_V), F32),
        ],
        compiler_params=pltpu.CompilerParams(
            dimension_semantics=("arbitrary",), vmem_limit_bytes=VMEM_LIMIT),
        name="mix_sample_state",
    )(cols, rows, sg, sr, gam)
    o = o.reshape(s, GLA_V + RET_V)

    y = pl.pallas_call(
        _mix_sample_out_kernel,
        out_shape=jax.ShapeDtypeStruct((s, d), F32),
        compiler_params=pltpu.CompilerParams(vmem_limit_bytes=VMEM_LIMIT),
        name="mix_sample_out",
    )(x, o, gates, ggla, gret, bret, wout, lng, lnb)
    return y, nsg, nsr


def _prep_mix_in(w):
    sizes = (GLA_QK, GLA_QK, GLA_V, GLA_RANK, GLA_V, RET_QK, RET_QK, RET_V, RET_V)
    offs = [0]
    for sz in sizes:
        offs.append(offs[-1] + sz)
    part = lambda i: w[:, offs[i]:offs[i + 1]]
    lr = jnp.pad(part(3), ((0, 0), (0, PAD_RANK - GLA_RANK)))
    return jnp.concatenate([part(0), part(1), part(2), part(4), part(5), part(6), part(7), part(8), lr],
                           axis=1).astype(BF16)


def _prep_ffn(w_in, w_dw, b_dw, w_out):
    d = w_in.shape[0]
    nfc = D_FF // FF_CHUNK
    a = w_in[:, :D_FF].reshape(d, nfc, FF_CHUNK)
    g = w_in[:, D_FF:].reshape(d, nfc, FF_CHUNK)
    win = jnp.concatenate([a, g], axis=2).transpose(1, 0, 2).astype(BF16)
    regroup = lambda v: jnp.concatenate(
        [v[:, :D_FF].reshape(-1, nfc, FF_CHUNK), v[:, D_FF:].reshape(-1, nfc, FF_CHUNK)],
        axis=2).transpose(1, 0, 2)
    wdw = regroup(w_dw)
    bdw = regroup(b_dw[None, :])
    wout = w_out.reshape(nfc, FF_CHUNK, d).astype(BF16)
    return win, wdw, bdw, wout


def _prep_sc(w_in, w_dw, b_dw, w_out):
    d = w_in.shape[0]
    nc = d // SC_CHUNK
    parts = [w_in[:, k * d:(k + 1) * d].reshape(d, nc, SC_CHUNK) for k in range(3)]
    win = jnp.concatenate(parts, axis=2).transpose(1, 0, 2).astype(BF16)
    wdw = w_dw.reshape(CONV_W, nc, SC_CHUNK).transpose(1, 0, 2)
    bdw = b_dw.reshape(1, nc, SC_CHUNK).transpose(1, 0, 2)
    wout = w_out.reshape(nc, SC_CHUNK, d).astype(BF16)
    return win, wdw, bdw, wout


def kernel(x_prompt, x_sample, state_gla, state_ret, state_conv, state_ffn,
           w_mix_in, w_gla_gate_up, b_gla_gate, g_gla_norm, g_ret_norm, b_ret_norm, w_mix_out,
           w_sc_in, w_sc_dw, b_sc_dw, w_sc_out, ln1_g, ln1_b,
           w_ffn_in, w_ffn_dw, b_ffn_dw, w_ffn_out, ln2_g, ln2_b):
    bp, tp, d = x_prompt.shape
    s, ts, _ = x_sample.shape
    assert ts == 1 and d == D_MODEL
    xp = x_prompt.reshape(bp * tp, d)
    xs = x_sample.reshape(s, d)

    tables = _mix_tables(tp)
    cos_s, sin_s = _rotary_tables(PAST_LEN + jnp.arange(ts, dtype=F32))
    log_gamma = jnp.log1p(-jnp.exp2(-5.0 - jnp.arange(RET_HEADS, dtype=F32)))
    gam = jnp.broadcast_to(jnp.exp(log_gamma)[:, None, None], (RET_HEADS, RET_DK, RET_DV))

    row = lambda v: v.reshape(1, -1)
    gla_p, gla_s, ret_p, ret_s, conv_p, conv_s, ffn_p, ffn_s = [], [], [], [], [], [], [], []
    for layer in range(DEPTH):
        i = layer // 2
        if layer % 2 == 0:
            win = _prep_mix_in(w_mix_in[i])
            wup = jnp.pad(w_gla_gate_up[i], ((0, PAD_RANK - GLA_RANK), (0, 0))).astype(BF16)
            wout = w_mix_out[i].astype(BF16)
            args = (win, wup, row(b_gla_gate[i]), row(g_gla_norm[i]), row(g_ret_norm[i]),
                    row(b_ret_norm[i]), wout, row(ln1_g[layer]), row(ln1_b[layer]))
            xp, sg, sr = _mix_prompt(xp, bp, tp, *args, tables)
            gla_p.append(sg)
            ret_p.append(sr)
            xs, sg, sr = _mix_sample(xs, state_gla[i], state_ret[i], *args, cos_s, sin_s, gam)
            gla_s.append(sg)
            ret_s.append(sr)
        else:
            sc_w = _prep_sc(w_sc_in[i], w_sc_dw[i], b_sc_dw[i], w_sc_out[i])
            xp, sc = _sc_prompt(xp, tp, *sc_w, row(ln1_g[layer]), row(ln1_b[layer]))
            conv_p.append(sc)
            xs, sc = _sc_sample(xs, state_conv[i].reshape(s, 2 * d), w_sc_in[i].astype(BF16),
                                w_sc_dw[i], row(b_sc_dw[i]), w_sc_out[i].astype(BF16),
                                row(ln1_g[layer]), row(ln1_b[layer]))
            conv_s.append(sc.reshape(s, 2, d))
        ffn_w = _prep_ffn(w_ffn_in[layer], w_ffn_dw[layer], b_ffn_dw[layer], w_ffn_out[layer])
        xp, sf = _ffn_prompt(xp, tp, *ffn_w, row(ln2_g[layer]), row(ln2_b[layer]))
        ffn_p.append(sf)
        xs, sf = _ffn_sample(xs, state_ffn[layer].reshape(s, 4 * D_FF), w_ffn_in[layer].astype(BF16),
                             w_ffn_dw[layer], row(b_ffn_dw[layer]), w_ffn_out[layer].astype(BF16),
                             row(ln2_g[layer]), row(ln2_b[layer]))
        ffn_s.append(sf.reshape(s, 2, 2 * D_FF))

    return (xp.reshape(bp, tp, d), xs.reshape(s, ts, d),
            jnp.stack(gla_p), jnp.stack(gla_s), jnp.stack(ret_p), jnp.stack(ret_s),
            jnp.stack(conv_p), jnp.stack(conv_s), jnp.stack(ffn_p), jnp.stack(ffn_s))
```

```python
import functools

import jax
import jax.numpy as jnp
from jax import lax
from jax.experimental import pallas as pl
from jax.experimental.pallas import tpu as pltpu

F32 = jnp.float32
BF16 = jnp.bfloat16

D_MODEL = 1024
DEPTH = 4
PAST_LEN = 16384
GLA_HEADS = 4
GLA_DK = 64
GLA_DV = 128
GLA_RANK = 16
GLA_TAU = 16.0
RET_HEADS = 4
RET_DK = 128
RET_DV = 128
CONV_W = 3
D_FF = D_MODEL * 11 // 4
ROPE_BASE = 10000.0
NORM_EPS = 1e-5
ALPHA = (2 * DEPTH) ** 0.25
GLA_QK = GLA_HEADS * GLA_DK
GLA_V = GLA_HEADS * GLA_DV
RET_QK = RET_HEADS * RET_DK
RET_V = RET_HEADS * RET_DV

LANES = 128
SUBLANES = 8
VMEM_LIMIT = 56 * 1024 * 1024

GLA_CHUNK = 64
GLA_SUB = 16
MIX_TILE = 256
ROW_TILE = 512
CONV_ROWS = 32
FF_CHUNK = 256
SC_CHUNK = 256
PAD_RANK = LANES

_OFF_QG = 0
_OFF_KG = _OFF_QG + GLA_QK
_OFF_VG = _OFF_KG + GLA_QK
_OFF_RG = _OFF_VG + GLA_V
_OFF_QR = _OFF_RG + GLA_V
_OFF_KR = _OFF_QR + RET_QK
_OFF_VR = _OFF_KR + RET_QK
_OFF_GR = _OFF_VR + RET_V
_OFF_LR = _OFF_GR + RET_V
MIX_COLS = _OFF_LR + PAD_RANK


def _dot(a, b):
    return jnp.dot(a, b, preferred_element_type=F32)


def _dot_nt(a, b):
    return lax.dot_general(a, b, (((1,), (1,)), ((), ())), preferred_element_type=F32)


def _layer_norm(x, g, b):
    mu = jnp.mean(x, axis=-1, keepdims=True)
    xc = x - mu
    var = jnp.mean(xc * xc, axis=-1, keepdims=True)
    return xc * lax.rsqrt(var + NORM_EPS) * g + b


def _silu(x):
    return x * jax.nn.sigmoid(x)


def _log_sigmoid(x):
    return jnp.minimum(x, 0.0) - jnp.log1p(jnp.exp(-jnp.abs(x)))


def _split3(x):
    hi = x.astype(BF16)
    r1 = x - hi.astype(F32)
    mid = r1.astype(BF16)
    lo = (r1 - mid.astype(F32)).astype(BF16)
    return hi, mid, lo


def _causal_conv3_blocks(rows, n_rows, prev, w, b, post):
    w0, w1, w2 = w[0:1], w[1:2], w[2:3]
    out = []
    for lo in range(0, n_rows, CONV_ROWS):
        head = prev if lo == 0 else rows(lo - SUBLANES, lo)
        s = jnp.concatenate([head, rows(lo, lo + CONV_ROWS)], axis=0)
        s1 = pltpu.roll(s, 1, 0)[SUBLANES:]
        s2 = pltpu.roll(s, 2, 0)[SUBLANES:]
        out.append(post(lo, b + w0 * s2 + w1 * s1 + w2 * s[SUBLANES:]))
    return jnp.concatenate(out, axis=0)


def _ffn_prompt_kernel(x_ref, win_ref, wdw_ref, bdw_ref, wout_ref, g_ref, b_ref,
                       y_ref, st_ref, xb_ref, carry_ref, acc_ref, *, tiles_per_seq, n_chunks):
    i = pl.program_id(0)

    @pl.when(i % tiles_per_seq == 0)
    def _():
        carry_ref[...] = jnp.zeros_like(carry_ref)

    tm = x_ref.shape[0]
    fc = wout_ref.shape[1]
    xb_ref[...] = x_ref[...].astype(BF16)
    acc_ref[...] = jnp.zeros_like(acc_ref)

    def up(f):
        return _dot(xb_ref[...], win_ref[f])

    def act(f, u):
        h = _causal_conv3_blocks(
            lambda lo, hi: u[lo:hi], tm, carry_ref[f], wdw_ref[f], bdw_ref[f],
            lambda lo, c: (_silu(c[:, fc:]) * c[:, :fc]).astype(BF16))
        tail = u[tm - SUBLANES:]
        carry_ref[f] = tail
        st_ref[0, f] = tail
        return h

    def down(f, h):
        acc_ref[...] += _dot(h, wout_ref[f])

    def body(f, carry):
        u, h_prev = carry
        u_next = up(f + 1)
        h = act(f, u)
        down(f - 1, h_prev)
        return u_next, h

    u1 = up(1)
    h0 = act(0, up(0))
    u_last, h_prev = lax.fori_loop(1, n_chunks - 1, body, (u1, h0))
    h_last = act(n_chunks - 1, u_last)
    down(n_chunks - 2, h_prev)
    down(n_chunks - 1, h_last)
    y_ref[...] = _layer_norm(ALPHA * x_ref[...] + acc_ref[...], g_ref[...], b_ref[...])


def _ffn_prompt(x, seq_len, win, wdw, bdw, wout, g, b):
    n, d = x.shape
    tm = min(ROW_TILE, seq_len)
    assert seq_len % tm == 0 and n % seq_len == 0
    n_seq = n // seq_len
    tps = seq_len // tm
    nfc, _, fc2 = win.shape
    fc = fc2 // 2
    const3 = lambda i: (0, 0, 0)
    const2 = lambda i: (0, 0)
    y, st = pl.pallas_call(
        functools.partial(_ffn_prompt_kernel, tiles_per_seq=tps, n_chunks=nfc),
        grid=(n // tm,),
        in_specs=[
            pl.BlockSpec((tm, d), lambda i: (i, 0)),
            pl.BlockSpec((nfc, d, fc2), const3, pipeline_mode=pl.Buffered(1)),
            pl.BlockSpec((nfc, CONV_W, fc2), const3),
            pl.BlockSpec((nfc, 1, fc2), const3),
            pl.BlockSpec((nfc, fc, d), const3, pipeline_mode=pl.Buffered(1)),
            pl.BlockSpec((1, d), const2),
            pl.BlockSpec((1, d), const2),
        ],
        out_specs=[
            pl.BlockSpec((tm, d), lambda i: (i, 0)),
            pl.BlockSpec((1, nfc, SUBLANES, fc2), lambda i: (i // tps, 0, 0, 0)),
        ],
        out_shape=[
            jax.ShapeDtypeStruct((n, d), F32),
            jax.ShapeDtypeStruct((n_seq, nfc, SUBLANES, fc2), F32),
        ],
        scratch_shapes=[
            pltpu.VMEM((tm, d), BF16),
            pltpu.VMEM((nfc, SUBLANES, fc2), F32),
            pltpu.VMEM((tm, d), F32),
        ],
        compiler_params=pltpu.CompilerParams(
            dimension_semantics=("arbitrary",), vmem_limit_bytes=VMEM_LIMIT),
        name="ffn_prompt",
    )(x, win, wdw, bdw, wout, g, b)
    st = st[:, :, SUBLANES - 2:, :].reshape(n_seq, nfc, 2, 2, fc)
    st = st.transpose(0, 2, 3, 1, 4).reshape(n_seq, 2, 2 * nfc * fc)
    return y, st


def _sc_prompt_kernel(x_ref, win_ref, wdw_ref, bdw_ref, wout_ref, g_ref, b_ref,
                      y_ref, st_ref, xb_ref, carry_ref, acc_ref, *, tiles_per_seq, n_chunks):
    i = pl.program_id(0)

    @pl.when(i % tiles_per_seq == 0)
    def _():
        carry_ref[...] = jnp.zeros_like(carry_ref)

    tm = x_ref.shape[0]
    cw = wout_ref.shape[1]
    xb_ref[...] = x_ref[...].astype(BF16)
    acc_ref[...] = jnp.zeros_like(acc_ref)

    def up(f):
        return _dot(xb_ref[...], win_ref[f])

    def act(f, z):
        ch_rows = lambda lo, hi: z[lo:hi, cw:2 * cw] * z[lo:hi, 2 * cw:]
        m = _causal_conv3_blocks(
            ch_rows, tm, carry_ref[f], wdw_ref[f], bdw_ref[f],
            lambda lo, conv: (z[lo:lo + CONV_ROWS, :cw] * conv).astype(BF16))
        tail = ch_rows(tm - SUBLANES, tm)
        carry_ref[f] = tail
        st_ref[0, f] = tail
        return m

    def down(f, m):
        acc_ref[...] += _dot(m, wout_ref[f])

    def body(f, carry):
        z, m_prev = carry
        z_next = up(f + 1)
        m = act(f, z)
        down(f - 1, m_prev)
        return z_next, m

    z1 = up(1)
    m0 = act(0, up(0))
    z_last, m_prev = lax.fori_loop(1, n_chunks - 1, body, (z1, m0))
    m_last = act(n_chunks - 1, z_last)
    down(n_chunks - 2, m_prev)
    down(n_chunks - 1, m_last)
    y_ref[...] = _layer_norm(ALPHA * x_ref[...] + acc_ref[...], g_ref[...], b_ref[...])


def _sc_prompt(x, seq_len, win, wdw, bdw, wout, g, b):
    n, d = x.shape
    tm = min(ROW_TILE, seq_len)
    assert seq_len % tm == 0 and n % seq_len == 0
    n_seq = n // seq_len
    tps = seq_len // tm
    nc, _, cw3 = win.shape
    cw = cw3 // 3
    const3 = lambda i: (0, 0, 0)
    const2 = lambda i: (0, 0)
    y, st = pl.pallas_call(
        functools.partial(_sc_prompt_kernel, tiles_per_seq=tps, n_chunks=nc),
        grid=(n // tm,),
        in_specs=[
            pl.BlockSpec((tm, d), lambda i: (i, 0)),
            pl.BlockSpec((nc, d, cw3), const3, pipeline_mode=pl.Buffered(1)),
            pl.BlockSpec((nc, CONV_W, cw), const3),
            pl.BlockSpec((nc, 1, cw), const3),
            pl.BlockSpec((nc, cw, d), const3, pipeline_mode=pl.Buffered(1)),
            pl.BlockSpec((1, d), const2),
            pl.BlockSpec((1, d), const2),
        ],
        out_specs=[
            pl.BlockSpec((tm, d), lambda i: (i, 0)),
            pl.BlockSpec((1, nc, SUBLANES, cw), lambda i: (i // tps, 0, 0, 0)),
        ],
        out_shape=[
            jax.ShapeDtypeStruct((n, d), F32),
            jax.ShapeDtypeStruct((n_seq, nc, SUBLANES, cw), F32),
        ],
        scratch_shapes=[
            pltpu.VMEM((tm, d), BF16),
            pltpu.VMEM((nc, SUBLANES, cw), F32),
            pltpu.VMEM((tm, d), F32),
        ],
        compiler_params=pltpu.CompilerParams(
            dimension_semantics=("arbitrary",), vmem_limit_bytes=VMEM_LIMIT),
        name="sc_prompt",
    )(x, win, wdw, bdw, wout, g, b)
    st = st[:, :, SUBLANES - 2:, :].transpose(0, 2, 1, 3).reshape(n_seq, 2, nc * cw)
    return y, st


def _rotary(x, cosf, sinf):
    return x * cosf + pltpu.roll(x, RET_DK // 2, 1) * sinf


def _gla_chunk(q_c, k_c, v_c, wn, r, st, mask_q, masks_p, mask_bd, sblock):
    c = q_c.shape[0]
    n_sub = c // GLA_SUB
    cum = r + wn
    tot = cum[c - 1:c, :]
    qw = q_c * jnp.exp(wn)
    qc = q_c * jnp.exp(cum)
    kd = k_c * jnp.exp(tot - cum)
    vst = jnp.concatenate([v_c[:, h * GLA_DV:(h + 1) * GLA_DV] for h in range(GLA_HEADS)], axis=0)
    vst_b = vst.astype(BF16)
    intra = []
    for i in range(n_sub):
        r_i = r[i * GLA_SUB:i * GLA_SUB + 1, :]
        kt = jnp.where(sblock <= i, k_c * jnp.exp(r_i - cum), 0.0)
        kbig = jnp.concatenate([kt] * GLA_HEADS, axis=0).astype(BF16)
        q16 = qw[i * GLA_SUB:(i + 1) * GLA_SUB]
        qs = jnp.where(mask_q, jnp.concatenate([q16] * GLA_HEADS, axis=0), 0.0).astype(BF16)
        sc = _dot_nt(qs, kbig)
        p = jnp.where(masks_p[i], sc, 0.0).astype(BF16)
        intra.append(_dot(p, vst_b))
    qcs = jnp.where(mask_bd, jnp.concatenate([qc] * GLA_HEADS, axis=0), 0.0).astype(BF16)
    inter = _dot_nt(qcs, st.astype(BF16))
    heads = []
    for h in range(GLA_HEADS):
        intra_h = jnp.concatenate([intra[i][h * GLA_SUB:(h + 1) * GLA_SUB] for i in range(n_sub)], axis=0)
        heads.append(intra_h + inter[h * c:(h + 1) * c])
    o = jnp.concatenate(heads, axis=1)
    kdbig = jnp.where(mask_bd, jnp.concatenate([kd] * GLA_HEADS, axis=0), 0.0).astype(BF16)
    st_new = jnp.exp(tot) * st + _dot(vst.T.astype(BF16), kdbig)
    return o, st_new


def _mix_prompt_kernel(x_ref, win_ref, wup_ref, bgate_ref, ggla_ref, gret_ref, bret_ref, wout_ref,
                       lng_ref, lnb_ref, cos_ref, sin_ref, mcat_ref, dmask_ref, qdec_ref, kdec_ref,
                       y_ref, sg_ref, sr_ref, stg_ref, str_ref):
    j = pl.program_id(1)
    nj = pl.num_programs(1)

    @pl.when(j == 0)
    def _():
        stg_ref[...] = jnp.zeros_like(stg_ref)
        str_ref[...] = jnp.zeros_like(str_ref)

    tc = x_ref.shape[0]
    x = x_ref[...]
    xb = x.astype(BF16)

    def proj(off, width):
        return _dot(xb, win_ref[:, off:off + width])

    qg = proj(_OFF_QG, GLA_QK) * (GLA_DK ** -0.5)
    kg = proj(_OFF_KG, GLA_QK)
    vg = proj(_OFF_VG, GLA_V)
    lr = proj(_OFF_LR, PAD_RANK)
    gate = _dot(lr.astype(BF16), wup_ref[...]) + bgate_ref[...]
    lg = _log_sigmoid(gate) * (1.0 / GLA_TAU)
    hi, mid, lo = _split3(lg)
    cs = _dot(mcat_ref[...], jnp.concatenate([hi, mid, lo], axis=1))
    cs = cs[:, :GLA_QK] + cs[:, GLA_QK:2 * GLA_QK] + cs[:, 2 * GLA_QK:]
    wn_all = cs[:tc]
    r_all = cs[tc:]

    c = GLA_CHUNK
    hq = GLA_HEADS * GLA_SUB
    row64 = lax.broadcasted_iota(jnp.int32, (hq, GLA_QK), 0)
    lane64 = lax.broadcasted_iota(jnp.int32, (hq, GLA_QK), 1)
    lane_head = lax.shift_right_logical(lane64, 6)
    row_head = lax.shift_right_logical(row64, 4)
    mask_q = lane_head == row_head
    masks_p = [mask_q & ((lane64 & (c - 1)) <= (row64 & (GLA_SUB - 1)) + i * GLA_SUB)
               for i in range(c // GLA_SUB)]
    rowb = lax.broadcasted_iota(jnp.int32, (GLA_HEADS * c, GLA_QK), 0)
    laneb = lax.broadcasted_iota(jnp.int32, (GLA_HEADS * c, GLA_QK), 1)
    mask_bd = lax.shift_right_logical(rowb, 6) == lax.shift_right_logical(laneb, 6)
    sblock = lax.shift_right_logical(lax.broadcasted_iota(jnp.int32, (c, GLA_QK), 0), 4)

    st = stg_ref[...]
    og_chunks = []
    for ci in range(tc // c):
        sl = slice(ci * c, (ci + 1) * c)
        o_c, st = _gla_chunk(qg[sl], kg[sl], vg[sl], wn_all[sl], r_all[sl], st,
                             mask_q, masks_p, mask_bd, sblock)
        og_chunks.append(o_c)
    stg_ref[...] = st
    og = jnp.concatenate(og_chunks, axis=0)
    rg = proj(_OFF_RG, GLA_V)
    ggla = ggla_ref[...]
    og_heads = []
    for h in range(GLA_HEADS):
        sl = slice(h * GLA_DV, (h + 1) * GLA_DV)
        o = og[:, sl]
        o = o * lax.rsqrt(jnp.mean(o * o, axis=-1, keepdims=True) + NORM_EPS) * ggla
        og_heads.append(o * _silu(rg[:, sl]))

    cosf = cos_ref[...]
    sinf = sin_ref[...]
    or_heads = []
    qr_all = proj(_OFF_QR, RET_QK)
    kr_all = proj(_OFF_KR, RET_QK)
    vr_all = proj(_OFF_VR, RET_V)
    gr_all = proj(_OFF_GR, RET_V)
    for h in range(RET_HEADS):
        sl = slice(h * RET_DK, (h + 1) * RET_DK)
        qh = _rotary(qr_all[:, sl], cosf, sinf)
        kh = _rotary(kr_all[:, sl], cosf, sinf) * (RET_DK ** -0.5)
        vh = vr_all[:, sl].astype(BF16)
        gh = gr_all[:, sl]
        qdec = qdec_ref[h]
        s_h = str_ref[h]
        sc = _dot_nt(qh.astype(BF16), kh.astype(BF16)) * dmask_ref[h]
        o = _dot(sc.astype(BF16), vh) + _dot((qh * qdec).astype(BF16), s_h.astype(BF16))
        str_ref[h] = qdec[tc - 1:tc, :] * s_h + _dot((kh * kdec_ref[h]).T.astype(BF16), vh)
        mu = jnp.mean(o, axis=-1, keepdims=True)
        oc = o - mu
        var = jnp.mean(oc * oc, axis=-1, keepdims=True)
        o = oc * lax.rsqrt(var + NORM_EPS) * gret_ref[:, sl] + bret_ref[:, sl]
        or_heads.append(o * _silu(gh))

    mixed = jnp.concatenate(og_heads + or_heads, axis=1).astype(BF16)
    y = _dot(mixed, wout_ref[...])
    y_ref[...] = _layer_norm(ALPHA * x + y, lng_ref[...], lnb_ref[...])

    @pl.when(j == nj - 1)
    def _():
        sg_ref[0] = stg_ref[...].T
        sr_ref[0] = str_ref[...]


def _mix_prompt(x, n_seq, seq_len, win, wup, bgate, ggla, gret, bret, wout, lng, lnb, tables):
    n, d = x.shape
    tc = min(MIX_TILE, seq_len)
    assert seq_len % tc == 0 and tc % GLA_CHUNK == 0
    nt = seq_len // tc
    cosf, sinf, mcat, dmask, qdec, kdec = tables
    c2 = lambda b, j: (0, 0)
    c3 = lambda b, j: (0, 0, 0)
    y, sg, sr = pl.pallas_call(
        _mix_prompt_kernel,
        grid=(n_seq, nt),
        in_specs=[
            pl.BlockSpec((tc, d), lambda b, j: (b * nt + j, 0)),
            pl.BlockSpec((d, MIX_COLS), c2, pipeline_mode=pl.Buffered(1)),
            pl.BlockSpec((PAD_RANK, GLA_QK), c2),
            pl.BlockSpec((1, GLA_QK), c2),
            pl.BlockSpec((1, GLA_DV), c2),
            pl.BlockSpec((1, RET_V), c2),
            pl.BlockSpec((1, RET_V), c2),
            pl.BlockSpec((GLA_V + RET_V, d), c2, pipeline_mode=pl.Buffered(1)),
            pl.BlockSpec((1, d), c2),
            pl.BlockSpec((1, d), c2),
            pl.BlockSpec((tc, RET_DK), lambda b, j: (j, 0)),
            pl.BlockSpec((tc, RET_DK), lambda b, j: (j, 0)),
            pl.BlockSpec((2 * tc, tc), c2),
            pl.BlockSpec((RET_HEADS, tc, tc), c3),
            pl.BlockSpec((RET_HEADS, tc, RET_DK), c3),
            pl.BlockSpec((RET_HEADS, tc, RET_DK), c3),
        ],
        out_specs=[
            pl.BlockSpec((tc, d), lambda b, j: (b * nt + j, 0)),
            pl.BlockSpec((1, GLA_QK, GLA_DV), lambda b, j: (b, 0, 0)),
            pl.BlockSpec((1, RET_HEADS, RET_DK, RET_DV), lambda b, j: (b, 0, 0, 0)),
        ],
        out_shape=[
            jax.ShapeDtypeStruct((n, d), F32),
            jax.ShapeDtypeStruct((n_seq, GLA_QK, GLA_DV), F32),
            jax.ShapeDtypeStruct((n_seq, RET_HEADS, RET_DK, RET_DV), F32),
        ],
        scratch_shapes=[
            pltpu.VMEM((GLA_DV, GLA_QK), F32),
            pltpu.VMEM((RET_HEADS, RET_DK, RET_DV), F32),
        ],
        compiler_params=pltpu.CompilerParams(
            dimension_semantics=("arbitrary", "arbitrary"), vmem_limit_bytes=VMEM_LIMIT),
        name="mix_prompt",
    )(x, win, wup, bgate, ggla, gret, bret, wout, lng, lnb, cosf, sinf, mcat, dmask, qdec, kdec)
    return y, sg.reshape(n_seq, GLA_HEADS, GLA_DK, GLA_DV), sr


def _mix_tables(seq_len):
    tc = min(MIX_TILE, seq_len)
    pos = jnp.arange(seq_len, dtype=F32)
    cosf, sinf = _rotary_tables(pos)
    t = jnp.arange(tc)
    same_sub = (t[:, None] // GLA_SUB) == (t[None, :] // GLA_SUB)
    same_chunk = (t[:, None] // GLA_CHUNK) == (t[None, :] // GLA_CHUNK)
    m_within = same_sub & (t[None, :] <= t[:, None])
    m_before = same_chunk & ((t[None, :] // GLA_SUB) < (t[:, None] // GLA_SUB))
    mcat = jnp.concatenate([m_within, m_before], axis=0).astype(BF16)
    log_gamma = jnp.log1p(-jnp.exp2(-5.0 - jnp.arange(RET_HEADS, dtype=F32)))
    tf = t.astype(F32)
    diff = tf[:, None] - tf[None, :]
    dmask = jnp.where(diff >= 0, jnp.exp(diff[None] * log_gamma[:, None, None]), 0.0)
    qdec = jnp.exp((tf[None, :, None] + 1.0) * log_gamma[:, None, None])
    kdec = jnp.exp((tc - 1.0 - tf[None, :, None]) * log_gamma[:, None, None])
    qdec = jnp.broadcast_to(qdec, (RET_HEADS, tc, RET_DK))
    kdec = jnp.broadcast_to(kdec, (RET_HEADS, tc, RET_DK))
    return cosf, sinf, mcat, dmask, qdec, kdec


def _rotary_tables(pos):
    inv_freq = 1.0 / (ROPE_BASE ** (jnp.arange(0, RET_DK, 2, dtype=F32) / RET_DK))
    ang = pos[:, None] * inv_freq[None, :]
    cos, sin = jnp.cos(ang), jnp.sin(ang)
    return jnp.concatenate([cos, cos], axis=1), jnp.concatenate([-sin, sin], axis=1)


def _ffn_sample_kernel(x_ref, st_ref, win_ref, wdw_ref, bdw_ref, wout_ref, g_ref, b_ref,
                       y_ref, nst_ref):
    x = x_ref[...]
    xb = x.astype(BF16)
    w2 = 2 * D_FF
    nfc, fc = wout_ref.shape[0], wout_ref.shape[1]
    nst_ref[:, :w2] = st_ref[:, w2:]
    y = jnp.zeros_like(x)
    for f in range(nfc):
        u = _dot(xb, win_ref[f])
        wdw = wdw_ref[f]
        halves = []
        for k in range(2):
            lo = k * D_FF + f * fc
            uk = u[:, k * fc:(k + 1) * fc]
            nst_ref[:, w2 + lo:w2 + lo + fc] = uk
            halves.append(bdw_ref[f][:, k * fc:(k + 1) * fc]
                          + wdw[0:1, k * fc:(k + 1) * fc] * st_ref[:, lo:lo + fc]
                          + wdw[1:2, k * fc:(k + 1) * fc] * st_ref[:, w2 + lo:w2 + lo + fc]
                          + wdw[2:3, k * fc:(k + 1) * fc] * uk)
        h = _silu(halves[1]) * halves[0]
        y = y + _dot(h.astype(BF16), wout_ref[f])
    y_ref[...] = _layer_norm(ALPHA * x + y, g_ref[...], b_ref[...])


def _ffn_sample(x, st, win, wdw, bdw, wout, g, b):
    s, d = x.shape
    return pl.pallas_call(
        _ffn_sample_kernel,
        out_shape=[jax.ShapeDtypeStruct((s, d), F32), jax.ShapeDtypeStruct(st.shape, F32)],
        compiler_params=pltpu.CompilerParams(vmem_limit_bytes=VMEM_LIMIT),
        name="ffn_sample",
    )(x, st, win, wdw, bdw, wout, g, b)


def _sc_sample_kernel(x_ref, st_ref, win_ref, wdw_ref, bdw_ref, wout_ref, g_ref, b_ref,
                      y_ref, nst_ref):
    x = x_ref[...]
    xb = x.astype(BF16)
    d = x.shape[1]
    nc, cw = wout_ref.shape[0], wout_ref.shape[1]
    nst_ref[:, :d] = st_ref[:, d:]
    y = jnp.zeros_like(x)
    for f in range(nc):
        z = _dot(xb, win_ref[f])
        ch = z[:, cw:2 * cw] * z[:, 2 * cw:]
        cols = slice(f * cw, (f + 1) * cw)
        nst_ref[:, d + f * cw:d + (f + 1) * cw] = ch
        wdw = wdw_ref[f]
        conv = (bdw_ref[f] + wdw[0:1] * st_ref[:, cols]
                + wdw[1:2] * st_ref[:, d + f * cw:d + (f + 1) * cw] + wdw[2:3] * ch)
        y = y + _dot((z[:, :cw] * conv).astype(BF16), wout_ref[f])
    y_ref[...] = _layer_norm(ALPHA * x + y, g_ref[...], b_ref[...])


def _sc_sample(x, st, win, wdw, bdw, wout, g, b):
    s, d = x.shape
    return pl.pallas_call(
        _sc_sample_kernel,
        out_shape=[jax.ShapeDtypeStruct((s, d), F32), jax.ShapeDtypeStruct(st.shape, F32)],
        compiler_params=pltpu.CompilerParams(vmem_limit_bytes=VMEM_LIMIT),
        name="sc_sample",
    )(x, st, win, wdw, bdw, wout, g, b)


def _mix_sample_proj_kernel(x_ref, win_ref, wup_ref, bgate_ref, cos_ref, sin_ref,
                            cols_ref, rows_ref, gates_ref):
    xb = x_ref[...].astype(BF16)

    def proj(off, width):
        return _dot(xb, win_ref[:, off:off + width])

    qg = proj(_OFF_QG, GLA_QK) * (GLA_DK ** -0.5)
    kg = proj(_OFF_KG, GLA_QK)
    lr = proj(_OFF_LR, PAD_RANK)
    gate = _dot(lr.astype(BF16), wup_ref[...]) + bgate_ref[...]
    ag = jnp.exp(_log_sigmoid(gate) * (1.0 / GLA_TAU))
    cosf = cos_ref[...]
    sinf = sin_ref[...]
    qr = [_rotary(proj(_OFF_QR + h * RET_DK, RET_DK), cosf, sinf) for h in range(RET_HEADS)]
    kr = [_rotary(proj(_OFF_KR + h * RET_DK, RET_DK), cosf, sinf) * (RET_DK ** -0.5)
          for h in range(RET_HEADS)]
    cols_ref[0:GLA_QK, :] = ag.T
    cols_ref[GLA_QK:2 * GLA_QK, :] = kg.T
    cols_ref[2 * GLA_QK:3 * GLA_QK, :] = qg.T
    base = 3 * GLA_QK
    for h in range(RET_HEADS):
        cols_ref[base + h * RET_DK:base + (h + 1) * RET_DK, :] = kr[h].T
        cols_ref[base + RET_QK + h * RET_DK:base + RET_QK + (h + 1) * RET_DK, :] = qr[h].T
    rows_ref[:, :GLA_V] = proj(_OFF_VG, GLA_V)
    rows_ref[:, GLA_V:] = proj(_OFF_VR, RET_V)
    gates_ref[:, :GLA_V] = proj(_OFF_RG, GLA_V)
    gates_ref[:, GLA_V:] = proj(_OFF_GR, RET_V)


_COL_ROWS = 3 * GLA_QK + 2 * RET_QK


def _mix_sample_state_kernel(*refs, n_stacked_in):
    cols_ref, rows_ref, sg_ref, sr_ref, gam_ref = refs[:5]
    nsg_ref, nsr_ref, o_ref = refs[5 + n_stacked_in:]
    nb = sg_ref.shape[0]
    s = cols_ref.shape[1]
    blk = pl.program_id(0)
    lane = lax.broadcasted_iota(jnp.int32, (1, s), 1)

    def body(bb, carry):
        b = blk * nb + bb
        onehot = (lane == b).astype(F32)
        col = jnp.sum(cols_ref[...] * onehot, axis=1, keepdims=True)
        vrow = rows_ref[b]
        a_g = col[0:GLA_QK]
        k_g = col[GLA_QK:2 * GLA_QK]
        q_g = col[2 * GLA_QK:3 * GLA_QK]
        base = 3 * GLA_QK
        outs = []
        for h in range(GLA_HEADS):
            rs = slice(h * GLA_DK, (h + 1) * GLA_DK)
            v_h = vrow[:, h * GLA_DV:(h + 1) * GLA_DV]
            s_new = a_g[rs] * sg_ref[bb, h] + k_g[rs] * v_h
            nsg_ref[bb, h] = s_new
            outs.append(jnp.sum(q_g[rs] * s_new, axis=0, keepdims=True))
        for h in range(RET_HEADS):
            rs = slice(h * RET_DK, (h + 1) * RET_DK)
            k_h = col[base + h * RET_DK:base + (h + 1) * RET_DK]
            q_h = col[base + RET_QK + h * RET_DK:base + RET_QK + (h + 1) * RET_DK]
            v_h = vrow[:, GLA_V + h * RET_DV:GLA_V + (h + 1) * RET_DV]
            s_new = gam_ref[h] * sr_ref[bb, h] + k_h * v_h
            nsr_ref[bb, h] = s_new
            outs.append(jnp.sum(q_h * s_new, axis=0, keepdims=True))
        o_ref[b] = jnp.concatenate(outs, axis=1)
        return carry

    lax.fori_loop(0, nb, body, 0)


def _mix_sample_out_kernel(x_ref, o_ref, gates_ref, ggla_ref, gret_ref, bret_ref, wout_ref,
                           lng_ref, lnb_ref, y_ref):
    x = x_ref[...]
    o = o_ref[...]
    gates = gates_ref[...]
    ggla = ggla_ref[...]
    heads = []
    for h in range(GLA_HEADS):
        sl = slice(h * GLA_DV, (h + 1) * GLA_DV)
        oh = o[:, sl]
        oh = oh * lax.rsqrt(jnp.mean(oh * oh, axis=-1, keepdims=True) + NORM_EPS) * ggla
        heads.append(oh * _silu(gates[:, sl]))
    for h in range(RET_HEADS):
        sl = slice(GLA_V + h * RET_DV, GLA_V + (h + 1) * RET_DV)
        ps = slice(h * RET_DV, (h + 1) * RET_DV)
        oh = o[:, sl]
        mu = jnp.mean(oh, axis=-1, keepdims=True)
        oc = oh - mu
        var = jnp.mean(oc * oc, axis=-1, keepdims=True)
        oh = oc * lax.rsqrt(var + NORM_EPS) * gret_ref[:, ps] + bret_ref[:, ps]
        heads.append(oh * _silu(gates[:, sl]))
    mixed = jnp.concatenate(heads, axis=1).astype(BF16)
    y = _dot(mixed, wout_ref[...])
    y_ref[...] = _layer_norm(ALPHA * x + y, lng_ref[...], lnb_ref[...])


def _mix_sample(x, state_gla, state_ret, li, stacked, win, wup, bgate, ggla, gret, bret, wout, lng, lnb,
                cos_s, sin_s, gam):
    s, d = x.shape
    cols, rows, gates = pl.pallas_call(
        _mix_sample_proj_kernel,
        out_shape=[
            jax.ShapeDtypeStruct((_COL_ROWS, s), F32),
            jax.ShapeDtypeStruct((s, GLA_V + RET_V), F32),
            jax.ShapeDtypeStruct((s, GLA_V + RET_V), F32),
        ],
        compiler_params=pltpu.CompilerParams(vmem_limit_bytes=VMEM_LIMIT),
        name="mix_sample_proj",
    )(x, win, wup, bgate, cos_s, sin_s)
    rows = rows.reshape(s, 1, GLA_V + RET_V)

    nb = SUBLANES
    assert s % nb == 0
    gla_spec = pl.BlockSpec((None, nb, GLA_HEADS, GLA_DK, GLA_DV), lambda i: (li, i, 0, 0, 0))
    ret_spec = pl.BlockSpec((None, nb, RET_HEADS, RET_DK, RET_DV), lambda i: (li, i, 0, 0, 0))
    stacked = () if stacked is None else tuple(stacked)
    nsg, nsr, o = pl.pallas_call(
        functools.partial(_mix_sample_state_kernel, n_stacked_in=len(stacked)),
        grid=(s // nb,),
        in_specs=[
            pl.BlockSpec((_COL_ROWS, s), lambda i: (0, 0)),
            pl.BlockSpec((s, 1, GLA_V + RET_V), lambda i: (0, 0, 0)),
            gla_spec,
            ret_spec,
            pl.BlockSpec((RET_HEADS, RET_DK, RET_DV), lambda i: (0, 0, 0)),
        ] + [pl.BlockSpec(memory_space=pl.ANY)] * len(stacked),
        out_specs=[
            gla_spec,
            ret_spec,
            pl.BlockSpec((s, 1, GLA_V + RET_V), lambda i: (0, 0, 0)),
        ],
        out_shape=[
            jax.ShapeDtypeStruct(state_gla.shape, F32),
            jax.ShapeDtypeStruct(state_ret.shape, F32),
            jax.ShapeDtypeStruct((s, 1, GLA_V + RET_V), F32),
        ],
        input_output_aliases={5 + k: k for k in range(len(stacked))},
        compiler_params=pltpu.CompilerParams(
            dimension_semantics=("arbitrary",), vmem_limit_bytes=VMEM_LIMIT),
        name="mix_sample_state",
    )(cols, rows, state_gla, state_ret, gam, *stacked)
    o = o.reshape(s, GLA_V + RET_V)

    y = pl.pallas_call(
        _mix_sample_out_kernel,
        out_shape=jax.ShapeDtypeStruct((s, d), F32),
        compiler_params=pltpu.CompilerParams(vmem_limit_bytes=VMEM_LIMIT),
        name="mix_sample_out",
    )(x, o, gates, ggla, gret, bret, wout, lng, lnb)
    return y, nsg, nsr


def _prep_mix_in(w):
    sizes = (GLA_QK, GLA_QK, GLA_V, GLA_RANK, GLA_V, RET_QK, RET_QK, RET_V, RET_V)
    offs = [0]
    for sz in sizes:
        offs.append(offs[-1] + sz)
    part = lambda i: w[:, offs[i]:offs[i + 1]]
    lr = jnp.pad(part(3), ((0, 0), (0, PAD_RANK - GLA_RANK)))
    return jnp.concatenate([part(0), part(1), part(2), part(4), part(5), part(6), part(7), part(8), lr],
                           axis=1).astype(BF16)


def _prep_ffn(w_in, w_dw, b_dw, w_out):
    d = w_in.shape[0]
    nfc = D_FF // FF_CHUNK
    a = w_in[:, :D_FF].reshape(d, nfc, FF_CHUNK)
    g = w_in[:, D_FF:].reshape(d, nfc, FF_CHUNK)
    win = jnp.concatenate([a, g], axis=2).transpose(1, 0, 2).astype(BF16)
    regroup = lambda v: jnp.concatenate(
        [v[:, :D_FF].reshape(-1, nfc, FF_CHUNK), v[:, D_FF:].reshape(-1, nfc, FF_CHUNK)],
        axis=2).transpose(1, 0, 2)
    wdw = regroup(w_dw)
    bdw = regroup(b_dw[None, :])
    wout = w_out.reshape(nfc, FF_CHUNK, d).astype(BF16)
    return win, wdw, bdw, wout


def _prep_sc(w_in, w_dw, b_dw, w_out):
    d = w_in.shape[0]
    nc = d // SC_CHUNK
    parts = [w_in[:, k * d:(k + 1) * d].reshape(d, nc, SC_CHUNK) for k in range(3)]
    win = jnp.concatenate(parts, axis=2).transpose(1, 0, 2).astype(BF16)
    wdw = w_dw.reshape(CONV_W, nc, SC_CHUNK).transpose(1, 0, 2)
    bdw = b_dw.reshape(1, nc, SC_CHUNK).transpose(1, 0, 2)
    wout = w_out.reshape(nc, SC_CHUNK, d).astype(BF16)
    return win, wdw, bdw, wout


def kernel(x_prompt, x_sample, state_gla, state_ret, state_conv, state_ffn,
           w_mix_in, w_gla_gate_up, b_gla_gate, g_gla_norm, g_ret_norm, b_ret_norm, w_mix_out,
           w_sc_in, w_sc_dw, b_sc_dw, w_sc_out, ln1_g, ln1_b,
           w_ffn_in, w_ffn_dw, b_ffn_dw, w_ffn_out, ln2_g, ln2_b):
    bp, tp, d = x_prompt.shape
    s, ts, _ = x_sample.shape
    assert ts == 1 and d == D_MODEL
    xp = x_prompt.reshape(bp * tp, d)
    xs = x_sample.reshape(s, d)

    tables = _mix_tables(tp)
    cos_s, sin_s = _rotary_tables(PAST_LEN + jnp.arange(ts, dtype=F32))
    log_gamma = jnp.log1p(-jnp.exp2(-5.0 - jnp.arange(RET_HEADS, dtype=F32)))
    gam = jnp.broadcast_to(jnp.exp(log_gamma)[:, None, None], (RET_HEADS, RET_DK, RET_DV))

    row = lambda v: v.reshape(1, -1)
    gla_p, ret_p, conv_p, conv_s, ffn_p, ffn_s = [], [], [], [], [], []
    stacked_s = None
    for layer in range(DEPTH):
        i = layer // 2
        if layer % 2 == 0:
            win = _prep_mix_in(w_mix_in[i])
            wup = jnp.pad(w_gla_gate_up[i], ((0, PAD_RANK - GLA_RANK), (0, 0))).astype(BF16)
            wout = w_mix_out[i].astype(BF16)
            args = (win, wup, row(b_gla_gate[i]), row(g_gla_norm[i]), row(g_ret_norm[i]),
                    row(b_ret_norm[i]), wout, row(ln1_g[layer]), row(ln1_b[layer]))
            xp, sg, sr = _mix_prompt(xp, bp, tp, *args, tables)
            gla_p.append(sg)
            ret_p.append(sr)
            xs, gla_s, ret_s = _mix_sample(xs, state_gla, state_ret, i, stacked_s, *args, cos_s, sin_s, gam)
            stacked_s = (gla_s, ret_s)
        else:
            sc_w = _prep_sc(w_sc_in[i], w_sc_dw[i], b_sc_dw[i], w_sc_out[i])
            xp, sc = _sc_prompt(xp, tp, *sc_w, row(ln1_g[layer]), row(ln1_b[layer]))
            conv_p.append(sc)
            xs, sc = _sc_sample(xs, state_conv[i].reshape(s, 2 * d), *sc_w,
                                row(ln1_g[layer]), row(ln1_b[layer]))
            conv_s.append(sc.reshape(s, 2, d))
        ffn_w = _prep_ffn(w_ffn_in[layer], w_ffn_dw[layer], b_ffn_dw[layer], w_ffn_out[layer])
        xp, sf = _ffn_prompt(xp, tp, *ffn_w, row(ln2_g[layer]), row(ln2_b[layer]))
        ffn_p.append(sf)
        xs, sf = _ffn_sample(xs, state_ffn[layer].reshape(s, 4 * D_FF), *ffn_w,
                             row(ln2_g[layer]), row(ln2_b[layer]))
        ffn_s.append(sf.reshape(s, 2, 2 * D_FF))

    return (xp.reshape(bp, tp, d), xs.reshape(s, ts, d),
            jnp.stack(gla_p), gla_s, jnp.stack(ret_p), ret_s,
            jnp.stack(conv_p), jnp.stack(conv_s), jnp.stack(ffn_p), jnp.stack(ffn_s))
```

```python
import functools

import jax
import jax.numpy as jnp
from jax import lax
from jax.experimental import pallas as pl
from jax.experimental.pallas import tpu as pltpu

F32 = jnp.float32
BF16 = jnp.bfloat16

D_MODEL = 1024
DEPTH = 4
PAST_LEN = 16384
GLA_HEADS = 4
GLA_DK = 64
GLA_DV = 128
GLA_RANK = 16
GLA_TAU = 16.0
RET_HEADS = 4
RET_DK = 128
RET_DV = 128
CONV_W = 3
D_FF = D_MODEL * 11 // 4
ROPE_BASE = 10000.0
NORM_EPS = 1e-5
ALPHA = (2 * DEPTH) ** 0.25
GLA_QK = GLA_HEADS * GLA_DK
GLA_V = GLA_HEADS * GLA_DV
RET_QK = RET_HEADS * RET_DK
RET_V = RET_HEADS * RET_DV

LANES = 128
SUBLANES = 8
VMEM_LIMIT = 56 * 1024 * 1024

GLA_CHUNK = 64
GLA_SUB = 16
MIX_TILE = 256
ROW_TILE = 512
CONV_ROWS = 32
FF_CHUNK = 256
SC_CHUNK = 256
PAD_RANK = LANES

_OFF_QG = 0
_OFF_KG = _OFF_QG + GLA_QK
_OFF_VG = _OFF_KG + GLA_QK
_OFF_RG = _OFF_VG + GLA_V
_OFF_QR = _OFF_RG + GLA_V
_OFF_KR = _OFF_QR + RET_QK
_OFF_VR = _OFF_KR + RET_QK
_OFF_GR = _OFF_VR + RET_V
_OFF_LR = _OFF_GR + RET_V
MIX_COLS = _OFF_LR + PAD_RANK


def _dot(a, b):
    return jnp.dot(a, b, preferred_element_type=F32)


def _dot_nt(a, b):
    return lax.dot_general(a, b, (((1,), (1,)), ((), ())), preferred_element_type=F32)


def _layer_norm(x, g, b):
    mu = jnp.mean(x, axis=-1, keepdims=True)
    xc = x - mu
    var = jnp.mean(xc * xc, axis=-1, keepdims=True)
    return xc * lax.rsqrt(var + NORM_EPS) * g + b


def _silu(x):
    return x * jax.nn.sigmoid(x)


def _log_sigmoid(x):
    return jnp.minimum(x, 0.0) - jnp.log1p(jnp.exp(-jnp.abs(x)))


def _split3(x):
    hi = x.astype(BF16)
    r1 = x - hi.astype(F32)
    mid = r1.astype(BF16)
    lo = (r1 - mid.astype(F32)).astype(BF16)
    return hi, mid, lo


def _causal_conv3_blocks(rows, n_rows, prev, w, b, post):
    w0, w1, w2 = w[0:1], w[1:2], w[2:3]
    out = []
    for lo in range(0, n_rows, CONV_ROWS):
        head = prev if lo == 0 else rows(lo - SUBLANES, lo)
        s = jnp.concatenate([head, rows(lo, lo + CONV_ROWS)], axis=0)
        s1 = pltpu.roll(s, 1, 0)[SUBLANES:]
        s2 = pltpu.roll(s, 2, 0)[SUBLANES:]
        out.append(post(lo, b + w0 * s2 + w1 * s1 + w2 * s[SUBLANES:]))
    return jnp.concatenate(out, axis=0)


def _ffn_prompt_kernel(x_ref, win_ref, wdw_ref, bdw_ref, wout_ref, g_ref, b_ref,
                       y_ref, st_ref, xb_ref, carry_ref, acc_ref, *, tiles_per_seq):
    i = pl.program_id(0)

    @pl.when(i % tiles_per_seq == 0)
    def _():
        carry_ref[...] = jnp.zeros_like(carry_ref)

    tm = x_ref.shape[0]
    ff = wout_ref.shape[0]
    fc = FF_CHUNK
    n_chunks = ff // fc
    xb_ref[...] = x_ref[...].astype(BF16)
    acc_ref[...] = jnp.zeros_like(acc_ref)

    def cols(ref, f):
        return jnp.concatenate([ref[:, f * fc:(f + 1) * fc], ref[:, ff + f * fc:ff + (f + 1) * fc]], axis=1)

    def up(f):
        xb = xb_ref[...]
        return jnp.concatenate([_dot(xb, win_ref[:, f * fc:(f + 1) * fc]),
                                _dot(xb, win_ref[:, ff + f * fc:ff + (f + 1) * fc])], axis=1)

    def act(f, u):
        h = _causal_conv3_blocks(
            lambda lo, hi: u[lo:hi], tm, cols(carry_ref, f), cols(wdw_ref, f), cols(bdw_ref, f),
            lambda lo, c: (_silu(c[:, fc:]) * c[:, :fc]).astype(BF16))
        for k in range(2):
            tail = u[tm - SUBLANES:, k * fc:(k + 1) * fc]
            carry_ref[:, k * ff + f * fc:k * ff + (f + 1) * fc] = tail
            st_ref[0, :, k * ff + f * fc:k * ff + (f + 1) * fc] = tail
        return h

    def down(f, h):
        acc_ref[...] += _dot(h, wout_ref[f * fc:(f + 1) * fc, :])

    u = up(0)
    h_prev = None
    for f in range(n_chunks):
        u_next = up(f + 1) if f + 1 < n_chunks else None
        h = act(f, u)
        if h_prev is not None:
            down(f - 1, h_prev)
        u, h_prev = u_next, h
    down(n_chunks - 1, h_prev)
    y_ref[...] = _layer_norm(ALPHA * x_ref[...] + acc_ref[...], g_ref[...], b_ref[...])


def _layer_spec(shape, layer):
    zeros = (0,) * (len(shape) - 1)
    return pl.BlockSpec((None,) + tuple(shape[1:]), lambda *_: (layer,) + zeros)


def _ffn_prompt(x, seq_len, layer, win, wdw, bdw, wout, g, b):
    n, d = x.shape
    tm = min(ROW_TILE, seq_len)
    assert seq_len % tm == 0 and n % seq_len == 0
    n_seq = n // seq_len
    tps = seq_len // tm
    ff2 = win.shape[2]
    buffered = lambda spec: pl.BlockSpec(spec.block_shape, spec.index_map, pipeline_mode=pl.Buffered(1))
    y, st = pl.pallas_call(
        functools.partial(_ffn_prompt_kernel, tiles_per_seq=tps),
        grid=(n // tm,),
        in_specs=[
            pl.BlockSpec((tm, d), lambda i: (i, 0)),
            buffered(_layer_spec(win.shape, layer)),
            _layer_spec(wdw.shape, layer),
            _layer_spec(bdw.shape, layer),
            buffered(_layer_spec(wout.shape, layer)),
            _layer_spec(g.shape, layer),
            _layer_spec(b.shape, layer),
        ],
        out_specs=[
            pl.BlockSpec((tm, d), lambda i: (i, 0)),
            pl.BlockSpec((1, SUBLANES, ff2), lambda i: (i // tps, 0, 0)),
        ],
        out_shape=[
            jax.ShapeDtypeStruct((n, d), F32),
            jax.ShapeDtypeStruct((n_seq, SUBLANES, ff2), F32),
        ],
        scratch_shapes=[
            pltpu.VMEM((tm, d), BF16),
            pltpu.VMEM((SUBLANES, ff2), F32),
            pltpu.VMEM((tm, d), F32),
        ],
        compiler_params=pltpu.CompilerParams(
            dimension_semantics=("arbitrary",), vmem_limit_bytes=VMEM_LIMIT),
        name="ffn_prompt",
    )(x, win, wdw, bdw, wout, g, b)
    return y, st[:, SUBLANES - 2:, :]


def _sc_prompt_kernel(x_ref, win_ref, wdw_ref, bdw_ref, wout_ref, g_ref, b_ref,
                      y_ref, st_ref, xb_ref, carry_ref, acc_ref, *, tiles_per_seq):
    i = pl.program_id(0)

    @pl.when(i % tiles_per_seq == 0)
    def _():
        carry_ref[...] = jnp.zeros_like(carry_ref)

    tm, d = x_ref.shape
    cw = SC_CHUNK
    n_chunks = d // cw
    xb_ref[...] = x_ref[...].astype(BF16)
    acc_ref[...] = jnp.zeros_like(acc_ref)

    def up(f):
        xb = xb_ref[...]
        return jnp.concatenate([_dot(xb, win_ref[:, k * d + f * cw:k * d + (f + 1) * cw]) for k in range(3)],
                               axis=1)

    def act(f, z):
        cs = slice(f * cw, (f + 1) * cw)
        ch_rows = lambda lo, hi: z[lo:hi, cw:2 * cw] * z[lo:hi, 2 * cw:]
        m = _causal_conv3_blocks(
            ch_rows, tm, carry_ref[:, cs], wdw_ref[:, cs], bdw_ref[:, cs],
            lambda lo, conv: (z[lo:lo + CONV_ROWS, :cw] * conv).astype(BF16))
        tail = ch_rows(tm - SUBLANES, tm)
        carry_ref[:, cs] = tail
        st_ref[0, :, cs] = tail
        return m

    def down(f, m):
        acc_ref[...] += _dot(m, wout_ref[f * cw:(f + 1) * cw, :])

    z = up(0)
    m_prev = None
    for f in range(n_chunks):
        z_next = up(f + 1) if f + 1 < n_chunks else None
        m = act(f, z)
        if m_prev is not None:
            down(f - 1, m_prev)
        z, m_prev = z_next, m
    down(n_chunks - 1, m_prev)
    y_ref[...] = _layer_norm(ALPHA * x_ref[...] + acc_ref[...], g_ref[...], b_ref[...])


def _sc_prompt(x, seq_len, layer, li, win, wdw, bdw, wout, g, b):
    n, d = x.shape
    tm = min(ROW_TILE, seq_len)
    assert seq_len % tm == 0 and n % seq_len == 0
    n_seq = n // seq_len
    tps = seq_len // tm
    buffered = lambda spec: pl.BlockSpec(spec.block_shape, spec.index_map, pipeline_mode=pl.Buffered(1))
    y, st = pl.pallas_call(
        functools.partial(_sc_prompt_kernel, tiles_per_seq=tps),
        grid=(n // tm,),
        in_specs=[
            pl.BlockSpec((tm, d), lambda i: (i, 0)),
            buffered(_layer_spec(win.shape, li)),
            _layer_spec(wdw.shape, li),
            _layer_spec(bdw.shape, li),
            buffered(_layer_spec(wout.shape, li)),
            _layer_spec(g.shape, layer),
            _layer_spec(b.shape, layer),
        ],
        out_specs=[
            pl.BlockSpec((tm, d), lambda i: (i, 0)),
            pl.BlockSpec((1, SUBLANES, d), lambda i: (i // tps, 0, 0)),
        ],
        out_shape=[
            jax.ShapeDtypeStruct((n, d), F32),
            jax.ShapeDtypeStruct((n_seq, SUBLANES, d), F32),
        ],
        scratch_shapes=[
            pltpu.VMEM((tm, d), BF16),
            pltpu.VMEM((SUBLANES, d), F32),
            pltpu.VMEM((tm, d), F32),
        ],
        compiler_params=pltpu.CompilerParams(
            dimension_semantics=("arbitrary",), vmem_limit_bytes=VMEM_LIMIT),
        name="sc_prompt",
    )(x, win, wdw, bdw, wout, g, b)
    return y, st[:, SUBLANES - 2:, :]


def _rotary(x, cosf, sinf):
    return x * cosf + pltpu.roll(x, RET_DK // 2, 1) * sinf


def _gla_chunk(q_c, k_c, v_c, wn, r, st, mask_q, masks_p, mask_bd, sblock):
    c = q_c.shape[0]
    n_sub = c // GLA_SUB
    cum = r + wn
    tot = cum[c - 1:c, :]
    qw = q_c * jnp.exp(wn)
    qc = q_c * jnp.exp(cum)
    kd = k_c * jnp.exp(tot - cum)
    vst = jnp.concatenate([v_c[:, h * GLA_DV:(h + 1) * GLA_DV] for h in range(GLA_HEADS)], axis=0)
    vst_b = vst.astype(BF16)
    intra = []
    for i in range(n_sub):
        r_i = r[i * GLA_SUB:i * GLA_SUB + 1, :]
        kt = jnp.where(sblock <= i, k_c * jnp.exp(r_i - cum), 0.0)
        kbig = jnp.concatenate([kt] * GLA_HEADS, axis=0).astype(BF16)
        q16 = qw[i * GLA_SUB:(i + 1) * GLA_SUB]
        qs = jnp.where(mask_q, jnp.concatenate([q16] * GLA_HEADS, axis=0), 0.0).astype(BF16)
        sc = _dot_nt(qs, kbig)
        p = jnp.where(masks_p[i], sc, 0.0).astype(BF16)
        intra.append(_dot(p, vst_b))
    qcs = jnp.where(mask_bd, jnp.concatenate([qc] * GLA_HEADS, axis=0), 0.0).astype(BF16)
    inter = _dot_nt(qcs, st.astype(BF16))
    heads = []
    for h in range(GLA_HEADS):
        intra_h = jnp.concatenate([intra[i][h * GLA_SUB:(h + 1) * GLA_SUB] for i in range(n_sub)], axis=0)
        heads.append(intra_h + inter[h * c:(h + 1) * c])
    o = jnp.concatenate(heads, axis=1)
    kdbig = jnp.where(mask_bd, jnp.concatenate([kd] * GLA_HEADS, axis=0), 0.0).astype(BF16)
    st_new = jnp.exp(tot) * st + _dot(vst.T.astype(BF16), kdbig)
    return o, st_new


def _mix_prompt_kernel(x_ref, win_ref, wup_ref, bgate_ref, ggla_ref, gret_ref, bret_ref, wout_ref,
                       lng_ref, lnb_ref, cos_ref, sin_ref, mcat_ref, dmask_ref, qdec_ref, kdec_ref,
                       y_ref, sg_ref, sr_ref, stg_ref, str_ref):
    j = pl.program_id(1)
    nj = pl.num_programs(1)

    @pl.when(j == 0)
    def _():
        stg_ref[...] = jnp.zeros_like(stg_ref)
        str_ref[...] = jnp.zeros_like(str_ref)

    tc = x_ref.shape[0]
    x = x_ref[...]
    xb = x.astype(BF16)

    def proj(off, width):
        return _dot(xb, win_ref[:, off:off + width])

    qg = proj(_OFF_QG, GLA_QK) * (GLA_DK ** -0.5)
    kg = proj(_OFF_KG, GLA_QK)
    vg = proj(_OFF_VG, GLA_V)
    lr = proj(_OFF_LR, PAD_RANK)
    gate = _dot(lr.astype(BF16), wup_ref[...]) + bgate_ref[...]
    lg = _log_sigmoid(gate) * (1.0 / GLA_TAU)
    hi, mid, lo = _split3(lg)
    cs = _dot(mcat_ref[...], jnp.concatenate([hi, mid, lo], axis=1))
    cs = cs[:, :GLA_QK] + cs[:, GLA_QK:2 * GLA_QK] + cs[:, 2 * GLA_QK:]
    wn_all = cs[:tc]
    r_all = cs[tc:]

    c = GLA_CHUNK
    hq = GLA_HEADS * GLA_SUB
    row64 = lax.broadcasted_iota(jnp.int32, (hq, GLA_QK), 0)
    lane64 = lax.broadcasted_iota(jnp.int32, (hq, GLA_QK), 1)
    lane_head = lax.shift_right_logical(lane64, 6)
    row_head = lax.shift_right_logical(row64, 4)
    mask_q = lane_head == row_head
    masks_p = [mask_q & ((lane64 & (c - 1)) <= (row64 & (GLA_SUB - 1)) + i * GLA_SUB)
               for i in range(c // GLA_SUB)]
    rowb = lax.broadcasted_iota(jnp.int32, (GLA_HEADS * c, GLA_QK), 0)
    laneb = lax.broadcasted_iota(jnp.int32, (GLA_HEADS * c, GLA_QK), 1)
    mask_bd = lax.shift_right_logical(rowb, 6) == lax.shift_right_logical(laneb, 6)
    sblock = lax.shift_right_logical(lax.broadcasted_iota(jnp.int32, (c, GLA_QK), 0), 4)

    st = stg_ref[...]
    og_chunks = []
    for ci in range(tc // c):
        sl = slice(ci * c, (ci + 1) * c)
        o_c, st = _gla_chunk(qg[sl], kg[sl], vg[sl], wn_all[sl], r_all[sl], st,
                             mask_q, masks_p, mask_bd, sblock)
        og_chunks.append(o_c)
    stg_ref[...] = st
    og = jnp.concatenate(og_chunks, axis=0)
    rg = proj(_OFF_RG, GLA_V)
    ggla = ggla_ref[...]
    og_heads = []
    for h in range(GLA_HEADS):
        sl = slice(h * GLA_DV, (h + 1) * GLA_DV)
        o = og[:, sl]
        o = o * lax.rsqrt(jnp.mean(o * o, axis=-1, keepdims=True) + NORM_EPS) * ggla
        og_heads.append(o * _silu(rg[:, sl]))

    cosf = cos_ref[...]
    sinf = sin_ref[...]
    or_heads = []
    qr_all = proj(_OFF_QR, RET_QK)
    kr_all = proj(_OFF_KR, RET_QK)
    vr_all = proj(_OFF_VR, RET_V)
    gr_all = proj(_OFF_GR, RET_V)
    for h in range(RET_HEADS):
        sl = slice(h * RET_DK, (h + 1) * RET_DK)
        qh = _rotary(qr_all[:, sl], cosf, sinf)
        kh = _rotary(kr_all[:, sl], cosf, sinf) * (RET_DK ** -0.5)
        vh = vr_all[:, sl].astype(BF16)
        gh = gr_all[:, sl]
        qdec = qdec_ref[h]
        s_h = str_ref[h]
        sc = _dot_nt(qh.astype(BF16), kh.astype(BF16)) * dmask_ref[h]
        o = _dot(sc.astype(BF16), vh) + _dot((qh * qdec).astype(BF16), s_h.astype(BF16))
        str_ref[h] = qdec[tc - 1:tc, :] * s_h + _dot((kh * kdec_ref[h]).T.astype(BF16), vh)
        mu = jnp.mean(o, axis=-1, keepdims=True)
        oc = o - mu
        var = jnp.mean(oc * oc, axis=-1, keepdims=True)
        o = oc * lax.rsqrt(var + NORM_EPS) * gret_ref[:, sl] + bret_ref[:, sl]
        or_heads.append(o * _silu(gh))

    mixed = jnp.concatenate(og_heads + or_heads, axis=1).astype(BF16)
    y = _dot(mixed, wout_ref[...])
    y_ref[...] = _layer_norm(ALPHA * x + y, lng_ref[...], lnb_ref[...])

    @pl.when(j == nj - 1)
    def _():
        sg_ref[0] = stg_ref[...].T
        sr_ref[0] = str_ref[...]


def _mix_prompt(x, n_seq, seq_len, layer, li, win, wup, bgate, ggla, gret, bret, wout, lng, lnb, tables):
    n, d = x.shape
    tc = min(MIX_TILE, seq_len)
    assert seq_len % tc == 0 and tc % GLA_CHUNK == 0
    nt = seq_len // tc
    cosf, sinf, mcat, dmask, qdec, kdec = tables
    c2 = lambda b, j: (0, 0)
    c3 = lambda b, j: (0, 0, 0)
    buffered = lambda spec: pl.BlockSpec(spec.block_shape, spec.index_map, pipeline_mode=pl.Buffered(1))
    y, sg, sr = pl.pallas_call(
        _mix_prompt_kernel,
        grid=(n_seq, nt),
        in_specs=[
            pl.BlockSpec((tc, d), lambda b, j: (b * nt + j, 0)),
            buffered(_layer_spec(win.shape, li)),
            _layer_spec(wup.shape, li),
            _layer_spec(bgate.shape, li),
            _layer_spec(ggla.shape, li),
            _layer_spec(gret.shape, li),
            _layer_spec(bret.shape, li),
            buffered(_layer_spec(wout.shape, li)),
            _layer_spec(lng.shape, layer),
            _layer_spec(lnb.shape, layer),
            pl.BlockSpec((tc, RET_DK), lambda b, j: (j, 0)),
            pl.BlockSpec((tc, RET_DK), lambda b, j: (j, 0)),
            pl.BlockSpec((2 * tc, tc), c2),
            pl.BlockSpec((RET_HEADS, tc, tc), c3),
            pl.BlockSpec((RET_HEADS, tc, RET_DK), c3),
            pl.BlockSpec((RET_HEADS, tc, RET_DK), c3),
        ],
        out_specs=[
            pl.BlockSpec((tc, d), lambda b, j: (b * nt + j, 0)),
            pl.BlockSpec((1, GLA_QK, GLA_DV), lambda b, j: (b, 0, 0)),
            pl.BlockSpec((1, RET_HEADS, RET_DK, RET_DV), lambda b, j: (b, 0, 0, 0)),
        ],
        out_shape=[
            jax.ShapeDtypeStruct((n, d), F32),
            jax.ShapeDtypeStruct((n_seq, GLA_QK, GLA_DV), F32),
            jax.ShapeDtypeStruct((n_seq, RET_HEADS, RET_DK, RET_DV), F32),
        ],
        scratch_shapes=[
            pltpu.VMEM((GLA_DV, GLA_QK), F32),
            pltpu.VMEM((RET_HEADS, RET_DK, RET_DV), F32),
        ],
        compiler_params=pltpu.CompilerParams(
            dimension_semantics=("arbitrary", "arbitrary"), vmem_limit_bytes=VMEM_LIMIT),
        name="mix_prompt",
    )(x, win, wup, bgate, ggla, gret, bret, wout, lng, lnb, cosf, sinf, mcat, dmask, qdec, kdec)
    return y, sg.reshape(n_seq, GLA_HEADS, GLA_DK, GLA_DV), sr


def _mix_tables(seq_len):
    tc = min(MIX_TILE, seq_len)
    pos = jnp.arange(seq_len, dtype=F32)
    cosf, sinf = _rotary_tables(pos)
    t = jnp.arange(tc)
    same_sub = (t[:, None] // GLA_SUB) == (t[None, :] // GLA_SUB)
    same_chunk = (t[:, None] // GLA_CHUNK) == (t[None, :] // GLA_CHUNK)
    m_within = same_sub & (t[None, :] <= t[:, None])
    m_before = same_chunk & ((t[None, :] // GLA_SUB) < (t[:, None] // GLA_SUB))
    mcat = jnp.concatenate([m_within, m_before], axis=0).astype(BF16)
    log_gamma = jnp.log1p(-jnp.exp2(-5.0 - jnp.arange(RET_HEADS, dtype=F32)))
    tf = t.astype(F32)
    diff = tf[:, None] - tf[None, :]
    dmask = jnp.where(diff >= 0, jnp.exp(diff[None] * log_gamma[:, None, None]), 0.0)
    qdec = jnp.exp((tf[None, :, None] + 1.0) * log_gamma[:, None, None])
    kdec = jnp.exp((tc - 1.0 - tf[None, :, None]) * log_gamma[:, None, None])
    qdec = jnp.broadcast_to(qdec, (RET_HEADS, tc, RET_DK))
    kdec = jnp.broadcast_to(kdec, (RET_HEADS, tc, RET_DK))
    return cosf, sinf, mcat, dmask, qdec, kdec


def _rotary_tables(pos):
    inv_freq = 1.0 / (ROPE_BASE ** (jnp.arange(0, RET_DK, 2, dtype=F32) / RET_DK))
    ang = pos[:, None] * inv_freq[None, :]
    cos, sin = jnp.cos(ang), jnp.sin(ang)
    return jnp.concatenate([cos, cos], axis=1), jnp.concatenate([-sin, sin], axis=1)


def _ffn_sample_kernel(x_ref, st_ref, win_ref, wdw_ref, bdw_ref, wout_ref, g_ref, b_ref,
                       y_ref, nst_ref):
    x = x_ref[...]
    w2 = 2 * D_FF
    u = _dot(x.astype(BF16), win_ref[...])
    wdw = wdw_ref[...]
    c = bdw_ref[...] + wdw[0:1] * st_ref[:, :w2] + wdw[1:2] * st_ref[:, w2:] + wdw[2:3] * u
    h = _silu(c[:, D_FF:]) * c[:, :D_FF]
    y = _dot(h.astype(BF16), wout_ref[...])
    y_ref[...] = _layer_norm(ALPHA * x + y, g_ref[...], b_ref[...])
    nst_ref[:, :w2] = st_ref[:, w2:]
    nst_ref[:, w2:] = u


def _sc_sample_kernel(x_ref, st_ref, win_ref, wdw_ref, bdw_ref, wout_ref, g_ref, b_ref,
                      y_ref, nst_ref):
    x = x_ref[...]
    d = x.shape[1]
    z = _dot(x.astype(BF16), win_ref[...])
    ch = z[:, d:2 * d] * z[:, 2 * d:]
    wdw = wdw_ref[...]
    conv = bdw_ref[...] + wdw[0:1] * st_ref[:, :d] + wdw[1:2] * st_ref[:, d:] + wdw[2:3] * ch
    y = _dot((z[:, :d] * conv).astype(BF16), wout_ref[...])
    y_ref[...] = _layer_norm(ALPHA * x + y, g_ref[...], b_ref[...])
    nst_ref[:, :d] = st_ref[:, d:]
    nst_ref[:, d:] = ch


def _conv_sample(body, name, x, st, layer, li, win, wdw, bdw, wout, g, b):
    s, d = x.shape
    whole = lambda a: pl.BlockSpec(a.shape, lambda i: (0,) * a.ndim)
    return pl.pallas_call(
        body,
        grid=(1,),
        in_specs=[whole(x), whole(st), _layer_spec(win.shape, li), _layer_spec(wdw.shape, li),
                  _layer_spec(bdw.shape, li), _layer_spec(wout.shape, li),
                  _layer_spec(g.shape, layer), _layer_spec(b.shape, layer)],
        out_specs=[whole(x), whole(st)],
        out_shape=[jax.ShapeDtypeStruct((s, d), F32), jax.ShapeDtypeStruct(st.shape, F32)],
        compiler_params=pltpu.CompilerParams(
            dimension_semantics=("arbitrary",), vmem_limit_bytes=VMEM_LIMIT),
        name=name,
    )(x, st, win, wdw, bdw, wout, g, b)


def _mix_sample_proj_kernel(x_ref, win_ref, wup_ref, bgate_ref, cos_ref, sin_ref,
                            cols_ref, rows_ref, gates_ref):
    xb = x_ref[...].astype(BF16)

    def proj(off, width):
        return _dot(xb, win_ref[:, off:off + width])

    qg = proj(_OFF_QG, GLA_QK) * (GLA_DK ** -0.5)
    kg = proj(_OFF_KG, GLA_QK)
    lr = proj(_OFF_LR, PAD_RANK)
    gate = _dot(lr.astype(BF16), wup_ref[...]) + bgate_ref[...]
    ag = jnp.exp(_log_sigmoid(gate) * (1.0 / GLA_TAU))
    cosf = cos_ref[...]
    sinf = sin_ref[...]
    qr = [_rotary(proj(_OFF_QR + h * RET_DK, RET_DK), cosf, sinf) for h in range(RET_HEADS)]
    kr = [_rotary(proj(_OFF_KR + h * RET_DK, RET_DK), cosf, sinf) * (RET_DK ** -0.5)
          for h in range(RET_HEADS)]
    cols_ref[0:GLA_QK, :] = ag.T
    cols_ref[GLA_QK:2 * GLA_QK, :] = kg.T
    cols_ref[2 * GLA_QK:3 * GLA_QK, :] = qg.T
    base = 3 * GLA_QK
    for h in range(RET_HEADS):
        cols_ref[base + h * RET_DK:base + (h + 1) * RET_DK, :] = kr[h].T
        cols_ref[base + RET_QK + h * RET_DK:base + RET_QK + (h + 1) * RET_DK, :] = qr[h].T
    rows_ref[:, :GLA_V] = proj(_OFF_VG, GLA_V)
    rows_ref[:, GLA_V:] = proj(_OFF_VR, RET_V)
    gates_ref[:, :GLA_V] = proj(_OFF_RG, GLA_V)
    gates_ref[:, GLA_V:] = proj(_OFF_GR, RET_V)


_COL_ROWS = 3 * GLA_QK + 2 * RET_QK


def _mix_sample_state_kernel(*refs, n_stacked_in):
    cols_ref, rows_ref, sg_ref, sr_ref, gam_ref = refs[:5]
    nsg_ref, nsr_ref, o_ref = refs[5 + n_stacked_in:]
    nb = sg_ref.shape[0]
    s = cols_ref.shape[1]
    blk = pl.program_id(0)
    lane = lax.broadcasted_iota(jnp.int32, (1, s), 1)

    def body(bb, carry):
        b = blk * nb + bb
        onehot = (lane == b).astype(F32)
        col = jnp.sum(cols_ref[...] * onehot, axis=1, keepdims=True)
        vrow = rows_ref[b]
        a_g = col[0:GLA_QK]
        k_g = col[GLA_QK:2 * GLA_QK]
        q_g = col[2 * GLA_QK:3 * GLA_QK]
        base = 3 * GLA_QK
        outs = []
        for h in range(GLA_HEADS):
            rs = slice(h * GLA_DK, (h + 1) * GLA_DK)
            v_h = vrow[:, h * GLA_DV:(h + 1) * GLA_DV]
            s_new = a_g[rs] * sg_ref[bb, h] + k_g[rs] * v_h
            nsg_ref[bb, h] = s_new
            outs.append(jnp.sum(q_g[rs] * s_new, axis=0, keepdims=True))
        for h in range(RET_HEADS):
            rs = slice(h * RET_DK, (h + 1) * RET_DK)
            k_h = col[base + h * RET_DK:base + (h + 1) * RET_DK]
            q_h = col[base + RET_QK + h * RET_DK:base + RET_QK + (h + 1) * RET_DK]
            v_h = vrow[:, GLA_V + h * RET_DV:GLA_V + (h + 1) * RET_DV]
            s_new = gam_ref[h] * sr_ref[bb, h] + k_h * v_h
            nsr_ref[bb, h] = s_new
            outs.append(jnp.sum(q_h * s_new, axis=0, keepdims=True))
        o_ref[b] = jnp.concatenate(outs, axis=1)
        return carry

    lax.fori_loop(0, nb, body, 0)


def _mix_sample_out_kernel(x_ref, o_ref, gates_ref, ggla_ref, gret_ref, bret_ref, wout_ref,
                           lng_ref, lnb_ref, y_ref):
    x = x_ref[...]
    o = o_ref[...]
    gates = gates_ref[...]
    ggla = ggla_ref[...]
    heads = []
    for h in range(GLA_HEADS):
        sl = slice(h * GLA_DV, (h + 1) * GLA_DV)
        oh = o[:, sl]
        oh = oh * lax.rsqrt(jnp.mean(oh * oh, axis=-1, keepdims=True) + NORM_EPS) * ggla
        heads.append(oh * _silu(gates[:, sl]))
    for h in range(RET_HEADS):
        sl = slice(GLA_V + h * RET_DV, GLA_V + (h + 1) * RET_DV)
        ps = slice(h * RET_DV, (h + 1) * RET_DV)
        oh = o[:, sl]
        mu = jnp.mean(oh, axis=-1, keepdims=True)
        oc = oh - mu
        var = jnp.mean(oc * oc, axis=-1, keepdims=True)
        oh = oc * lax.rsqrt(var + NORM_EPS) * gret_ref[:, ps] + bret_ref[:, ps]
        heads.append(oh * _silu(gates[:, sl]))
    mixed = jnp.concatenate(heads, axis=1).astype(BF16)
    y = _dot(mixed, wout_ref[...])
    y_ref[...] = _layer_norm(ALPHA * x + y, lng_ref[...], lnb_ref[...])


def _mix_sample(x, state_gla, state_ret, layer, li, stacked, win, wup, bgate, ggla, gret, bret, wout, lng, lnb,
                cos_s, sin_s, gam):
    s, d = x.shape
    whole = lambda a: pl.BlockSpec(a.shape, lambda i: (0,) * a.ndim)
    proj_out = [
        jax.ShapeDtypeStruct((_COL_ROWS, s), F32),
        jax.ShapeDtypeStruct((s, GLA_V + RET_V), F32),
        jax.ShapeDtypeStruct((s, GLA_V + RET_V), F32),
    ]
    cols, rows, gates = pl.pallas_call(
        _mix_sample_proj_kernel,
        grid=(1,),
        in_specs=[whole(x), _layer_spec(win.shape, li), _layer_spec(wup.shape, li),
                  _layer_spec(bgate.shape, li), whole(cos_s), whole(sin_s)],
        out_specs=[whole(o) for o in proj_out],
        out_shape=proj_out,
        compiler_params=pltpu.CompilerParams(
            dimension_semantics=("arbitrary",), vmem_limit_bytes=VMEM_LIMIT),
        name="mix_sample_proj",
    )(x, win, wup, bgate, cos_s, sin_s)
    rows = rows.reshape(s, 1, GLA_V + RET_V)

    nb = SUBLANES
    assert s % nb == 0
    gla_spec = pl.BlockSpec((None, nb, GLA_HEADS, GLA_DK, GLA_DV), lambda i: (li, i, 0, 0, 0))
    ret_spec = pl.BlockSpec((None, nb, RET_HEADS, RET_DK, RET_DV), lambda i: (li, i, 0, 0, 0))
    stacked = () if stacked is None else tuple(stacked)
    nsg, nsr, o = pl.pallas_call(
        functools.partial(_mix_sample_state_kernel, n_stacked_in=len(stacked)),
        grid=(s // nb,),
        in_specs=[
            pl.BlockSpec((_COL_ROWS, s), lambda i: (0, 0)),
            pl.BlockSpec((s, 1, GLA_V + RET_V), lambda i: (0, 0, 0)),
            gla_spec,
            ret_spec,
            pl.BlockSpec((RET_HEADS, RET_DK, RET_DV), lambda i: (0, 0, 0)),
        ] + [pl.BlockSpec(memory_space=pl.ANY)] * len(stacked),
        out_specs=[
            gla_spec,
            ret_spec,
            pl.BlockSpec((s, 1, GLA_V + RET_V), lambda i: (0, 0, 0)),
        ],
        out_shape=[
            jax.ShapeDtypeStruct(state_gla.shape, F32),
            jax.ShapeDtypeStruct(state_ret.shape, F32),
            jax.ShapeDtypeStruct((s, 1, GLA_V + RET_V), F32),
        ],
        input_output_aliases={5 + k: k for k in range(len(stacked))},
        compiler_params=pltpu.CompilerParams(
            dimension_semantics=("arbitrary",), vmem_limit_bytes=VMEM_LIMIT),
        name="mix_sample_state",
    )(cols, rows, state_gla, state_ret, gam, *stacked)
    o = o.reshape(s, GLA_V + RET_V)

    y = pl.pallas_call(
        _mix_sample_out_kernel,
        grid=(1,),
        in_specs=[whole(x), whole(o), whole(gates), _layer_spec(ggla.shape, li), _layer_spec(gret.shape, li),
                  _layer_spec(bret.shape, li), _layer_spec(wout.shape, li),
                  _layer_spec(lng.shape, layer), _layer_spec(lnb.shape, layer)],
        out_specs=whole(x),
        out_shape=jax.ShapeDtypeStruct((s, d), F32),
        compiler_params=pltpu.CompilerParams(
            dimension_semantics=("arbitrary",), vmem_limit_bytes=VMEM_LIMIT),
        name="mix_sample_out",
    )(x, o, gates, ggla, gret, bret, wout, lng, lnb)
    return y, nsg, nsr


def _prep_mix_in(w):
    a = 2 * GLA_QK + GLA_V
    lr = jnp.pad(w[:, :, a:a + GLA_RANK], ((0, 0), (0, 0), (0, PAD_RANK - GLA_RANK)))
    return jnp.concatenate([w[:, :, :a], w[:, :, a + GLA_RANK:], lr], axis=2).astype(BF16)


def kernel(x_prompt, x_sample, state_gla, state_ret, state_conv, state_ffn,
           w_mix_in, w_gla_gate_up, b_gla_gate, g_gla_norm, g_ret_norm, b_ret_norm, w_mix_out,
           w_sc_in, w_sc_dw, b_sc_dw, w_sc_out, ln1_g, ln1_b,
           w_ffn_in, w_ffn_dw, b_ffn_dw, w_ffn_out, ln2_g, ln2_b):
    bp, tp, d = x_prompt.shape
    s, ts, _ = x_sample.shape
    assert ts == 1 and d == D_MODEL
    xp = x_prompt.reshape(bp * tp, d)
    xs = x_sample.reshape(s, d)

    tables = _mix_tables(tp)
    cos_s, sin_s = _rotary_tables(PAST_LEN + jnp.arange(ts, dtype=F32))
    log_gamma = jnp.log1p(-jnp.exp2(-5.0 - jnp.arange(RET_HEADS, dtype=F32)))
    gam = jnp.broadcast_to(jnp.exp(log_gamma)[:, None, None], (RET_HEADS, RET_DK, RET_DV))

    rows = lambda v: v[:, None, :]
    mix_w = (_prep_mix_in(w_mix_in),
             jnp.pad(w_gla_gate_up, ((0, 0), (0, PAD_RANK - GLA_RANK), (0, 0))).astype(BF16),
             rows(b_gla_gate), rows(g_gla_norm), rows(g_ret_norm), rows(b_ret_norm), w_mix_out.astype(BF16))
    sc_w = (w_sc_in.astype(BF16), w_sc_dw, rows(b_sc_dw), w_sc_out.astype(BF16))
    ffn_w = (w_ffn_in.astype(BF16), w_ffn_dw, rows(b_ffn_dw), w_ffn_out.astype(BF16))
    ln1 = (rows(ln1_g), rows(ln1_b))
    ln2 = (rows(ln2_g), rows(ln2_b))

    gla_p, ret_p, conv_p, conv_s, ffn_p, ffn_s = [], [], [], [], [], []
    stacked_s = None
    for layer in range(DEPTH):
        li = layer // 2
        if layer % 2 == 0:
            xp, sg, sr = _mix_prompt(xp, bp, tp, layer, li, *mix_w, *ln1, tables)
            gla_p.append(sg)
            ret_p.append(sr)
            xs, gla_s, ret_s = _mix_sample(xs, state_gla, state_ret, layer, li, stacked_s, *mix_w, *ln1,
                                           cos_s, sin_s, gam)
            stacked_s = (gla_s, ret_s)
        else:
            xp, sc = _sc_prompt(xp, tp, layer, li, *sc_w, *ln1)
            conv_p.append(sc)
            xs, sc = _conv_sample(_sc_sample_kernel, "sc_sample", xs, state_conv[li].reshape(s, 2 * d),
                                  layer, li, *sc_w, *ln1)
            conv_s.append(sc.reshape(s, 2, d))
        xp, sf = _ffn_prompt(xp, tp, layer, *ffn_w, *ln2)
        ffn_p.append(sf)
        xs, sf = _conv_sample(_ffn_sample_kernel, "ffn_sample", xs, state_ffn[layer].reshape(s, 4 * D_FF),
                              layer, layer, *ffn_w, *ln2)
        ffn_s.append(sf.reshape(s, 2, 2 * D_FF))

    return (xp.reshape(bp, tp, d), xs.reshape(s, ts, d),
            jnp.stack(gla_p), gla_s, jnp.stack(ret_p), ret_s,
            jnp.stack(conv_p), jnp.stack(conv_s), jnp.stack(ffn_p), jnp.stack(ffn_s))
```

```python
import functools

import jax
import jax.numpy as jnp
from jax import lax
from jax.experimental import pallas as pl
from jax.experimental.pallas import tpu as pltpu

F32 = jnp.float32
BF16 = jnp.bfloat16

D_MODEL = 1024
DEPTH = 4
PAST_LEN = 16384
GLA_HEADS = 4
GLA_DK = 64
GLA_DV = 128
GLA_RANK = 16
GLA_TAU = 16.0
RET_HEADS = 4
RET_DK = 128
RET_DV = 128
CONV_W = 3
D_FF = D_MODEL * 11 // 4
ROPE_BASE = 10000.0
NORM_EPS = 1e-5
ALPHA = (2 * DEPTH) ** 0.25
GLA_QK = GLA_HEADS * GLA_DK
GLA_V = GLA_HEADS * GLA_DV
RET_QK = RET_HEADS * RET_DK
RET_V = RET_HEADS * RET_DV

LANES = 128
SUBLANES = 8
VMEM_LIMIT = 56 * 1024 * 1024

GLA_CHUNK = 64
GLA_SUB = 16
MIX_TILE = 256
ROW_TILE = 512
CONV_ROWS = 32
FF_CHUNK = 768
SC_CHUNK = 256
PAD_RANK = LANES

_OFF_QG = 0
_OFF_KG = _OFF_QG + GLA_QK
_OFF_VG = _OFF_KG + GLA_QK
_OFF_RG = _OFF_VG + GLA_V
_OFF_QR = _OFF_RG + GLA_V
_OFF_KR = _OFF_QR + RET_QK
_OFF_VR = _OFF_KR + RET_QK
_OFF_GR = _OFF_VR + RET_V
_OFF_LR = _OFF_GR + RET_V
MIX_COLS = _OFF_LR + PAD_RANK


def _dot(a, b):
    return jnp.dot(a, b, preferred_element_type=F32)


def _dot_nt(a, b):
    return lax.dot_general(a, b, (((1,), (1,)), ((), ())), preferred_element_type=F32)


def _layer_norm(x, g, b):
    mu = jnp.mean(x, axis=-1, keepdims=True)
    xc = x - mu
    var = jnp.mean(xc * xc, axis=-1, keepdims=True)
    return xc * lax.rsqrt(var + NORM_EPS) * g + b


def _silu(x):
    return x * jax.nn.sigmoid(x)


def _log_sigmoid(x):
    return jnp.minimum(x, 0.0) - jnp.log1p(jnp.exp(-jnp.abs(x)))


def _split3(x):
    hi = x.astype(BF16)
    r1 = x - hi.astype(F32)
    mid = r1.astype(BF16)
    lo = (r1 - mid.astype(F32)).astype(BF16)
    return hi, mid, lo


def _causal_conv3_blocks(rows, n_rows, prev, w, b, post):
    w0, w1, w2 = w[0:1], w[1:2], w[2:3]
    out = []
    for lo in range(0, n_rows, CONV_ROWS):
        head = prev if lo == 0 else rows(lo - SUBLANES, lo)
        s = jnp.concatenate([head, rows(lo, lo + CONV_ROWS)], axis=0)
        s1 = pltpu.roll(s, 1, 0)[SUBLANES:]
        s2 = pltpu.roll(s, 2, 0)[SUBLANES:]
        out.append(post(lo, b + w0 * s2 + w1 * s1 + w2 * s[SUBLANES:]))
    return jnp.concatenate(out, axis=0)


def _ffn_prompt_kernel(x_ref, win_ref, wdw_ref, bdw_ref, wout_ref, g_ref, b_ref,
                       y_ref, st_ref, xb_ref, carry_ref, *, tiles_per_seq):
    i = pl.program_id(0)

    @pl.when(i % tiles_per_seq == 0)
    def _():
        carry_ref[...] = jnp.zeros_like(carry_ref)

    tm = x_ref.shape[0]
    ff = wout_ref.shape[0]
    bounds = list(range(0, ff, FF_CHUNK)) + [ff]
    n_chunks = len(bounds) - 1
    xb_ref[...] = x_ref[...].astype(BF16)

    def cols(ref, f):
        lo, hi = bounds[f], bounds[f + 1]
        return jnp.concatenate([ref[:, lo:hi], ref[:, ff + lo:ff + hi]], axis=1)

    def up(f):
        lo, hi = bounds[f], bounds[f + 1]
        xb = xb_ref[...]
        return jnp.concatenate([_dot(xb, win_ref[:, lo:hi]), _dot(xb, win_ref[:, ff + lo:ff + hi])], axis=1)

    def act(f, u):
        lo, hi = bounds[f], bounds[f + 1]
        fc = hi - lo
        h = _causal_conv3_blocks(
            lambda r0, r1: u[r0:r1], tm, cols(carry_ref, f), cols(wdw_ref, f), cols(bdw_ref, f),
            lambda r0, c: (_silu(c[:, fc:]) * c[:, :fc]).astype(BF16))
        for k in range(2):
            tail = u[tm - SUBLANES:, k * fc:(k + 1) * fc]
            carry_ref[:, k * ff + lo:k * ff + hi] = tail
            st_ref[0, :, k * ff + lo:k * ff + hi] = tail
        return h

    def down(f, h):
        return _dot(h, wout_ref[bounds[f]:bounds[f + 1], :])

    u = up(0)
    h_prev = None
    acc = None
    for f in range(n_chunks):
        u_next = up(f + 1) if f + 1 < n_chunks else None
        h = act(f, u)
        if h_prev is not None:
            p = down(f - 1, h_prev)
            acc = p if acc is None else acc + p
        u, h_prev = u_next, h
    p = down(n_chunks - 1, h_prev)
    acc = p if acc is None else acc + p
    y_ref[...] = _layer_norm(ALPHA * x_ref[...] + acc, g_ref[...], b_ref[...])


def _layer_spec(shape, layer):
    zeros = (0,) * (len(shape) - 1)
    return pl.BlockSpec((None,) + tuple(shape[1:]), lambda *_: (layer,) + zeros)


def _ffn_prompt(x, seq_len, layer, win, wdw, bdw, wout, g, b):
    n, d = x.shape
    tm = min(ROW_TILE, seq_len)
    assert seq_len % tm == 0 and n % seq_len == 0
    n_seq = n // seq_len
    tps = seq_len // tm
    ff2 = win.shape[2]
    buffered = lambda spec: pl.BlockSpec(spec.block_shape, spec.index_map, pipeline_mode=pl.Buffered(1))
    y, st = pl.pallas_call(
        functools.partial(_ffn_prompt_kernel, tiles_per_seq=tps),
        grid=(n // tm,),
        in_specs=[
            pl.BlockSpec((tm, d), lambda i: (i, 0)),
            buffered(_layer_spec(win.shape, layer)),
            _layer_spec(wdw.shape, layer),
            _layer_spec(bdw.shape, layer),
            buffered(_layer_spec(wout.shape, layer)),
            _layer_spec(g.shape, layer),
            _layer_spec(b.shape, layer),
        ],
        out_specs=[
            pl.BlockSpec((tm, d), lambda i: (i, 0)),
            pl.BlockSpec((1, SUBLANES, ff2), lambda i: (i // tps, 0, 0)),
        ],
        out_shape=[
            jax.ShapeDtypeStruct((n, d), F32),
            jax.ShapeDtypeStruct((n_seq, SUBLANES, ff2), F32),
        ],
        scratch_shapes=[
            pltpu.VMEM((tm, d), BF16),
            pltpu.VMEM((SUBLANES, ff2), F32),
        ],
        compiler_params=pltpu.CompilerParams(
            dimension_semantics=("arbitrary",), vmem_limit_bytes=VMEM_LIMIT),
        name="ffn_prompt",
    )(x, win, wdw, bdw, wout, g, b)
    return y, st[:, SUBLANES - 2:, :]


def _sc_prompt_kernel(x_ref, win_ref, wdw_ref, bdw_ref, wout_ref, g_ref, b_ref,
                      y_ref, st_ref, xb_ref, carry_ref, *, tiles_per_seq):
    i = pl.program_id(0)

    @pl.when(i % tiles_per_seq == 0)
    def _():
        carry_ref[...] = jnp.zeros_like(carry_ref)

    tm, d = x_ref.shape
    cw = SC_CHUNK
    n_chunks = d // cw
    xb_ref[...] = x_ref[...].astype(BF16)

    def up(f):
        xb = xb_ref[...]
        return jnp.concatenate([_dot(xb, win_ref[:, k * d + f * cw:k * d + (f + 1) * cw]) for k in range(3)],
                               axis=1)

    def act(f, z):
        cs = slice(f * cw, (f + 1) * cw)
        ch_rows = lambda lo, hi: z[lo:hi, cw:2 * cw] * z[lo:hi, 2 * cw:]
        m = _causal_conv3_blocks(
            ch_rows, tm, carry_ref[:, cs], wdw_ref[:, cs], bdw_ref[:, cs],
            lambda lo, conv: (z[lo:lo + CONV_ROWS, :cw] * conv).astype(BF16))
        tail = ch_rows(tm - SUBLANES, tm)
        carry_ref[:, cs] = tail
        st_ref[0, :, cs] = tail
        return m

    def down(f, m):
        return _dot(m, wout_ref[f * cw:(f + 1) * cw, :])

    z = up(0)
    m_prev = None
    acc = None
    for f in range(n_chunks):
        z_next = up(f + 1) if f + 1 < n_chunks else None
        m = act(f, z)
        if m_prev is not None:
            p = down(f - 1, m_prev)
            acc = p if acc is None else acc + p
        z, m_prev = z_next, m
    p = down(n_chunks - 1, m_prev)
    acc = p if acc is None else acc + p
    y_ref[...] = _layer_norm(ALPHA * x_ref[...] + acc, g_ref[...], b_ref[...])


def _sc_prompt(x, seq_len, layer, li, win, wdw, bdw, wout, g, b):
    n, d = x.shape
    tm = min(ROW_TILE, seq_len)
    assert seq_len % tm == 0 and n % seq_len == 0
    n_seq = n // seq_len
    tps = seq_len // tm
    buffered = lambda spec: pl.BlockSpec(spec.block_shape, spec.index_map, pipeline_mode=pl.Buffered(1))
    y, st = pl.pallas_call(
        functools.partial(_sc_prompt_kernel, tiles_per_seq=tps),
        grid=(n // tm,),
        in_specs=[
            pl.BlockSpec((tm, d), lambda i: (i, 0)),
            buffered(_layer_spec(win.shape, li)),
            _layer_spec(wdw.shape, li),
            _layer_spec(bdw.shape, li),
            buffered(_layer_spec(wout.shape, li)),
            _layer_spec(g.shape, layer),
            _layer_spec(b.shape, layer),
        ],
        out_specs=[
            pl.BlockSpec((tm, d), lambda i: (i, 0)),
            pl.BlockSpec((1, SUBLANES, d), lambda i: (i // tps, 0, 0)),
        ],
        out_shape=[
            jax.ShapeDtypeStruct((n, d), F32),
            jax.ShapeDtypeStruct((n_seq, SUBLANES, d), F32),
        ],
        scratch_shapes=[
            pltpu.VMEM((tm, d), BF16),
            pltpu.VMEM((SUBLANES, d), F32),
        ],
        compiler_params=pltpu.CompilerParams(
            dimension_semantics=("arbitrary",), vmem_limit_bytes=VMEM_LIMIT),
        name="sc_prompt",
    )(x, win, wdw, bdw, wout, g, b)
    return y, st[:, SUBLANES - 2:, :]


def _rotary(x, cosf, sinf):
    return x * cosf + pltpu.roll(x, RET_DK // 2, 1) * sinf


def _gla_tile(qg, kg, vg, wn_all, r_all, st, mask_q, masks_p, mask_bd, sblock):
    c = GLA_CHUNK
    n_chunks = qg.shape[0] // c
    n_sub = c // GLA_SUB
    pre = []
    for ci in range(n_chunks):
        sl = slice(ci * c, (ci + 1) * c)
        q_c, k_c, v_c, wn, r = qg[sl], kg[sl], vg[sl], wn_all[sl], r_all[sl]
        cum = r + wn
        tot = cum[c - 1:c, :]
        vst = jnp.concatenate([v_c[:, h * GLA_DV:(h + 1) * GLA_DV] for h in range(GLA_HEADS)], axis=0)
        pre.append(dict(q=q_c, k=k_c, r=r, cum=cum, tot=tot, qw=q_c * jnp.exp(wn), vst=vst,
                        vst_b=vst.astype(BF16)))
    scores = []
    for p in pre:
        row = []
        for i in range(n_sub):
            r_i = p["r"][i * GLA_SUB:i * GLA_SUB + 1, :]
            kt = jnp.where(sblock <= i, p["k"] * jnp.exp(r_i - p["cum"]), 0.0)
            kbig = jnp.concatenate([kt] * GLA_HEADS, axis=0).astype(BF16)
            q16 = p["qw"][i * GLA_SUB:(i + 1) * GLA_SUB]
            qs = jnp.where(mask_q, jnp.concatenate([q16] * GLA_HEADS, axis=0), 0.0).astype(BF16)
            row.append(_dot_nt(qs, kbig))
        scores.append(row)
    intra = [[_dot(jnp.where(masks_p[i], scores[ci][i], 0.0).astype(BF16), p["vst_b"])
              for i in range(n_sub)] for ci, p in enumerate(pre)]
    upds, qcss = [], []
    for p in pre:
        kd = p["k"] * jnp.exp(p["tot"] - p["cum"])
        kdbig = jnp.where(mask_bd, jnp.concatenate([kd] * GLA_HEADS, axis=0), 0.0).astype(BF16)
        upds.append(_dot(p["vst"].T.astype(BF16), kdbig))
        qc = p["q"] * jnp.exp(p["cum"])
        qcss.append(jnp.where(mask_bd, jnp.concatenate([qc] * GLA_HEADS, axis=0), 0.0).astype(BF16))
    inters = []
    for ci, p in enumerate(pre):
        inters.append(_dot_nt(qcss[ci], st.astype(BF16)))
        st = jnp.exp(p["tot"]) * st + upds[ci]
    chunks = []
    for ci in range(n_chunks):
        heads = []
        for h in range(GLA_HEADS):
            intra_h = jnp.concatenate([intra[ci][i][h * GLA_SUB:(h + 1) * GLA_SUB] for i in range(n_sub)], axis=0)
            heads.append(intra_h + inters[ci][h * c:(h + 1) * c])
        chunks.append(jnp.concatenate(heads, axis=1))
    return jnp.concatenate(chunks, axis=0), st


def _mix_prompt_kernel(x_ref, win_ref, wup_ref, bgate_ref, ggla_ref, gret_ref, bret_ref, wout_ref,
                       lng_ref, lnb_ref, cos_ref, sin_ref, mcat_ref, dmask_ref, qdec_ref, kdec_ref,
                       y_ref, sg_ref, sr_ref, stg_ref, str_ref):
    j = pl.program_id(1)
    nj = pl.num_programs(1)

    @pl.when(j == 0)
    def _():
        stg_ref[...] = jnp.zeros_like(stg_ref)
        str_ref[...] = jnp.zeros_like(str_ref)

    tc = x_ref.shape[0]
    x = x_ref[...]
    xb = x.astype(BF16)

    def proj(off, width):
        return _dot(xb, win_ref[:, off:off + width])

    lr = proj(_OFF_LR, PAD_RANK)
    gate = _dot(lr.astype(BF16), wup_ref[...]) + bgate_ref[...]
    lg = _log_sigmoid(gate) * (1.0 / GLA_TAU)
    hi, mid, lo = _split3(lg)
    qr_all = proj(_OFF_QR, RET_QK)
    kr_all = proj(_OFF_KR, RET_QK)
    vr_all = proj(_OFF_VR, RET_V)
    qg = proj(_OFF_QG, GLA_QK) * (GLA_DK ** -0.5)
    kg = proj(_OFF_KG, GLA_QK)
    vg = proj(_OFF_VG, GLA_V)
    cs = _dot(mcat_ref[...], jnp.concatenate([hi, mid, lo], axis=1))
    cs = cs[:, :GLA_QK] + cs[:, GLA_QK:2 * GLA_QK] + cs[:, 2 * GLA_QK:]
    wn_all = cs[:tc]
    r_all = cs[tc:]

    cosf = cos_ref[...]
    sinf = sin_ref[...]
    hs = range(RET_HEADS)
    hsl = [slice(h * RET_DK, (h + 1) * RET_DK) for h in hs]
    qhs = [_rotary(qr_all[:, hsl[h]], cosf, sinf) for h in hs]
    khs = [_rotary(kr_all[:, hsl[h]], cosf, sinf) * (RET_DK ** -0.5) for h in hs]
    vhs = [vr_all[:, hsl[h]].astype(BF16) for h in hs]
    scs = [_dot_nt(qhs[h].astype(BF16), khs[h].astype(BF16)) for h in hs]
    s_hs = [str_ref[h] for h in hs]
    inter_r = [_dot((qhs[h] * qdec_ref[h]).astype(BF16), s_hs[h].astype(BF16)) for h in hs]
    upd_r = [_dot((khs[h] * kdec_ref[h]).T.astype(BF16), vhs[h]) for h in hs]
    o_r = [_dot((scs[h] * dmask_ref[h]).astype(BF16), vhs[h]) + inter_r[h] for h in hs]
    for h in hs:
        str_ref[h] = qdec_ref[h][tc - 1:tc, :] * s_hs[h] + upd_r[h]
    gr_all = proj(_OFF_GR, RET_V)
    rg = proj(_OFF_RG, GLA_V)


    c = GLA_CHUNK
    hq = GLA_HEADS * GLA_SUB
    row64 = lax.broadcasted_iota(jnp.int32, (hq, GLA_QK), 0)
    lane64 = lax.broadcasted_iota(jnp.int32, (hq, GLA_QK), 1)
    lane_head = lax.shift_right_logical(lane64, 6)
    row_head = lax.shift_right_logical(row64, 4)
    mask_q = lane_head == row_head
    masks_p = [mask_q & ((lane64 & (c - 1)) <= (row64 & (GLA_SUB - 1)) + i * GLA_SUB)
               for i in range(c // GLA_SUB)]
    rowb = lax.broadcasted_iota(jnp.int32, (GLA_HEADS * c, GLA_QK), 0)
    laneb = lax.broadcasted_iota(jnp.int32, (GLA_HEADS * c, GLA_QK), 1)
    mask_bd = lax.shift_right_logical(rowb, 6) == lax.shift_right_logical(laneb, 6)
    sblock = lax.shift_right_logical(lax.broadcasted_iota(jnp.int32, (c, GLA_QK), 0), 4)

    og, st = _gla_tile(qg, kg, vg, wn_all, r_all, stg_ref[...], mask_q, masks_p, mask_bd, sblock)
    stg_ref[...] = st
    ggla = ggla_ref[...]
    og_heads = []
    for h in range(GLA_HEADS):
        sl = slice(h * GLA_DV, (h + 1) * GLA_DV)
        o = og[:, sl]
        o = o * lax.rsqrt(jnp.mean(o * o, axis=-1, keepdims=True) + NORM_EPS) * ggla
        og_heads.append(o * _silu(rg[:, sl]))
    or_heads = []
    for h in hs:
        o = o_r[h]
        mu = jnp.mean(o, axis=-1, keepdims=True)
        oc = o - mu
        var = jnp.mean(oc * oc, axis=-1, keepdims=True)
        o = oc * lax.rsqrt(var + NORM_EPS) * gret_ref[:, hsl[h]] + bret_ref[:, hsl[h]]
        or_heads.append(o * _silu(gr_all[:, hsl[h]]))

    mixed = jnp.concatenate(og_heads + or_heads, axis=1).astype(BF16)
    half = tc // 2
    for lo_r in (0, half):
        rs = slice(lo_r, lo_r + half)
        y = _dot(mixed[rs], wout_ref[...])
        y_ref[rs, :] = _layer_norm(ALPHA * x[rs] + y, lng_ref[...], lnb_ref[...])

    @pl.when(j == nj - 1)
    def _():
        sg_ref[0] = stg_ref[...].T
        sr_ref[0] = str_ref[...]


def _mix_prompt(x, n_seq, seq_len, layer, li, win, wup, bgate, ggla, gret, bret, wout, lng, lnb, tables):
    n, d = x.shape
    tc = min(MIX_TILE, seq_len)
    assert seq_len % tc == 0 and tc % GLA_CHUNK == 0
    nt = seq_len // tc
    cosf, sinf, mcat, dmask, qdec, kdec = tables
    c2 = lambda b, j: (0, 0)
    c3 = lambda b, j: (0, 0, 0)
    buffered = lambda spec: pl.BlockSpec(spec.block_shape, spec.index_map, pipeline_mode=pl.Buffered(1))
    y, sg, sr = pl.pallas_call(
        _mix_prompt_kernel,
        grid=(n_seq, nt),
        in_specs=[
            pl.BlockSpec((tc, d), lambda b, j: (b * nt + j, 0)),
            buffered(_layer_spec(win.shape, li)),
            _layer_spec(wup.shape, li),
            _layer_spec(bgate.shape, li),
            _layer_spec(ggla.shape, li),
            _layer_spec(gret.shape, li),
            _layer_spec(bret.shape, li),
            buffered(_layer_spec(wout.shape, li)),
            _layer_spec(lng.shape, layer),
            _layer_spec(lnb.shape, layer),
            pl.BlockSpec((tc, RET_DK), lambda b, j: (j, 0)),
            pl.BlockSpec((tc, RET_DK), lambda b, j: (j, 0)),
            pl.BlockSpec((2 * tc, tc), c2),
            pl.BlockSpec((RET_HEADS, tc, tc), c3),
            pl.BlockSpec((RET_HEADS, tc, RET_DK), c3),
            pl.BlockSpec((RET_HEADS, tc, RET_DK), c3),
        ],
        out_specs=[
            pl.BlockSpec((tc, d), lambda b, j: (b * nt + j, 0)),
            pl.BlockSpec((1, GLA_QK, GLA_DV), lambda b, j: (b, 0, 0)),
            pl.BlockSpec((1, RET_HEADS, RET_DK, RET_DV), lambda b, j: (b, 0, 0, 0)),
        ],
        out_shape=[
            jax.ShapeDtypeStruct((n, d), F32),
            jax.ShapeDtypeStruct((n_seq, GLA_QK, GLA_DV), F32),
            jax.ShapeDtypeStruct((n_seq, RET_HEADS, RET_DK, RET_DV), F32),
        ],
        scratch_shapes=[
            pltpu.VMEM((GLA_DV, GLA_QK), F32),
            pltpu.VMEM((RET_HEADS, RET_DK, RET_DV), F32),
        ],
        compiler_params=pltpu.CompilerParams(
            dimension_semantics=("arbitrary", "arbitrary"), vmem_limit_bytes=VMEM_LIMIT),
        name="mix_prompt",
    )(x, win, wup, bgate, ggla, gret, bret, wout, lng, lnb, cosf, sinf, mcat, dmask, qdec, kdec)
    return y, sg.reshape(n_seq, GLA_HEADS, GLA_DK, GLA_DV), sr


def _mix_tables(seq_len):
    tc = min(MIX_TILE, seq_len)
    pos = jnp.arange(seq_len, dtype=F32)
    cosf, sinf = _rotary_tables(pos)
    t = jnp.arange(tc)
    same_sub = (t[:, None] // GLA_SUB) == (t[None, :] // GLA_SUB)
    same_chunk = (t[:, None] // GLA_CHUNK) == (t[None, :] // GLA_CHUNK)
    m_within = same_sub & (t[None, :] <= t[:, None])
    m_before = same_chunk & ((t[None, :] // GLA_SUB) < (t[:, None] // GLA_SUB))
    mcat = jnp.concatenate([m_within, m_before], axis=0).astype(BF16)
    log_gamma = jnp.log1p(-jnp.exp2(-5.0 - jnp.arange(RET_HEADS, dtype=F32)))
    tf = t.astype(F32)
    diff = tf[:, None] - tf[None, :]
    dmask = jnp.where(diff >= 0, jnp.exp(diff[None] * log_gamma[:, None, None]), 0.0)
    qdec = jnp.exp((tf[None, :, None] + 1.0) * log_gamma[:, None, None])
    kdec = jnp.exp((tc - 1.0 - tf[None, :, None]) * log_gamma[:, None, None])
    qdec = jnp.broadcast_to(qdec, (RET_HEADS, tc, RET_DK))
    kdec = jnp.broadcast_to(kdec, (RET_HEADS, tc, RET_DK))
    return cosf, sinf, mcat, dmask, qdec, kdec


def _rotary_tables(pos):
    inv_freq = 1.0 / (ROPE_BASE ** (jnp.arange(0, RET_DK, 2, dtype=F32) / RET_DK))
    ang = pos[:, None] * inv_freq[None, :]
    cos, sin = jnp.cos(ang), jnp.sin(ang)
    return jnp.concatenate([cos, cos], axis=1), jnp.concatenate([-sin, sin], axis=1)


def _ffn_sample_kernel(x_ref, st_ref, win_ref, wdw_ref, bdw_ref, wout_ref, g_ref, b_ref,
                       y_ref, nst_ref):
    x = x_ref[...]
    w2 = 2 * D_FF
    u = _dot(x.astype(BF16), win_ref[...])
    wdw = wdw_ref[...]
    c = bdw_ref[...] + wdw[0:1] * st_ref[:, :w2] + wdw[1:2] * st_ref[:, w2:] + wdw[2:3] * u
    h = _silu(c[:, D_FF:]) * c[:, :D_FF]
    y = _dot(h.astype(BF16), wout_ref[...])
    y_ref[...] = _layer_norm(ALPHA * x + y, g_ref[...], b_ref[...])
    nst_ref[:, :w2] = st_ref[:, w2:]
    nst_ref[:, w2:] = u


def _sc_sample_kernel(x_ref, st_ref, win_ref, wdw_ref, bdw_ref, wout_ref, g_ref, b_ref,
                      y_ref, nst_ref):
    x = x_ref[...]
    d = x.shape[1]
    z = _dot(x.astype(BF16), win_ref[...])
    ch = z[:, d:2 * d] * z[:, 2 * d:]
    wdw = wdw_ref[...]
    conv = bdw_ref[...] + wdw[0:1] * st_ref[:, :d] + wdw[1:2] * st_ref[:, d:] + wdw[2:3] * ch
    y = _dot((z[:, :d] * conv).astype(BF16), wout_ref[...])
    y_ref[...] = _layer_norm(ALPHA * x + y, g_ref[...], b_ref[...])
    nst_ref[:, :d] = st_ref[:, d:]
    nst_ref[:, d:] = ch


def _conv_sample(body, name, x, st, layer, li, win, wdw, bdw, wout, g, b):
    s, d = x.shape
    whole = lambda a: pl.BlockSpec(a.shape, lambda i: (0,) * a.ndim)
    return pl.pallas_call(
        body,
        grid=(1,),
        in_specs=[whole(x), whole(st), _layer_spec(win.shape, li), _layer_spec(wdw.shape, li),
                  _layer_spec(bdw.shape, li), _layer_spec(wout.shape, li),
                  _layer_spec(g.shape, layer), _layer_spec(b.shape, layer)],
        out_specs=[whole(x), whole(st)],
        out_shape=[jax.ShapeDtypeStruct((s, d), F32), jax.ShapeDtypeStruct(st.shape, F32)],
        compiler_params=pltpu.CompilerParams(
            dimension_semantics=("arbitrary",), vmem_limit_bytes=VMEM_LIMIT),
        name=name,
    )(x, st, win, wdw, bdw, wout, g, b)


def _mix_sample_proj_kernel(x_ref, win_ref, wup_ref, bgate_ref, cos_ref, sin_ref,
                            cols_ref, rows_ref, gates_ref):
    xb = x_ref[...].astype(BF16)

    def proj(off, width):
        return _dot(xb, win_ref[:, off:off + width])

    qg = proj(_OFF_QG, GLA_QK) * (GLA_DK ** -0.5)
    kg = proj(_OFF_KG, GLA_QK)
    lr = proj(_OFF_LR, PAD_RANK)
    gate = _dot(lr.astype(BF16), wup_ref[...]) + bgate_ref[...]
    ag = jnp.exp(_log_sigmoid(gate) * (1.0 / GLA_TAU))
    cosf = cos_ref[...]
    sinf = sin_ref[...]
    qr = [_rotary(proj(_OFF_QR + h * RET_DK, RET_DK), cosf, sinf) for h in range(RET_HEADS)]
    kr = [_rotary(proj(_OFF_KR + h * RET_DK, RET_DK), cosf, sinf) * (RET_DK ** -0.5)
          for h in range(RET_HEADS)]
    cols_ref[0:GLA_QK, :] = ag.T
    cols_ref[GLA_QK:2 * GLA_QK, :] = kg.T
    cols_ref[2 * GLA_QK:3 * GLA_QK, :] = qg.T
    base = 3 * GLA_QK
    for h in range(RET_HEADS):
        cols_ref[base + h * RET_DK:base + (h + 1) * RET_DK, :] = kr[h].T
        cols_ref[base + RET_QK + h * RET_DK:base + RET_QK + (h + 1) * RET_DK, :] = qr[h].T
    rows_ref[:, :GLA_V] = proj(_OFF_VG, GLA_V)
    rows_ref[:, GLA_V:] = proj(_OFF_VR, RET_V)
    gates_ref[:, :GLA_V] = proj(_OFF_RG, GLA_V)
    gates_ref[:, GLA_V:] = proj(_OFF_GR, RET_V)


_COL_ROWS = 3 * GLA_QK + 2 * RET_QK


def _mix_sample_state_kernel(*refs, n_stacked_in):
    cols_ref, rows_ref, sg_ref, sr_ref, gam_ref = refs[:5]
    nsg_ref, nsr_ref, o_ref = refs[5 + n_stacked_in:]
    nb = sg_ref.shape[0]
    s = cols_ref.shape[1]
    blk = pl.program_id(0)
    lane = lax.broadcasted_iota(jnp.int32, (1, s), 1)

    def body(bb, carry):
        b = blk * nb + bb
        onehot = (lane == b).astype(F32)
        col = jnp.sum(cols_ref[...] * onehot, axis=1, keepdims=True)
        vrow = rows_ref[b]
        a_g = col[0:GLA_QK]
        k_g = col[GLA_QK:2 * GLA_QK]
        q_g = col[2 * GLA_QK:3 * GLA_QK]
        base = 3 * GLA_QK
        outs = []
        for h in range(GLA_HEADS):
            rs = slice(h * GLA_DK, (h + 1) * GLA_DK)
            v_h = vrow[:, h * GLA_DV:(h + 1) * GLA_DV]
            s_new = a_g[rs] * sg_ref[bb, h] + k_g[rs] * v_h
            nsg_ref[bb, h] = s_new
            outs.append(jnp.sum(q_g[rs] * s_new, axis=0, keepdims=True))
        for h in range(RET_HEADS):
            rs = slice(h * RET_DK, (h + 1) * RET_DK)
            k_h = col[base + h * RET_DK:base + (h + 1) * RET_DK]
            q_h = col[base + RET_QK + h * RET_DK:base + RET_QK + (h + 1) * RET_DK]
            v_h = vrow[:, GLA_V + h * RET_DV:GLA_V + (h + 1) * RET_DV]
            s_new = gam_ref[h] * sr_ref[bb, h] + k_h * v_h
            nsr_ref[bb, h] = s_new
            outs.append(jnp.sum(q_h * s_new, axis=0, keepdims=True))
        o_ref[b] = jnp.concatenate(outs, axis=1)
        return carry

    lax.fori_loop(0, nb, body, 0)


def _mix_sample_out_kernel(x_ref, o_ref, gates_ref, ggla_ref, gret_ref, bret_ref, wout_ref,
                           lng_ref, lnb_ref, y_ref):
    x = x_ref[...]
    o = o_ref[...]
    gates = gates_ref[...]
    ggla = ggla_ref[...]
    heads = []
    for h in range(GLA_HEADS):
        sl = slice(h * GLA_DV, (h + 1) * GLA_DV)
        oh = o[:, sl]
        oh = oh * lax.rsqrt(jnp.mean(oh * oh, axis=-1, keepdims=True) + NORM_EPS) * ggla
        heads.append(oh * _silu(gates[:, sl]))
    for h in range(RET_HEADS):
        sl = slice(GLA_V + h * RET_DV, GLA_V + (h + 1) * RET_DV)
        ps = slice(h * RET_DV, (h + 1) * RET_DV)
        oh = o[:, sl]
        mu = jnp.mean(oh, axis=-1, keepdims=True)
        oc = oh - mu
        var = jnp.mean(oc * oc, axis=-1, keepdims=True)
        oh = oc * lax.rsqrt(var + NORM_EPS) * gret_ref[:, ps] + bret_ref[:, ps]
        heads.append(oh * _silu(gates[:, sl]))
    mixed = jnp.concatenate(heads, axis=1).astype(BF16)
    y = _dot(mixed, wout_ref[...])
    y_ref[...] = _layer_norm(ALPHA * x + y, lng_ref[...], lnb_ref[...])


def _mix_sample(x, state_gla, state_ret, layer, li, stacked, win, wup, bgate, ggla, gret, bret, wout, lng, lnb,
                cos_s, sin_s, gam):
    s, d = x.shape
    whole = lambda a: pl.BlockSpec(a.shape, lambda i: (0,) * a.ndim)
    proj_out = [
        jax.ShapeDtypeStruct((_COL_ROWS, s), F32),
        jax.ShapeDtypeStruct((s, GLA_V + RET_V), F32),
        jax.ShapeDtypeStruct((s, GLA_V + RET_V), F32),
    ]
    cols, rows, gates = pl.pallas_call(
        _mix_sample_proj_kernel,
        grid=(1,),
        in_specs=[whole(x), _layer_spec(win.shape, li), _layer_spec(wup.shape, li),
                  _layer_spec(bgate.shape, li), whole(cos_s), whole(sin_s)],
        out_specs=[whole(o) for o in proj_out],
        out_shape=proj_out,
        compiler_params=pltpu.CompilerParams(
            dimension_semantics=("arbitrary",), vmem_limit_bytes=VMEM_LIMIT),
        name="mix_sample_proj",
    )(x, win, wup, bgate, cos_s, sin_s)
    rows = rows.reshape(s, 1, GLA_V + RET_V)

    nb = SUBLANES
    assert s % nb == 0
    gla_spec = pl.BlockSpec((None, nb, GLA_HEADS, GLA_DK, GLA_DV), lambda i: (li, i, 0, 0, 0))
    ret_spec = pl.BlockSpec((None, nb, RET_HEADS, RET_DK, RET_DV), lambda i: (li, i, 0, 0, 0))
    stacked = () if stacked is None else tuple(stacked)
    nsg, nsr, o = pl.pallas_call(
        functools.partial(_mix_sample_state_kernel, n_stacked_in=len(stacked)),
        grid=(s // nb,),
        in_specs=[
            pl.BlockSpec((_COL_ROWS, s), lambda i: (0, 0)),
            pl.BlockSpec((s, 1, GLA_V + RET_V), lambda i: (0, 0, 0)),
            gla_spec,
            ret_spec,
            pl.BlockSpec((RET_HEADS, RET_DK, RET_DV), lambda i: (0, 0, 0)),
        ] + [pl.BlockSpec(memory_space=pl.ANY)] * len(stacked),
        out_specs=[
            gla_spec,
            ret_spec,
            pl.BlockSpec((s, 1, GLA_V + RET_V), lambda i: (0, 0, 0)),
        ],
        out_shape=[
            jax.ShapeDtypeStruct(state_gla.shape, F32),
            jax.ShapeDtypeStruct(state_ret.shape, F32),
            jax.ShapeDtypeStruct((s, 1, GLA_V + RET_V), F32),
        ],
        input_output_aliases={5 + k: k for k in range(len(stacked))},
        compiler_params=pltpu.CompilerParams(
            dimension_semantics=("arbitrary",), vmem_limit_bytes=VMEM_LIMIT),
        name="mix_sample_state",
    )(cols, rows, state_gla, state_ret, gam, *stacked)
    o = o.reshape(s, GLA_V + RET_V)

    y = pl.pallas_call(
        _mix_sample_out_kernel,
        grid=(1,),
        in_specs=[whole(x), whole(o), whole(gates), _layer_spec(ggla.shape, li), _layer_spec(gret.shape, li),
                  _layer_spec(bret.shape, li), _layer_spec(wout.shape, li),
                  _layer_spec(lng.shape, layer), _layer_spec(lnb.shape, layer)],
        out_specs=whole(x),
        out_shape=jax.ShapeDtypeStruct((s, d), F32),
        compiler_params=pltpu.CompilerParams(
            dimension_semantics=("arbitrary",), vmem_limit_bytes=VMEM_LIMIT),
        name="mix_sample_out",
    )(x, o, gates, ggla, gret, bret, wout, lng, lnb)
    return y, nsg, nsr


def _prep_mix_in(w):
    a = 2 * GLA_QK + GLA_V
    lr = jnp.pad(w[:, :, a:a + GLA_RANK], ((0, 0), (0, 0), (0, PAD_RANK - GLA_RANK)))
    return jnp.concatenate([w[:, :, :a], w[:, :, a + GLA_RANK:], lr], axis=2).astype(BF16)


def kernel(x_prompt, x_sample, state_gla, state_ret, state_conv, state_ffn,
           w_mix_in, w_gla_gate_up, b_gla_gate, g_gla_norm, g_ret_norm, b_ret_norm, w_mix_out,
           w_sc_in, w_sc_dw, b_sc_dw, w_sc_out, ln1_g, ln1_b,
           w_ffn_in, w_ffn_dw, b_ffn_dw, w_ffn_out, ln2_g, ln2_b):
    bp, tp, d = x_prompt.shape
    s, ts, _ = x_sample.shape
    assert ts == 1 and d == D_MODEL
    xp = x_prompt.reshape(bp * tp, d)
    xs = x_sample.reshape(s, d)

    tables = _mix_tables(tp)
    cos_s, sin_s = _rotary_tables(PAST_LEN + jnp.arange(ts, dtype=F32))
    log_gamma = jnp.log1p(-jnp.exp2(-5.0 - jnp.arange(RET_HEADS, dtype=F32)))
    gam = jnp.broadcast_to(jnp.exp(log_gamma)[:, None, None], (RET_HEADS, RET_DK, RET_DV))

    rows = lambda v: v[:, None, :]
    mix_w = (_prep_mix_in(w_mix_in),
             jnp.pad(w_gla_gate_up, ((0, 0), (0, PAD_RANK - GLA_RANK), (0, 0))).astype(BF16),
             rows(b_gla_gate), rows(g_gla_norm), rows(g_ret_norm), rows(b_ret_norm), w_mix_out.astype(BF16))
    sc_w = (w_sc_in.astype(BF16), w_sc_dw, rows(b_sc_dw), w_sc_out.astype(BF16))
    ffn_w = (w_ffn_in.astype(BF16), w_ffn_dw, rows(b_ffn_dw), w_ffn_out.astype(BF16))
    ln1 = (rows(ln1_g), rows(ln1_b))
    ln2 = (rows(ln2_g), rows(ln2_b))

    gla_p, ret_p, conv_p, conv_s, ffn_p, ffn_s = [], [], [], [], [], []
    stacked_s = None
    for layer in range(DEPTH):
        li = layer // 2
        if layer % 2 == 0:
            xp, sg, sr = _mix_prompt(xp, bp, tp, layer, li, *mix_w, *ln1, tables)
            gla_p.append(sg)
            ret_p.append(sr)
            xs, gla_s, ret_s = _mix_sample(xs, state_gla, state_ret, layer, li, stacked_s, *mix_w, *ln1,
                                           cos_s, sin_s, gam)
            stacked_s = (gla_s, ret_s)
        else:
            xp, sc = _sc_prompt(xp, tp, layer, li, *sc_w, *ln1)
            conv_p.append(sc)
            xs, sc = _conv_sample(_sc_sample_kernel, "sc_sample", xs, state_conv[li].reshape(s, 2 * d),
                                  layer, li, *sc_w, *ln1)
            conv_s.append(sc.reshape(s, 2, d))
        xp, sf = _ffn_prompt(xp, tp, layer, *ffn_w, *ln2)
        ffn_p.append(sf)
        xs, sf = _conv_sample(_ffn_sample_kernel, "ffn_sample", xs, state_ffn[layer].reshape(s, 4 * D_FF),
                              layer, layer, *ffn_w, *ln2)
        ffn_s.append(sf.reshape(s, 2, 2 * D_FF))

    return (xp.reshape(bp, tp, d), xs.reshape(s, ts, d),
            jnp.stack(gla_p), gla_s, jnp.stack(ret_p), ret_s,
            jnp.stack(conv_p), jnp.stack(conv_s), jnp.stack(ffn_p), jnp.stack(ffn_s))
```

```python
import functools

import jax
import jax.numpy as jnp
from jax import lax
from jax.experimental import pallas as pl
from jax.experimental.pallas import tpu as pltpu

F32 = jnp.float32
BF16 = jnp.bfloat16

D_MODEL = 1024
DEPTH = 4
PAST_LEN = 16384
GLA_HEADS = 4
GLA_DK = 64
GLA_DV = 128
GLA_RANK = 16
GLA_TAU = 16.0
RET_HEADS = 4
RET_DK = 128
RET_DV = 128
CONV_W = 3
D_FF = D_MODEL * 11 // 4
ROPE_BASE = 10000.0
NORM_EPS = 1e-5
ALPHA = (2 * DEPTH) ** 0.25
GLA_QK = GLA_HEADS * GLA_DK
GLA_V = GLA_HEADS * GLA_DV
RET_QK = RET_HEADS * RET_DK
RET_V = RET_HEADS * RET_DV

LANES = 128
SUBLANES = 8
VMEM_LIMIT = 56 * 1024 * 1024

GLA_CHUNK = 64
GLA_SUB = 16
MIX_TILE = 256
ROW_TILE = 512
CONV_ROWS = 32
FF_CHUNK = 768
SC_CHUNK = 256
PAD_RANK = LANES

_OFF_QG = 0
_OFF_KG = _OFF_QG + GLA_QK
_OFF_VG = _OFF_KG + GLA_QK
_OFF_RG = _OFF_VG + GLA_V
_OFF_QR = _OFF_RG + GLA_V
_OFF_KR = _OFF_QR + RET_QK
_OFF_VR = _OFF_KR + RET_QK
_OFF_GR = _OFF_VR + RET_V
_OFF_LR = _OFF_GR + RET_V
MIX_COLS = _OFF_LR + PAD_RANK


def _dot(a, b):
    return jnp.dot(a, b, preferred_element_type=F32)


def _dot_nt(a, b):
    return lax.dot_general(a, b, (((1,), (1,)), ((), ())), preferred_element_type=F32)


def _layer_norm(x, g, b):
    mu = jnp.mean(x, axis=-1, keepdims=True)
    xc = x - mu
    var = jnp.mean(xc * xc, axis=-1, keepdims=True)
    return xc * lax.rsqrt(var + NORM_EPS) * g + b


def _silu(x):
    return x * jax.nn.sigmoid(x)


def _log_sigmoid(x):
    return jnp.minimum(x, 0.0) - jnp.log1p(jnp.exp(-jnp.abs(x)))


def _split3(x):
    hi = x.astype(BF16)
    r1 = x - hi.astype(F32)
    mid = r1.astype(BF16)
    lo = (r1 - mid.astype(F32)).astype(BF16)
    return hi, mid, lo


def _causal_conv3_blocks(rows, n_rows, prev, w, b, post):
    w0, w1, w2 = w[0:1], w[1:2], w[2:3]
    out = []
    for lo in range(0, n_rows, CONV_ROWS):
        head = prev if lo == 0 else rows(lo - SUBLANES, lo)
        s = jnp.concatenate([head, rows(lo, lo + CONV_ROWS)], axis=0)
        s1 = pltpu.roll(s, 1, 0)[SUBLANES:]
        s2 = pltpu.roll(s, 2, 0)[SUBLANES:]
        out.append(post(lo, b + w0 * s2 + w1 * s1 + w2 * s[SUBLANES:]))
    return jnp.concatenate(out, axis=0)


def _ffn_prompt_kernel(x_ref, win_ref, wdw_ref, bdw_ref, wout_ref, g_ref, b_ref,
                       y_ref, st_ref, xb_ref, carry_ref, *, tiles_per_seq):
    i = pl.program_id(0)

    @pl.when(i % tiles_per_seq == 0)
    def _():
        carry_ref[...] = jnp.zeros_like(carry_ref)

    tm = x_ref.shape[0]
    ff = wout_ref.shape[0]
    bounds = list(range(0, ff, FF_CHUNK)) + [ff]
    n_chunks = len(bounds) - 1
    xb_ref[...] = x_ref[...].astype(BF16)

    def cols(ref, f):
        lo, hi = bounds[f], bounds[f + 1]
        return jnp.concatenate([ref[:, lo:hi], ref[:, ff + lo:ff + hi]], axis=1)

    def up(f):
        lo, hi = bounds[f], bounds[f + 1]
        xb = xb_ref[...]
        return jnp.concatenate([_dot(xb, win_ref[:, lo:hi]), _dot(xb, win_ref[:, ff + lo:ff + hi])], axis=1)

    def act(f, u):
        lo, hi = bounds[f], bounds[f + 1]
        fc = hi - lo
        h = _causal_conv3_blocks(
            lambda r0, r1: u[r0:r1], tm, cols(carry_ref, f), cols(wdw_ref, f), cols(bdw_ref, f),
            lambda r0, c: (_silu(c[:, fc:]) * c[:, :fc]).astype(BF16))
        for k in range(2):
            tail = u[tm - SUBLANES:, k * fc:(k + 1) * fc]
            carry_ref[:, k * ff + lo:k * ff + hi] = tail
            st_ref[0, :, k * ff + lo:k * ff + hi] = tail
        return h

    def down(f, h):
        return _dot(h, wout_ref[bounds[f]:bounds[f + 1], :])

    u = up(0)
    h_prev = None
    acc = None
    for f in range(n_chunks):
        u_next = up(f + 1) if f + 1 < n_chunks else None
        h = act(f, u)
        if h_prev is not None:
            p = down(f - 1, h_prev)
            acc = p if acc is None else acc + p
        u, h_prev = u_next, h
    p = down(n_chunks - 1, h_prev)
    acc = p if acc is None else acc + p
    y_ref[...] = _layer_norm(ALPHA * x_ref[...] + acc, g_ref[...], b_ref[...])


def _layer_spec(shape, layer):
    zeros = (0,) * (len(shape) - 1)
    return pl.BlockSpec((None,) + tuple(shape[1:]), lambda *_: (layer,) + zeros)


def _ffn_prompt(x, seq_len, layer, win, wdw, bdw, wout, g, b):
    n, d = x.shape
    tm = min(ROW_TILE, seq_len)
    assert seq_len % tm == 0 and n % seq_len == 0
    n_seq = n // seq_len
    tps = seq_len // tm
    ff2 = win.shape[2]
    buffered = lambda spec: pl.BlockSpec(spec.block_shape, spec.index_map, pipeline_mode=pl.Buffered(1))
    y, st = pl.pallas_call(
        functools.partial(_ffn_prompt_kernel, tiles_per_seq=tps),
        grid=(n // tm,),
        in_specs=[
            pl.BlockSpec((tm, d), lambda i: (i, 0)),
            buffered(_layer_spec(win.shape, layer)),
            _layer_spec(wdw.shape, layer),
            _layer_spec(bdw.shape, layer),
            buffered(_layer_spec(wout.shape, layer)),
            _layer_spec(g.shape, layer),
            _layer_spec(b.shape, layer),
        ],
        out_specs=[
            pl.BlockSpec((tm, d), lambda i: (i, 0)),
            pl.BlockSpec((1, SUBLANES, ff2), lambda i: (i // tps, 0, 0)),
        ],
        out_shape=[
            jax.ShapeDtypeStruct((n, d), F32),
            jax.ShapeDtypeStruct((n_seq, SUBLANES, ff2), F32),
        ],
        scratch_shapes=[
            pltpu.VMEM((tm, d), BF16),
            pltpu.VMEM((SUBLANES, ff2), F32),
        ],
        compiler_params=pltpu.CompilerParams(
            dimension_semantics=("arbitrary",), vmem_limit_bytes=VMEM_LIMIT),
        name="ffn_prompt",
    )(x, win, wdw, bdw, wout, g, b)
    return y, st[:, SUBLANES - 2:, :]


def _sc_prompt_kernel(x_ref, win_ref, wdw_ref, bdw_ref, wout_ref, g_ref, b_ref,
                      y_ref, st_ref, xb_ref, carry_ref, *, tiles_per_seq):
    i = pl.program_id(0)

    @pl.when(i % tiles_per_seq == 0)
    def _():
        carry_ref[...] = jnp.zeros_like(carry_ref)

    tm, d = x_ref.shape
    cw = SC_CHUNK
    n_chunks = d // cw
    xb_ref[...] = x_ref[...].astype(BF16)

    def up(f):
        xb = xb_ref[...]
        return jnp.concatenate([_dot(xb, win_ref[:, k * d + f * cw:k * d + (f + 1) * cw]) for k in range(3)],
                               axis=1)

    def act(f, z):
        cs = slice(f * cw, (f + 1) * cw)
        ch_rows = lambda lo, hi: z[lo:hi, cw:2 * cw] * z[lo:hi, 2 * cw:]
        m = _causal_conv3_blocks(
            ch_rows, tm, carry_ref[:, cs], wdw_ref[:, cs], bdw_ref[:, cs],
            lambda lo, conv: (z[lo:lo + CONV_ROWS, :cw] * conv).astype(BF16))
        tail = ch_rows(tm - SUBLANES, tm)
        carry_ref[:, cs] = tail
        st_ref[0, :, cs] = tail
        return m

    def down(f, m):
        return _dot(m, wout_ref[f * cw:(f + 1) * cw, :])

    z = up(0)
    m_prev = None
    acc = None
    for f in range(n_chunks):
        z_next = up(f + 1) if f + 1 < n_chunks else None
        m = act(f, z)
        if m_prev is not None:
            p = down(f - 1, m_prev)
            acc = p if acc is None else acc + p
        z, m_prev = z_next, m
    p = down(n_chunks - 1, m_prev)
    acc = p if acc is None else acc + p
    y_ref[...] = _layer_norm(ALPHA * x_ref[...] + acc, g_ref[...], b_ref[...])


def _sc_prompt(x, seq_len, layer, li, win, wdw, bdw, wout, g, b):
    n, d = x.shape
    tm = min(ROW_TILE, seq_len)
    assert seq_len % tm == 0 and n % seq_len == 0
    n_seq = n // seq_len
    tps = seq_len // tm
    buffered = lambda spec: pl.BlockSpec(spec.block_shape, spec.index_map, pipeline_mode=pl.Buffered(1))
    y, st = pl.pallas_call(
        functools.partial(_sc_prompt_kernel, tiles_per_seq=tps),
        grid=(n // tm,),
        in_specs=[
            pl.BlockSpec((tm, d), lambda i: (i, 0)),
            buffered(_layer_spec(win.shape, li)),
            _layer_spec(wdw.shape, li),
            _layer_spec(bdw.shape, li),
            buffered(_layer_spec(wout.shape, li)),
            _layer_spec(g.shape, layer),
            _layer_spec(b.shape, layer),
        ],
        out_specs=[
            pl.BlockSpec((tm, d), lambda i: (i, 0)),
            pl.BlockSpec((1, SUBLANES, d), lambda i: (i // tps, 0, 0)),
        ],
        out_shape=[
            jax.ShapeDtypeStruct((n, d), F32),
            jax.ShapeDtypeStruct((n_seq, SUBLANES, d), F32),
        ],
        scratch_shapes=[
            pltpu.VMEM((tm, d), BF16),
            pltpu.VMEM((SUBLANES, d), F32),
        ],
        compiler_params=pltpu.CompilerParams(
            dimension_semantics=("arbitrary",), vmem_limit_bytes=VMEM_LIMIT),
        name="sc_prompt",
    )(x, win, wdw, bdw, wout, g, b)
    return y, st[:, SUBLANES - 2:, :]


def _rotary(x, cosf, sinf):
    return x * cosf + pltpu.roll(x, RET_DK // 2, 1) * sinf


def _gla_tile(qg, kg, vg, wn_all, r_all, st, mask_q, masks_p, mask_bd, sblock):
    c = GLA_CHUNK
    n_chunks = qg.shape[0] // c
    n_sub = c // GLA_SUB
    pre = []
    for ci in range(n_chunks):
        sl = slice(ci * c, (ci + 1) * c)
        q_c, k_c, v_c, wn, r = qg[sl], kg[sl], vg[sl], wn_all[sl], r_all[sl]
        cum = r + wn
        tot = cum[c - 1:c, :]
        vst = jnp.concatenate([v_c[:, h * GLA_DV:(h + 1) * GLA_DV] for h in range(GLA_HEADS)], axis=0)
        pre.append(dict(q=q_c, k=k_c, r=r, cum=cum, tot=tot, qw=q_c * jnp.exp(wn), vst=vst,
                        vst_b=vst.astype(BF16)))
    scores = []
    for p in pre:
        row = []
        for i in range(n_sub):
            r_i = p["r"][i * GLA_SUB:i * GLA_SUB + 1, :]
            kt = jnp.where(sblock <= i, p["k"] * jnp.exp(r_i - p["cum"]), 0.0)
            kbig = jnp.concatenate([kt] * GLA_HEADS, axis=0).astype(BF16)
            q16 = p["qw"][i * GLA_SUB:(i + 1) * GLA_SUB]
            qs = jnp.where(mask_q, jnp.concatenate([q16] * GLA_HEADS, axis=0), 0.0).astype(BF16)
            row.append(_dot_nt(qs, kbig))
        scores.append(row)
    intra = [[_dot(jnp.where(masks_p[i], scores[ci][i], 0.0).astype(BF16), p["vst_b"])
              for i in range(n_sub)] for ci, p in enumerate(pre)]
    upds, qcss = [], []
    for p in pre:
        kd = p["k"] * jnp.exp(p["tot"] - p["cum"])
        kdbig = jnp.where(mask_bd, jnp.concatenate([kd] * GLA_HEADS, axis=0), 0.0).astype(BF16)
        upds.append(_dot(p["vst"].T.astype(BF16), kdbig))
        qc = p["q"] * jnp.exp(p["cum"])
        qcss.append(jnp.where(mask_bd, jnp.concatenate([qc] * GLA_HEADS, axis=0), 0.0).astype(BF16))
    inters = []
    for ci, p in enumerate(pre):
        inters.append(_dot_nt(qcss[ci], st.astype(BF16)))
        st = jnp.exp(p["tot"]) * st + upds[ci]
    chunks = []
    for ci in range(n_chunks):
        heads = []
        for h in range(GLA_HEADS):
            intra_h = jnp.concatenate([intra[ci][i][h * GLA_SUB:(h + 1) * GLA_SUB] for i in range(n_sub)], axis=0)
            heads.append(intra_h + inters[ci][h * c:(h + 1) * c])
        chunks.append(jnp.concatenate(heads, axis=1))
    return jnp.concatenate(chunks, axis=0), st


def _mix_prompt_kernel(x_ref, win_ref, wup_ref, bgate_ref, ggla_ref, gret_ref, bret_ref, wout_ref,
                       lng_ref, lnb_ref, cos_ref, sin_ref, mcat_ref, dmask_ref, qdec_ref, kdec_ref,
                       y_ref, sg_ref, sr_ref, stg_ref, str_ref):
    j = pl.program_id(1)
    nj = pl.num_programs(1)

    @pl.when(j == 0)
    def _():
        stg_ref[...] = jnp.zeros_like(stg_ref)
        str_ref[...] = jnp.zeros_like(str_ref)

    tc = x_ref.shape[0]
    x = x_ref[...]
    xb = x.astype(BF16)

    def proj(off, width):
        return _dot(xb, win_ref[:, off:off + width])

    lr = proj(_OFF_LR, PAD_RANK)
    gate = _dot(lr.astype(BF16), wup_ref[...]) + bgate_ref[...]
    lg = _log_sigmoid(gate) * (1.0 / GLA_TAU)
    hi, mid, lo = _split3(lg)
    qr_all = proj(_OFF_QR, RET_QK)
    kr_all = proj(_OFF_KR, RET_QK)
    vr_all = proj(_OFF_VR, RET_V)
    qg = proj(_OFF_QG, GLA_QK) * (GLA_DK ** -0.5)
    kg = proj(_OFF_KG, GLA_QK)
    vg = proj(_OFF_VG, GLA_V)
    cs = _dot(mcat_ref[...], jnp.concatenate([hi, mid, lo], axis=1))
    cs = cs[:, :GLA_QK] + cs[:, GLA_QK:2 * GLA_QK] + cs[:, 2 * GLA_QK:]
    wn_all = cs[:tc]
    r_all = cs[tc:]

    cosf = cos_ref[...]
    sinf = sin_ref[...]
    hs = range(RET_HEADS)
    hsl = [slice(h * RET_DK, (h + 1) * RET_DK) for h in hs]
    qhs = [_rotary(qr_all[:, hsl[h]], cosf, sinf) for h in hs]
    khs = [_rotary(kr_all[:, hsl[h]], cosf, sinf) * (RET_DK ** -0.5) for h in hs]
    vhs = [vr_all[:, hsl[h]].astype(BF16) for h in hs]
    scs = [_dot_nt(qhs[h].astype(BF16), khs[h].astype(BF16)) for h in hs]
    s_hs = [str_ref[h] for h in hs]
    inter_r = [_dot((qhs[h] * qdec_ref[h]).astype(BF16), s_hs[h].astype(BF16)) for h in hs]
    upd_r = [_dot((khs[h] * kdec_ref[h]).T.astype(BF16), vhs[h]) for h in hs]
    o_r = [_dot((scs[h] * dmask_ref[h]).astype(BF16), vhs[h]) + inter_r[h] for h in hs]
    for h in hs:
        str_ref[h] = qdec_ref[h][tc - 1:tc, :] * s_hs[h] + upd_r[h]
    gr_all = proj(_OFF_GR, RET_V)
    rg = proj(_OFF_RG, GLA_V)


    c = GLA_CHUNK
    hq = GLA_HEADS * GLA_SUB
    row64 = lax.broadcasted_iota(jnp.int32, (hq, GLA_QK), 0)
    lane64 = lax.broadcasted_iota(jnp.int32, (hq, GLA_QK), 1)
    lane_head = lax.shift_right_logical(lane64, 6)
    row_head = lax.shift_right_logical(row64, 4)
    mask_q = lane_head == row_head
    masks_p = [mask_q & ((lane64 & (c - 1)) <= (row64 & (GLA_SUB - 1)) + i * GLA_SUB)
               for i in range(c // GLA_SUB)]
    rowb = lax.broadcasted_iota(jnp.int32, (GLA_HEADS * c, GLA_QK), 0)
    laneb = lax.broadcasted_iota(jnp.int32, (GLA_HEADS * c, GLA_QK), 1)
    mask_bd = lax.shift_right_logical(rowb, 6) == lax.shift_right_logical(laneb, 6)
    sblock = lax.shift_right_logical(lax.broadcasted_iota(jnp.int32, (c, GLA_QK), 0), 4)

    og, st = _gla_tile(qg, kg, vg, wn_all, r_all, stg_ref[...], mask_q, masks_p, mask_bd, sblock)
    stg_ref[...] = st
    ggla = ggla_ref[...]
    og_heads = []
    for h in range(GLA_HEADS):
        sl = slice(h * GLA_DV, (h + 1) * GLA_DV)
        o = og[:, sl]
        o = o * lax.rsqrt(jnp.mean(o * o, axis=-1, keepdims=True) + NORM_EPS) * ggla
        og_heads.append(o * _silu(rg[:, sl]))
    or_heads = []
    for h in hs:
        o = o_r[h]
        mu = jnp.mean(o, axis=-1, keepdims=True)
        oc = o - mu
        var = jnp.mean(oc * oc, axis=-1, keepdims=True)
        o = oc * lax.rsqrt(var + NORM_EPS) * gret_ref[:, hsl[h]] + bret_ref[:, hsl[h]]
        or_heads.append(o * _silu(gr_all[:, hsl[h]]))

    mixed = jnp.concatenate(og_heads + or_heads, axis=1).astype(BF16)
    half = tc // 2
    for lo_r in (0, half):
        rs = slice(lo_r, lo_r + half)
        y = _dot(mixed[rs], wout_ref[...])
        y_ref[rs, :] = _layer_norm(ALPHA * x[rs] + y, lng_ref[...], lnb_ref[...])

    @pl.when(j == nj - 1)
    def _():
        sg_ref[0] = stg_ref[...].T
        sr_ref[0] = str_ref[...]


def _mix_prompt(x, n_seq, seq_len, layer, li, win, wup, bgate, ggla, gret, bret, wout, lng, lnb, tables):
    n, d = x.shape
    tc = min(MIX_TILE, seq_len)
    assert seq_len % tc == 0 and tc % GLA_CHUNK == 0
    nt = seq_len // tc
    cosf, sinf, mcat, dmask, qdec, kdec = tables
    c2 = lambda b, j: (0, 0)
    c3 = lambda b, j: (0, 0, 0)
    buffered = lambda spec: pl.BlockSpec(spec.block_shape, spec.index_map, pipeline_mode=pl.Buffered(1))
    y, sg, sr = pl.pallas_call(
        _mix_prompt_kernel,
        grid=(n_seq, nt),
        in_specs=[
            pl.BlockSpec((tc, d), lambda b, j: (b * nt + j, 0)),
            buffered(_layer_spec(win.shape, li)),
            _layer_spec(wup.shape, li),
            _layer_spec(bgate.shape, li),
            _layer_spec(ggla.shape, li),
            _layer_spec(gret.shape, li),
            _layer_spec(bret.shape, li),
            buffered(_layer_spec(wout.shape, li)),
            _layer_spec(lng.shape, layer),
            _layer_spec(lnb.shape, layer),
            pl.BlockSpec((tc, RET_DK), lambda b, j: (j, 0)),
            pl.BlockSpec((tc, RET_DK), lambda b, j: (j, 0)),
            pl.BlockSpec((2 * tc, tc), c2),
            pl.BlockSpec((RET_HEADS, tc, tc), c3),
            pl.BlockSpec((RET_HEADS, tc, RET_DK), c3),
            pl.BlockSpec((RET_HEADS, tc, RET_DK), c3),
        ],
        out_specs=[
            pl.BlockSpec((tc, d), lambda b, j: (b * nt + j, 0)),
            pl.BlockSpec((1, GLA_QK, GLA_DV), lambda b, j: (b, 0, 0)),
            pl.BlockSpec((1, RET_HEADS, RET_DK, RET_DV), lambda b, j: (b, 0, 0, 0)),
        ],
        out_shape=[
            jax.ShapeDtypeStruct((n, d), F32),
            jax.ShapeDtypeStruct((n_seq, GLA_QK, GLA_DV), F32),
            jax.ShapeDtypeStruct((n_seq, RET_HEADS, RET_DK, RET_DV), F32),
        ],
        scratch_shapes=[
            pltpu.VMEM((GLA_DV, GLA_QK), F32),
            pltpu.VMEM((RET_HEADS, RET_DK, RET_DV), F32),
        ],
        compiler_params=pltpu.CompilerParams(
            dimension_semantics=("arbitrary", "arbitrary"), vmem_limit_bytes=VMEM_LIMIT),
        name="mix_prompt",
    )(x, win, wup, bgate, ggla, gret, bret, wout, lng, lnb, cosf, sinf, mcat, dmask, qdec, kdec)
    return y, sg.reshape(n_seq, GLA_HEADS, GLA_DK, GLA_DV), sr


def _mix_tables(seq_len):
    tc = min(MIX_TILE, seq_len)
    pos = jnp.arange(seq_len, dtype=F32)
    cosf, sinf = _rotary_tables(pos)
    t = jnp.arange(tc)
    same_sub = (t[:, None] // GLA_SUB) == (t[None, :] // GLA_SUB)
    same_chunk = (t[:, None] // GLA_CHUNK) == (t[None, :] // GLA_CHUNK)
    m_within = same_sub & (t[None, :] <= t[:, None])
    m_before = same_chunk & ((t[None, :] // GLA_SUB) < (t[:, None] // GLA_SUB))
    mcat = jnp.concatenate([m_within, m_before], axis=0).astype(BF16)
    log_gamma = jnp.log1p(-jnp.exp2(-5.0 - jnp.arange(RET_HEADS, dtype=F32)))
    tf = t.astype(F32)
    diff = tf[:, None] - tf[None, :]
    dmask = jnp.where(diff >= 0, jnp.exp(diff[None] * log_gamma[:, None, None]), 0.0)
    qdec = jnp.exp((tf[None, :, None] + 1.0) * log_gamma[:, None, None])
    kdec = jnp.exp((tc - 1.0 - tf[None, :, None]) * log_gamma[:, None, None])
    qdec = jnp.broadcast_to(qdec, (RET_HEADS, tc, RET_DK))
    kdec = jnp.broadcast_to(kdec, (RET_HEADS, tc, RET_DK))
    return cosf, sinf, mcat, dmask, qdec, kdec


def _rotary_tables(pos):
    inv_freq = 1.0 / (ROPE_BASE ** (jnp.arange(0, RET_DK, 2, dtype=F32) / RET_DK))
    ang = pos[:, None] * inv_freq[None, :]
    cos, sin = jnp.cos(ang), jnp.sin(ang)
    return jnp.concatenate([cos, cos], axis=1), jnp.concatenate([-sin, sin], axis=1)


def _ffn_sample_kernel(x_ref, st_ref, win_ref, wdw_ref, bdw_ref, wout_ref, g_ref, b_ref,
                       y_ref, nst_ref):
    x = x_ref[...]
    u = _dot(x.astype(BF16), win_ref[...])
    wdw = wdw_ref[...]
    b1 = st_ref[:, 1, :]
    c = bdw_ref[...] + wdw[0:1] * st_ref[:, 0, :] + wdw[1:2] * b1 + wdw[2:3] * u
    h = _silu(c[:, D_FF:]) * c[:, :D_FF]
    y = _dot(h.astype(BF16), wout_ref[...])
    y_ref[...] = _layer_norm(ALPHA * x + y, g_ref[...], b_ref[...])
    nst_ref[:, 0, :] = b1
    nst_ref[:, 1, :] = u


def _sc_sample_kernel(x_ref, st_ref, win_ref, wdw_ref, bdw_ref, wout_ref, g_ref, b_ref,
                      y_ref, nst_ref):
    x = x_ref[...]
    d = x.shape[1]
    z = _dot(x.astype(BF16), win_ref[...])
    ch = z[:, d:2 * d] * z[:, 2 * d:]
    wdw = wdw_ref[...]
    b1 = st_ref[:, 1, :]
    conv = bdw_ref[...] + wdw[0:1] * st_ref[:, 0, :] + wdw[1:2] * b1 + wdw[2:3] * ch
    y = _dot((z[:, :d] * conv).astype(BF16), wout_ref[...])
    y_ref[...] = _layer_norm(ALPHA * x + y, g_ref[...], b_ref[...])
    nst_ref[:, 0, :] = b1
    nst_ref[:, 1, :] = ch


def _conv_sample_kernel(body, *refs):
    body(*refs[:8], *refs[9:])


def _conv_sample(body, name, x, states, new_states, layer, li, win, wdw, bdw, wout, g, b):
    s, d = x.shape
    whole = lambda a: pl.BlockSpec(a.shape, lambda i: (0,) * a.ndim)
    st_spec = _layer_spec(states.shape, li)
    return pl.pallas_call(
        functools.partial(_conv_sample_kernel, body),
        grid=(1,),
        in_specs=[whole(x), st_spec, _layer_spec(win.shape, li), _layer_spec(wdw.shape, li),
                  _layer_spec(bdw.shape, li), _layer_spec(wout.shape, li),
                  _layer_spec(g.shape, layer), _layer_spec(b.shape, layer),
                  pl.BlockSpec(memory_space=pl.ANY)],
        out_specs=[whole(x), st_spec],
        out_shape=[jax.ShapeDtypeStruct((s, d), F32), jax.ShapeDtypeStruct(states.shape, F32)],
        input_output_aliases={8: 1},
        compiler_params=pltpu.CompilerParams(
            dimension_semantics=("arbitrary",), vmem_limit_bytes=VMEM_LIMIT),
        name=name,
    )(x, states, win, wdw, bdw, wout, g, b, new_states)


def _mix_sample_proj_kernel(x_ref, win_ref, wup_ref, bgate_ref, cos_ref, sin_ref,
                            cols_ref, rows_ref, gates_ref):
    xb = x_ref[...].astype(BF16)

    def proj(off, width):
        return _dot(xb, win_ref[:, off:off + width])

    qg = proj(_OFF_QG, GLA_QK) * (GLA_DK ** -0.5)
    kg = proj(_OFF_KG, GLA_QK)
    lr = proj(_OFF_LR, PAD_RANK)
    gate = _dot(lr.astype(BF16), wup_ref[...]) + bgate_ref[...]
    ag = jnp.exp(_log_sigmoid(gate) * (1.0 / GLA_TAU))
    cosf = cos_ref[...]
    sinf = sin_ref[...]
    qr = [_rotary(proj(_OFF_QR + h * RET_DK, RET_DK), cosf, sinf) for h in range(RET_HEADS)]
    kr = [_rotary(proj(_OFF_KR + h * RET_DK, RET_DK), cosf, sinf) * (RET_DK ** -0.5)
          for h in range(RET_HEADS)]
    cols_ref[0:GLA_QK, :] = ag.T
    cols_ref[GLA_QK:2 * GLA_QK, :] = kg.T
    cols_ref[2 * GLA_QK:3 * GLA_QK, :] = qg.T
    base = 3 * GLA_QK
    for h in range(RET_HEADS):
        cols_ref[base + h * RET_DK:base + (h + 1) * RET_DK, :] = kr[h].T
        cols_ref[base + RET_QK + h * RET_DK:base + RET_QK + (h + 1) * RET_DK, :] = qr[h].T
    rows_ref[:, :GLA_V] = proj(_OFF_VG, GLA_V)
    rows_ref[:, GLA_V:] = proj(_OFF_VR, RET_V)
    gates_ref[:, :GLA_V] = proj(_OFF_RG, GLA_V)
    gates_ref[:, GLA_V:] = proj(_OFF_GR, RET_V)


_COL_ROWS = 3 * GLA_QK + 2 * RET_QK


def _mix_sample_state_kernel(*refs, n_stacked_in):
    cols_ref, rows_ref, sg_ref, sr_ref, gam_ref = refs[:5]
    nsg_ref, nsr_ref, o_ref = refs[5 + n_stacked_in:]
    nb = sg_ref.shape[0]
    s = cols_ref.shape[1]
    blk = pl.program_id(0)
    lane = lax.broadcasted_iota(jnp.int32, (1, s), 1)

    def body(bb, carry):
        b = blk * nb + bb
        onehot = (lane == b).astype(F32)
        col = jnp.sum(cols_ref[...] * onehot, axis=1, keepdims=True)
        vrow = rows_ref[b]
        a_g = col[0:GLA_QK]
        k_g = col[GLA_QK:2 * GLA_QK]
        q_g = col[2 * GLA_QK:3 * GLA_QK]
        base = 3 * GLA_QK
        outs = []
        for h in range(GLA_HEADS):
            rs = slice(h * GLA_DK, (h + 1) * GLA_DK)
            v_h = vrow[:, h * GLA_DV:(h + 1) * GLA_DV]
            s_new = a_g[rs] * sg_ref[bb, h] + k_g[rs] * v_h
            nsg_ref[bb, h] = s_new
            outs.append(jnp.sum(q_g[rs] * s_new, axis=0, keepdims=True))
        for h in range(RET_HEADS):
            rs = slice(h * RET_DK, (h + 1) * RET_DK)
            k_h = col[base + h * RET_DK:base + (h + 1) * RET_DK]
            q_h = col[base + RET_QK + h * RET_DK:base + RET_QK + (h + 1) * RET_DK]
            v_h = vrow[:, GLA_V + h * RET_DV:GLA_V + (h + 1) * RET_DV]
            s_new = gam_ref[h] * sr_ref[bb, h] + k_h * v_h
            nsr_ref[bb, h] = s_new
            outs.append(jnp.sum(q_h * s_new, axis=0, keepdims=True))
        o_ref[b] = jnp.concatenate(outs, axis=1)
        return carry

    lax.fori_loop(0, nb, body, 0)


def _mix_sample_out_kernel(x_ref, o_ref, gates_ref, ggla_ref, gret_ref, bret_ref, wout_ref,
                           lng_ref, lnb_ref, y_ref):
    x = x_ref[...]
    o = o_ref[...]
    gates = gates_ref[...]
    ggla = ggla_ref[...]
    heads = []
    for h in range(GLA_HEADS):
        sl = slice(h * GLA_DV, (h + 1) * GLA_DV)
        oh = o[:, sl]
        oh = oh * lax.rsqrt(jnp.mean(oh * oh, axis=-1, keepdims=True) + NORM_EPS) * ggla
        heads.append(oh * _silu(gates[:, sl]))
    for h in range(RET_HEADS):
        sl = slice(GLA_V + h * RET_DV, GLA_V + (h + 1) * RET_DV)
        ps = slice(h * RET_DV, (h + 1) * RET_DV)
        oh = o[:, sl]
        mu = jnp.mean(oh, axis=-1, keepdims=True)
        oc = oh - mu
        var = jnp.mean(oc * oc, axis=-1, keepdims=True)
        oh = oc * lax.rsqrt(var + NORM_EPS) * gret_ref[:, ps] + bret_ref[:, ps]
        heads.append(oh * _silu(gates[:, sl]))
    mixed = jnp.concatenate(heads, axis=1).astype(BF16)
    y = _dot(mixed, wout_ref[...])
    y_ref[...] = _layer_norm(ALPHA * x + y, lng_ref[...], lnb_ref[...])


def _mix_sample(x, state_gla, state_ret, layer, li, stacked, win, wup, bgate, ggla, gret, bret, wout, lng, lnb,
                cos_s, sin_s, gam):
    s, d = x.shape
    whole = lambda a: pl.BlockSpec(a.shape, lambda i: (0,) * a.ndim)
    proj_out = [
        jax.ShapeDtypeStruct((_COL_ROWS, s), F32),
        jax.ShapeDtypeStruct((s, GLA_V + RET_V), F32),
        jax.ShapeDtypeStruct((s, GLA_V + RET_V), F32),
    ]
    cols, rows, gates = pl.pallas_call(
        _mix_sample_proj_kernel,
        grid=(1,),
        in_specs=[whole(x), _layer_spec(win.shape, li), _layer_spec(wup.shape, li),
                  _layer_spec(bgate.shape, li), whole(cos_s), whole(sin_s)],
        out_specs=[whole(o) for o in proj_out],
        out_shape=proj_out,
        compiler_params=pltpu.CompilerParams(
            dimension_semantics=("arbitrary",), vmem_limit_bytes=VMEM_LIMIT),
        name="mix_sample_proj",
    )(x, win, wup, bgate, cos_s, sin_s)
    rows = rows.reshape(s, 1, GLA_V + RET_V)

    nb = SUBLANES
    assert s % nb == 0
    gla_spec = pl.BlockSpec((None, nb, GLA_HEADS, GLA_DK, GLA_DV), lambda i: (li, i, 0, 0, 0))
    ret_spec = pl.BlockSpec((None, nb, RET_HEADS, RET_DK, RET_DV), lambda i: (li, i, 0, 0, 0))
    stacked = tuple(stacked)
    nsg, nsr, o = pl.pallas_call(
        functools.partial(_mix_sample_state_kernel, n_stacked_in=len(stacked)),
        grid=(s // nb,),
        in_specs=[
            pl.BlockSpec((_COL_ROWS, s), lambda i: (0, 0)),
            pl.BlockSpec((s, 1, GLA_V + RET_V), lambda i: (0, 0, 0)),
            gla_spec,
            ret_spec,
            pl.BlockSpec((RET_HEADS, RET_DK, RET_DV), lambda i: (0, 0, 0)),
        ] + [pl.BlockSpec(memory_space=pl.ANY)] * len(stacked),
        out_specs=[
            gla_spec,
            ret_spec,
            pl.BlockSpec((s, 1, GLA_V + RET_V), lambda i: (0, 0, 0)),
        ],
        out_shape=[
            jax.ShapeDtypeStruct(state_gla.shape, F32),
            jax.ShapeDtypeStruct(state_ret.shape, F32),
            jax.ShapeDtypeStruct((s, 1, GLA_V + RET_V), F32),
        ],
        input_output_aliases={5 + k: k for k in range(len(stacked))},
        compiler_params=pltpu.CompilerParams(
            dimension_semantics=("arbitrary",), vmem_limit_bytes=VMEM_LIMIT),
        name="mix_sample_state",
    )(cols, rows, state_gla, state_ret, gam, *stacked)
    o = o.reshape(s, GLA_V + RET_V)

    y = pl.pallas_call(
        _mix_sample_out_kernel,
        grid=(1,),
        in_specs=[whole(x), whole(o), whole(gates), _layer_spec(ggla.shape, li), _layer_spec(gret.shape, li),
                  _layer_spec(bret.shape, li), _layer_spec(wout.shape, li),
                  _layer_spec(lng.shape, layer), _layer_spec(lnb.shape, layer)],
        out_specs=whole(x),
        out_shape=jax.ShapeDtypeStruct((s, d), F32),
        compiler_params=pltpu.CompilerParams(
            dimension_semantics=("arbitrary",), vmem_limit_bytes=VMEM_LIMIT),
        name="mix_sample_out",
    )(x, o, gates, ggla, gret, bret, wout, lng, lnb)
    return y, nsg, nsr


def _prep_mix_in(w):
    a = 2 * GLA_QK + GLA_V
    lr = jnp.pad(w[:, :, a:a + GLA_RANK], ((0, 0), (0, 0), (0, PAD_RANK - GLA_RANK)))
    return jnp.concatenate([w[:, :, :a], w[:, :, a + GLA_RANK:], lr], axis=2).astype(BF16)


def kernel(x_prompt, x_sample, state_gla, state_ret, state_conv, state_ffn,
           w_mix_in, w_gla_gate_up, b_gla_gate, g_gla_norm, g_ret_norm, b_ret_norm, w_mix_out,
           w_sc_in, w_sc_dw, b_sc_dw, w_sc_out, ln1_g, ln1_b,
           w_ffn_in, w_ffn_dw, b_ffn_dw, w_ffn_out, ln2_g, ln2_b):
    bp, tp, d = x_prompt.shape
    s, ts, _ = x_sample.shape
    assert ts == 1 and d == D_MODEL
    xp = x_prompt.reshape(bp * tp, d)
    xs = x_sample.reshape(s, d)

    tables = _mix_tables(tp)
    cos_s, sin_s = _rotary_tables(PAST_LEN + jnp.arange(ts, dtype=F32))
    log_gamma = jnp.log1p(-jnp.exp2(-5.0 - jnp.arange(RET_HEADS, dtype=F32)))
    gam = jnp.broadcast_to(jnp.exp(log_gamma)[:, None, None], (RET_HEADS, RET_DK, RET_DV))

    rows = lambda v: v[:, None, :]
    mix_w = (_prep_mix_in(w_mix_in),
             jnp.pad(w_gla_gate_up, ((0, 0), (0, PAD_RANK - GLA_RANK), (0, 0))).astype(BF16),
             rows(b_gla_gate), rows(g_gla_norm), rows(g_ret_norm), rows(b_ret_norm), w_mix_out.astype(BF16))
    sc_w = (w_sc_in.astype(BF16), w_sc_dw, rows(b_sc_dw), w_sc_out.astype(BF16))
    ffn_w = (w_ffn_in.astype(BF16), w_ffn_dw, rows(b_ffn_dw), w_ffn_out.astype(BF16))
    ln1 = (rows(ln1_g), rows(ln1_b))
    ln2 = (rows(ln2_g), rows(ln2_b))

    gla_p, ret_p, conv_p, ffn_p = [], [], [], []
    mix_s = (jnp.zeros_like(state_gla), jnp.zeros_like(state_ret))
    conv_s = jnp.zeros_like(state_conv)
    ffn_s = jnp.zeros_like(state_ffn)
    for layer in range(DEPTH):
        li = layer // 2
        if layer % 2 == 0:
            xp, sg, sr = _mix_prompt(xp, bp, tp, layer, li, *mix_w, *ln1, tables)
            gla_p.append(sg)
            ret_p.append(sr)
            xs, *mix_s = _mix_sample(xs, state_gla, state_ret, layer, li, mix_s, *mix_w, *ln1,
                                     cos_s, sin_s, gam)
        else:
            xp, sc = _sc_prompt(xp, tp, layer, li, *sc_w, *ln1)
            conv_p.append(sc)
            xs, conv_s = _conv_sample(_sc_sample_kernel, "sc_sample", xs, state_conv, conv_s,
                                      layer, li, *sc_w, *ln1)
        xp, sf = _ffn_prompt(xp, tp, layer, *ffn_w, *ln2)
        ffn_p.append(sf)
        xs, ffn_s = _conv_sample(_ffn_sample_kernel, "ffn_sample", xs, state_ffn, ffn_s,
                                 layer, layer, *ffn_w, *ln2)
    gla_s, ret_s = mix_s

    return (xp.reshape(bp, tp, d), xs.reshape(s, ts, d),
            jnp.stack(gla_p), gla_s, jnp.stack(ret_p), ret_s,
            jnp.stack(conv_p), conv_s, jnp.stack(ffn_p), ffn_s)
```

```python
import functools

import jax
import jax.numpy as jnp
from jax import lax
from jax.experimental import pallas as pl
from jax.experimental.pallas import tpu as pltpu

F32 = jnp.float32
BF16 = jnp.bfloat16

D_MODEL = 1024
DEPTH = 4
PAST_LEN = 16384
GLA_HEADS = 4
GLA_DK = 64
GLA_DV = 128
GLA_RANK = 16
GLA_TAU = 16.0
RET_HEADS = 4
RET_DK = 128
RET_DV = 128
CONV_W = 3
D_FF = D_MODEL * 11 // 4
ROPE_BASE = 10000.0
NORM_EPS = 1e-5
ALPHA = (2 * DEPTH) ** 0.25
GLA_QK = GLA_HEADS * GLA_DK
GLA_V = GLA_HEADS * GLA_DV
RET_QK = RET_HEADS * RET_DK
RET_V = RET_HEADS * RET_DV

LANES = 128
SUBLANES = 8
VMEM_LIMIT = 56 * 1024 * 1024

GLA_CHUNK = 64
GLA_SUB = 32
MIX_TILE = 256
ROW_TILE = 512
CONV_ROWS = 32
FF_CHUNK = 768
SC_CHUNK = 256
PAD_RANK = LANES

_OFF_QG = 0
_OFF_KG = _OFF_QG + GLA_QK
_OFF_VG = _OFF_KG + GLA_QK
_OFF_RG = _OFF_VG + GLA_V
_OFF_QR = _OFF_RG + GLA_V
_OFF_KR = _OFF_QR + RET_QK
_OFF_VR = _OFF_KR + RET_QK
_OFF_GR = _OFF_VR + RET_V
_OFF_LR = _OFF_GR + RET_V
MIX_COLS = _OFF_LR + PAD_RANK


def _dot(a, b):
    return jnp.dot(a, b, preferred_element_type=F32)


def _dot_nt(a, b):
    return lax.dot_general(a, b, (((1,), (1,)), ((), ())), preferred_element_type=F32)


def _layer_norm(x, g, b):
    mu = jnp.mean(x, axis=-1, keepdims=True)
    xc = x - mu
    var = jnp.mean(xc * xc, axis=-1, keepdims=True)
    return xc * lax.rsqrt(var + NORM_EPS) * g + b


def _silu(x):
    return x * jax.nn.sigmoid(x)


def _log_sigmoid(x):
    return jnp.minimum(x, 0.0) - jnp.log1p(jnp.exp(-jnp.abs(x)))


def _split3(x):
    hi = x.astype(BF16)
    r1 = x - hi.astype(F32)
    mid = r1.astype(BF16)
    lo = (r1 - mid.astype(F32)).astype(BF16)
    return hi, mid, lo


def _causal_conv3_blocks(rows, n_rows, prev, w, b, post):
    w0, w1, w2 = w[0:1], w[1:2], w[2:3]
    out = []
    for lo in range(0, n_rows, CONV_ROWS):
        head = prev if lo == 0 else rows(lo - SUBLANES, lo)
        s = jnp.concatenate([head, rows(lo, lo + CONV_ROWS)], axis=0)
        s1 = pltpu.roll(s, 1, 0)[SUBLANES:]
        s2 = pltpu.roll(s, 2, 0)[SUBLANES:]
        out.append(post(lo, b + w0 * s2 + w1 * s1 + w2 * s[SUBLANES:]))
    return jnp.concatenate(out, axis=0)


def _ffn_prompt_kernel(x_ref, win_ref, wdw_ref, bdw_ref, wout_ref, g_ref, b_ref,
                       y_ref, st_ref, xb_ref, carry_ref, *, tiles_per_seq):
    i = pl.program_id(0)

    @pl.when(i % tiles_per_seq == 0)
    def _():
        carry_ref[...] = jnp.zeros_like(carry_ref)

    tm = x_ref.shape[0]
    ff = wout_ref.shape[0]
    bounds = list(range(0, ff, FF_CHUNK)) + [ff]
    n_chunks = len(bounds) - 1
    xb_ref[...] = x_ref[...].astype(BF16)

    def cols(ref, f):
        lo, hi = bounds[f], bounds[f + 1]
        return jnp.concatenate([ref[:, lo:hi], ref[:, ff + lo:ff + hi]], axis=1)

    def up(f):
        lo, hi = bounds[f], bounds[f + 1]
        xb = xb_ref[...]
        return jnp.concatenate([_dot(xb, win_ref[:, lo:hi]), _dot(xb, win_ref[:, ff + lo:ff + hi])], axis=1)

    def act(f, u):
        lo, hi = bounds[f], bounds[f + 1]
        fc = hi - lo
        h = _causal_conv3_blocks(
            lambda r0, r1: u[r0:r1], tm, cols(carry_ref, f), cols(wdw_ref, f), cols(bdw_ref, f),
            lambda r0, c: (_silu(c[:, fc:]) * c[:, :fc]).astype(BF16))
        for k in range(2):
            tail = u[tm - SUBLANES:, k * fc:(k + 1) * fc]
            carry_ref[:, k * ff + lo:k * ff + hi] = tail
            st_ref[0, :, k * ff + lo:k * ff + hi] = tail
        return h

    def down(f, h):
        return _dot(h, wout_ref[bounds[f]:bounds[f + 1], :])

    u = up(0)
    h_prev = None
    acc = None
    for f in range(n_chunks):
        u_next = up(f + 1) if f + 1 < n_chunks else None
        h = act(f, u)
        if h_prev is not None:
            p = down(f - 1, h_prev)
            acc = p if acc is None else acc + p
        u, h_prev = u_next, h
    half = tm // 2
    w_last = wout_ref[bounds[n_chunks - 1]:bounds[n_chunks], :]
    for lo_r in (0, half):
        rs = slice(lo_r, lo_r + half)
        tot = acc[rs] + _dot(h_prev[rs], w_last)
        y_ref[rs, :] = _layer_norm(ALPHA * x_ref[rs, :] + tot, g_ref[...], b_ref[...])


def _layer_spec(shape, layer):
    zeros = (0,) * (len(shape) - 1)
    return pl.BlockSpec((None,) + tuple(shape[1:]), lambda *_: (layer,) + zeros)


def _ffn_prompt(x, seq_len, layer, win, wdw, bdw, wout, g, b):
    n, d = x.shape
    tm = min(ROW_TILE, seq_len)
    assert seq_len % tm == 0 and n % seq_len == 0
    n_seq = n // seq_len
    tps = seq_len // tm
    ff2 = win.shape[2]
    buffered = lambda spec: pl.BlockSpec(spec.block_shape, spec.index_map, pipeline_mode=pl.Buffered(1))
    y, st = pl.pallas_call(
        functools.partial(_ffn_prompt_kernel, tiles_per_seq=tps),
        grid=(n // tm,),
        in_specs=[
            pl.BlockSpec((tm, d), lambda i: (i, 0)),
            buffered(_layer_spec(win.shape, layer)),
            _layer_spec(wdw.shape, layer),
            _layer_spec(bdw.shape, layer),
            buffered(_layer_spec(wout.shape, layer)),
            _layer_spec(g.shape, layer),
            _layer_spec(b.shape, layer),
        ],
        out_specs=[
            pl.BlockSpec((tm, d), lambda i: (i, 0)),
            pl.BlockSpec((1, SUBLANES, ff2), lambda i: (i // tps, 0, 0)),
        ],
        out_shape=[
            jax.ShapeDtypeStruct((n, d), F32),
            jax.ShapeDtypeStruct((n_seq, SUBLANES, ff2), F32),
        ],
        scratch_shapes=[
            pltpu.VMEM((tm, d), BF16),
            pltpu.VMEM((SUBLANES, ff2), F32),
        ],
        compiler_params=pltpu.CompilerParams(
            dimension_semantics=("arbitrary",), vmem_limit_bytes=VMEM_LIMIT),
        name="ffn_prompt",
    )(x, win, wdw, bdw, wout, g, b)
    return y, st[:, SUBLANES - 2:, :]


def _sc_prompt_kernel(x_ref, win_ref, wdw_ref, bdw_ref, wout_ref, g_ref, b_ref,
                      y_ref, st_ref, xb_ref, carry_ref, *, tiles_per_seq):
    i = pl.program_id(0)

    @pl.when(i % tiles_per_seq == 0)
    def _():
        carry_ref[...] = jnp.zeros_like(carry_ref)

    tm, d = x_ref.shape
    cw = SC_CHUNK
    n_chunks = d // cw
    xb_ref[...] = x_ref[...].astype(BF16)

    def up(f):
        xb = xb_ref[...]
        return jnp.concatenate([_dot(xb, win_ref[:, k * d + f * cw:k * d + (f + 1) * cw]) for k in range(3)],
                               axis=1)

    def act(f, z):
        cs = slice(f * cw, (f + 1) * cw)
        ch_rows = lambda lo, hi: z[lo:hi, cw:2 * cw] * z[lo:hi, 2 * cw:]
        m = _causal_conv3_blocks(
            ch_rows, tm, carry_ref[:, cs], wdw_ref[:, cs], bdw_ref[:, cs],
            lambda lo, conv: (z[lo:lo + CONV_ROWS, :cw] * conv).astype(BF16))
        tail = ch_rows(tm - SUBLANES, tm)
        carry_ref[:, cs] = tail
        st_ref[0, :, cs] = tail
        return m

    def down(f, m):
        return _dot(m, wout_ref[f * cw:(f + 1) * cw, :])

    z = up(0)
    m_prev = None
    acc = None
    for f in range(n_chunks):
        z_next = up(f + 1) if f + 1 < n_chunks else None
        m = act(f, z)
        if m_prev is not None:
            p = down(f - 1, m_prev)
            acc = p if acc is None else acc + p
        z, m_prev = z_next, m
    half = tm // 2
    w_last = wout_ref[(n_chunks - 1) * cw:n_chunks * cw, :]
    for lo_r in (0, half):
        rs = slice(lo_r, lo_r + half)
        tot = acc[rs] + _dot(m_prev[rs], w_last)
        y_ref[rs, :] = _layer_norm(ALPHA * x_ref[rs, :] + tot, g_ref[...], b_ref[...])


def _sc_prompt(x, seq_len, layer, li, win, wdw, bdw, wout, g, b):
    n, d = x.shape
    tm = min(ROW_TILE, seq_len)
    assert seq_len % tm == 0 and n % seq_len == 0
    n_seq = n // seq_len
    tps = seq_len // tm
    buffered = lambda spec: pl.BlockSpec(spec.block_shape, spec.index_map, pipeline_mode=pl.Buffered(1))
    y, st = pl.pallas_call(
        functools.partial(_sc_prompt_kernel, tiles_per_seq=tps),
        grid=(n // tm,),
        in_specs=[
            pl.BlockSpec((tm, d), lambda i: (i, 0)),
            buffered(_layer_spec(win.shape, li)),
            _layer_spec(wdw.shape, li),
            _layer_spec(bdw.shape, li),
            buffered(_layer_spec(wout.shape, li)),
            _layer_spec(g.shape, layer),
            _layer_spec(b.shape, layer),
        ],
        out_specs=[
            pl.BlockSpec((tm, d), lambda i: (i, 0)),
            pl.BlockSpec((1, SUBLANES, d), lambda i: (i // tps, 0, 0)),
        ],
        out_shape=[
            jax.ShapeDtypeStruct((n, d), F32),
            jax.ShapeDtypeStruct((n_seq, SUBLANES, d), F32),
        ],
        scratch_shapes=[
            pltpu.VMEM((tm, d), BF16),
            pltpu.VMEM((SUBLANES, d), F32),
        ],
        compiler_params=pltpu.CompilerParams(
            dimension_semantics=("arbitrary",), vmem_limit_bytes=VMEM_LIMIT),
        name="sc_prompt",
    )(x, win, wdw, bdw, wout, g, b)
    return y, st[:, SUBLANES - 2:, :]


def _rotary(x, cosf, sinf):
    return x * cosf + pltpu.roll(x, RET_DK // 2, 1) * sinf


def _gla_tile(qg, kg, vg, wn_all, r_all, st, mask_q, masks_p, mask_bd, sblock):
    c = GLA_CHUNK
    n_chunks = qg.shape[0] // c
    n_sub = c // GLA_SUB
    pre = []
    for ci in range(n_chunks):
        sl = slice(ci * c, (ci + 1) * c)
        q_c, k_c, v_c, wn, r = qg[sl], kg[sl], vg[sl], wn_all[sl], r_all[sl]
        cum = r + wn
        tot = cum[c - 1:c, :]
        vst = jnp.concatenate([v_c[:, h * GLA_DV:(h + 1) * GLA_DV] for h in range(GLA_HEADS)], axis=0)
        pre.append(dict(q=q_c, k=k_c, r=r, cum=cum, tot=tot, qw=q_c * jnp.exp(wn), vst=vst,
                        vst_b=vst.astype(BF16)))
    scores = []
    for p in pre:
        row = []
        for i in range(n_sub):
            r_i = p["r"][i * GLA_SUB:i * GLA_SUB + 1, :]
            kt = jnp.where(sblock <= i, p["k"] * jnp.exp(r_i - p["cum"]), 0.0)
            kbig = jnp.concatenate([kt] * GLA_HEADS, axis=0).astype(BF16)
            q16 = p["qw"][i * GLA_SUB:(i + 1) * GLA_SUB]
            qs = jnp.where(mask_q, jnp.concatenate([q16] * GLA_HEADS, axis=0), 0.0).astype(BF16)
            row.append(_dot_nt(qs, kbig))
        scores.append(row)
    intra = [[_dot(jnp.where(masks_p[i], scores[ci][i], 0.0).astype(BF16), p["vst_b"])
              for i in range(n_sub)] for ci, p in enumerate(pre)]
    upds, qcss = [], []
    for p in pre:
        kd = p["k"] * jnp.exp(p["tot"] - p["cum"])
        kdbig = jnp.where(mask_bd, jnp.concatenate([kd] * GLA_HEADS, axis=0), 0.0).astype(BF16)
        upds.append(_dot(p["vst"].T.astype(BF16), kdbig))
        qc = p["q"] * jnp.exp(p["cum"])
        qcss.append(jnp.where(mask_bd, jnp.concatenate([qc] * GLA_HEADS, axis=0), 0.0).astype(BF16))
    inters = []
    for ci, p in enumerate(pre):
        inters.append(_dot_nt(qcss[ci], st.astype(BF16)))
        st = jnp.exp(p["tot"]) * st + upds[ci]
    chunks = []
    for ci in range(n_chunks):
        heads = []
        for h in range(GLA_HEADS):
            intra_h = jnp.concatenate([intra[ci][i][h * GLA_SUB:(h + 1) * GLA_SUB] for i in range(n_sub)], axis=0)
            heads.append(intra_h + inters[ci][h * c:(h + 1) * c])
        chunks.append(jnp.concatenate(heads, axis=1))
    return jnp.concatenate(chunks, axis=0), st


def _mix_prompt_kernel(x_ref, win_ref, wup_ref, bgate_ref, ggla_ref, gret_ref, bret_ref, wout_ref,
                       lng_ref, lnb_ref, cos_ref, sin_ref, mcat_ref, dmask_ref, qdec_ref, kdec_ref,
                       y_ref, sg_ref, sr_ref, stg_ref, str_ref):
    j = pl.program_id(1)
    nj = pl.num_programs(1)

    @pl.when(j == 0)
    def _():
        stg_ref[...] = jnp.zeros_like(stg_ref)
        str_ref[...] = jnp.zeros_like(str_ref)

    tc = x_ref.shape[0]
    x = x_ref[...]
    xb = x.astype(BF16)

    def proj(off, width):
        return _dot(xb, win_ref[:, off:off + width])

    lr = proj(_OFF_LR, PAD_RANK)
    gate = _dot(lr.astype(BF16), wup_ref[...]) + bgate_ref[...]
    lg = _log_sigmoid(gate) * (1.0 / GLA_TAU)
    hi, mid, lo = _split3(lg)
    qr_all = proj(_OFF_QR, RET_QK)
    kr_all = proj(_OFF_KR, RET_QK)
    vr_all = proj(_OFF_VR, RET_V)
    qg = proj(_OFF_QG, GLA_QK) * (GLA_DK ** -0.5)
    kg = proj(_OFF_KG, GLA_QK)
    vg = proj(_OFF_VG, GLA_V)
    cs = _dot(mcat_ref[...], jnp.concatenate([hi, mid, lo], axis=1))
    cs = cs[:, :GLA_QK] + cs[:, GLA_QK:2 * GLA_QK] + cs[:, 2 * GLA_QK:]
    wn_all = cs[:tc]
    r_all = cs[tc:]

    cosf = cos_ref[...]
    sinf = sin_ref[...]
    hs = range(RET_HEADS)
    hsl = [slice(h * RET_DK, (h + 1) * RET_DK) for h in hs]
    qhs = [_rotary(qr_all[:, hsl[h]], cosf, sinf) for h in hs]
    khs = [_rotary(kr_all[:, hsl[h]], cosf, sinf) * (RET_DK ** -0.5) for h in hs]
    vhs = [vr_all[:, hsl[h]].astype(BF16) for h in hs]
    scs = [_dot_nt(qhs[h].astype(BF16), khs[h].astype(BF16)) for h in hs]
    s_hs = [str_ref[h] for h in hs]
    inter_r = [_dot((qhs[h] * qdec_ref[h]).astype(BF16), s_hs[h].astype(BF16)) for h in hs]
    upd_r = [_dot((khs[h] * kdec_ref[h]).T.astype(BF16), vhs[h]) for h in hs]
    o_r = [_dot((scs[h] * dmask_ref[h]).astype(BF16), vhs[h]) + inter_r[h] for h in hs]
    for h in hs:
        str_ref[h] = qdec_ref[h][tc - 1:tc, :] * s_hs[h] + upd_r[h]
    gr_all = proj(_OFF_GR, RET_V)
    rg = proj(_OFF_RG, GLA_V)


    c = GLA_CHUNK
    hq = GLA_HEADS * GLA_SUB
    row64 = lax.broadcasted_iota(jnp.int32, (hq, GLA_QK), 0)
    lane64 = lax.broadcasted_iota(jnp.int32, (hq, GLA_QK), 1)
    lane_head = lax.shift_right_logical(lane64, 6)
    row_head = lax.shift_right_logical(row64, GLA_SUB.bit_length() - 1)
    mask_q = lane_head == row_head
    masks_p = [mask_q & ((lane64 & (c - 1)) <= (row64 & (GLA_SUB - 1)) + i * GLA_SUB)
               for i in range(c // GLA_SUB)]
    rowb = lax.broadcasted_iota(jnp.int32, (GLA_HEADS * c, GLA_QK), 0)
    laneb = lax.broadcasted_iota(jnp.int32, (GLA_HEADS * c, GLA_QK), 1)
    mask_bd = lax.shift_right_logical(rowb, 6) == lax.shift_right_logical(laneb, 6)
    sblock = lax.shift_right_logical(lax.broadcasted_iota(jnp.int32, (c, GLA_QK), 0), GLA_SUB.bit_length() - 1)

    og, st = _gla_tile(qg, kg, vg, wn_all, r_all, stg_ref[...], mask_q, masks_p, mask_bd, sblock)
    stg_ref[...] = st
    ggla = ggla_ref[...]
    og_heads = []
    for h in range(GLA_HEADS):
        sl = slice(h * GLA_DV, (h + 1) * GLA_DV)
        o = og[:, sl]
        o = o * lax.rsqrt(jnp.mean(o * o, axis=-1, keepdims=True) + NORM_EPS) * ggla
        og_heads.append(o * _silu(rg[:, sl]))
    or_heads = []
    for h in hs:
        o = o_r[h]
        mu = jnp.mean(o, axis=-1, keepdims=True)
        oc = o - mu
        var = jnp.mean(oc * oc, axis=-1, keepdims=True)
        o = oc * lax.rsqrt(var + NORM_EPS) * gret_ref[:, hsl[h]] + bret_ref[:, hsl[h]]
        or_heads.append(o * _silu(gr_all[:, hsl[h]]))

    mixed = jnp.concatenate(og_heads + or_heads, axis=1).astype(BF16)
    half = tc // 2
    for lo_r in (0, half):
        rs = slice(lo_r, lo_r + half)
        y = _dot(mixed[rs], wout_ref[...])
        y_ref[rs, :] = _layer_norm(ALPHA * x[rs] + y, lng_ref[...], lnb_ref[...])

    @pl.when(j == nj - 1)
    def _():
        sg_ref[0] = stg_ref[...].T
        sr_ref[0] = str_ref[...]


def _mix_prompt(x, n_seq, seq_len, layer, li, win, wup, bgate, ggla, gret, bret, wout, lng, lnb, tables):
    n, d = x.shape
    tc = min(MIX_TILE, seq_len)
    assert seq_len % tc == 0 and tc % GLA_CHUNK == 0
    nt = seq_len // tc
    cosf, sinf, mcat, dmask, qdec, kdec = tables
    c2 = lambda b, j: (0, 0)
    c3 = lambda b, j: (0, 0, 0)
    buffered = lambda spec: pl.BlockSpec(spec.block_shape, spec.index_map, pipeline_mode=pl.Buffered(1))
    y, sg, sr = pl.pallas_call(
        _mix_prompt_kernel,
        grid=(n_seq, nt),
        in_specs=[
            pl.BlockSpec((tc, d), lambda b, j: (b * nt + j, 0)),
            buffered(_layer_spec(win.shape, li)),
            _layer_spec(wup.shape, li),
            _layer_spec(bgate.shape, li),
            _layer_spec(ggla.shape, li),
            _layer_spec(gret.shape, li),
            _layer_spec(bret.shape, li),
            buffered(_layer_spec(wout.shape, li)),
            _layer_spec(lng.shape, layer),
            _layer_spec(lnb.shape, layer),
            pl.BlockSpec((tc, RET_DK), lambda b, j: (j, 0)),
            pl.BlockSpec((tc, RET_DK), lambda b, j: (j, 0)),
            pl.BlockSpec((2 * tc, tc), c2),
            pl.BlockSpec((RET_HEADS, tc, tc), c3),
            pl.BlockSpec((RET_HEADS, tc, RET_DK), c3),
            pl.BlockSpec((RET_HEADS, tc, RET_DK), c3),
        ],
        out_specs=[
            pl.BlockSpec((tc, d), lambda b, j: (b * nt + j, 0)),
            pl.BlockSpec((1, GLA_QK, GLA_DV), lambda b, j: (b, 0, 0)),
            pl.BlockSpec((1, RET_HEADS, RET_DK, RET_DV), lambda b, j: (b, 0, 0, 0)),
        ],
        out_shape=[
            jax.ShapeDtypeStruct((n, d), F32),
            jax.ShapeDtypeStruct((n_seq, GLA_QK, GLA_DV), F32),
            jax.ShapeDtypeStruct((n_seq, RET_HEADS, RET_DK, RET_DV), F32),
        ],
        scratch_shapes=[
            pltpu.VMEM((GLA_DV, GLA_QK), F32),
            pltpu.VMEM((RET_HEADS, RET_DK, RET_DV), F32),
        ],
        compiler_params=pltpu.CompilerParams(
            dimension_semantics=("arbitrary", "arbitrary"), vmem_limit_bytes=VMEM_LIMIT),
        name="mix_prompt",
    )(x, win, wup, bgate, ggla, gret, bret, wout, lng, lnb, cosf, sinf, mcat, dmask, qdec, kdec)
    return y, sg.reshape(n_seq, GLA_HEADS, GLA_DK, GLA_DV), sr


def _mix_tables(seq_len):
    tc = min(MIX_TILE, seq_len)
    pos = jnp.arange(seq_len, dtype=F32)
    cosf, sinf = _rotary_tables(pos)
    t = jnp.arange(tc)
    same_sub = (t[:, None] // GLA_SUB) == (t[None, :] // GLA_SUB)
    same_chunk = (t[:, None] // GLA_CHUNK) == (t[None, :] // GLA_CHUNK)
    m_within = same_sub & (t[None, :] <= t[:, None])
    m_before = same_chunk & ((t[None, :] // GLA_SUB) < (t[:, None] // GLA_SUB))
    mcat = jnp.concatenate([m_within, m_before], axis=0).astype(BF16)
    log_gamma = jnp.log1p(-jnp.exp2(-5.0 - jnp.arange(RET_HEADS, dtype=F32)))
    tf = t.astype(F32)
    diff = tf[:, None] - tf[None, :]
    dmask = jnp.where(diff >= 0, jnp.exp(diff[None] * log_gamma[:, None, None]), 0.0)
    qdec = jnp.exp((tf[None, :, None] + 1.0) * log_gamma[:, None, None])
    kdec = jnp.exp((tc - 1.0 - tf[None, :, None]) * log_gamma[:, None, None])
    qdec = jnp.broadcast_to(qdec, (RET_HEADS, tc, RET_DK))
    kdec = jnp.broadcast_to(kdec, (RET_HEADS, tc, RET_DK))
    return cosf, sinf, mcat, dmask, qdec, kdec


def _rotary_tables(pos):
    inv_freq = 1.0 / (ROPE_BASE ** (jnp.arange(0, RET_DK, 2, dtype=F32) / RET_DK))
    ang = pos[:, None] * inv_freq[None, :]
    cos, sin = jnp.cos(ang), jnp.sin(ang)
    return jnp.concatenate([cos, cos], axis=1), jnp.concatenate([-sin, sin], axis=1)


def _ffn_sample_kernel(x_ref, st_ref, win_ref, wdw_ref, bdw_ref, wout_ref, g_ref, b_ref,
                       y_ref, nst_ref):
    x = x_ref[...]
    u = _dot(x.astype(BF16), win_ref[...])
    wdw = wdw_ref[...]
    b1 = st_ref[:, 1, :]
    c = bdw_ref[...] + wdw[0:1] * st_ref[:, 0, :] + wdw[1:2] * b1 + wdw[2:3] * u
    h = _silu(c[:, D_FF:]) * c[:, :D_FF]
    y = _dot(h.astype(BF16), wout_ref[...])
    y_ref[...] = _layer_norm(ALPHA * x + y, g_ref[...], b_ref[...])
    nst_ref[:, 0, :] = b1
    nst_ref[:, 1, :] = u


def _sc_sample_kernel(x_ref, st_ref, win_ref, wdw_ref, bdw_ref, wout_ref, g_ref, b_ref,
                      y_ref, nst_ref):
    x = x_ref[...]
    d = x.shape[1]
    z = _dot(x.astype(BF16), win_ref[...])
    ch = z[:, d:2 * d] * z[:, 2 * d:]
    wdw = wdw_ref[...]
    b1 = st_ref[:, 1, :]
    conv = bdw_ref[...] + wdw[0:1] * st_ref[:, 0, :] + wdw[1:2] * b1 + wdw[2:3] * ch
    y = _dot((z[:, :d] * conv).astype(BF16), wout_ref[...])
    y_ref[...] = _layer_norm(ALPHA * x + y, g_ref[...], b_ref[...])
    nst_ref[:, 0, :] = b1
    nst_ref[:, 1, :] = ch


def _conv_sample_kernel(body, *refs):
    body(*refs[:8], *refs[9:])


def _conv_sample(body, name, x, states, new_states, layer, li, win, wdw, bdw, wout, g, b):
    s, d = x.shape
    whole = lambda a: pl.BlockSpec(a.shape, lambda i: (0,) * a.ndim)
    st_spec = _layer_spec(states.shape, li)
    return pl.pallas_call(
        functools.partial(_conv_sample_kernel, body),
        grid=(1,),
        in_specs=[whole(x), st_spec, _layer_spec(win.shape, li), _layer_spec(wdw.shape, li),
                  _layer_spec(bdw.shape, li), _layer_spec(wout.shape, li),
                  _layer_spec(g.shape, layer), _layer_spec(b.shape, layer),
                  pl.BlockSpec(memory_space=pl.ANY)],
        out_specs=[whole(x), st_spec],
        out_shape=[jax.ShapeDtypeStruct((s, d), F32), jax.ShapeDtypeStruct(states.shape, F32)],
        input_output_aliases={8: 1},
        compiler_params=pltpu.CompilerParams(
            dimension_semantics=("arbitrary",), vmem_limit_bytes=VMEM_LIMIT),
        name=name,
    )(x, states, win, wdw, bdw, wout, g, b, new_states)


def _mix_sample_proj_kernel(x_ref, win_ref, wup_ref, bgate_ref, cos_ref, sin_ref,
                            cols_ref, rows_ref, gates_ref):
    xb = x_ref[...].astype(BF16)

    def proj(off, width):
        return _dot(xb, win_ref[:, off:off + width])

    qg = proj(_OFF_QG, GLA_QK) * (GLA_DK ** -0.5)
    kg = proj(_OFF_KG, GLA_QK)
    lr = proj(_OFF_LR, PAD_RANK)
    gate = _dot(lr.astype(BF16), wup_ref[...]) + bgate_ref[...]
    ag = jnp.exp(_log_sigmoid(gate) * (1.0 / GLA_TAU))
    cosf = cos_ref[...]
    sinf = sin_ref[...]
    qr = [_rotary(proj(_OFF_QR + h * RET_DK, RET_DK), cosf, sinf) for h in range(RET_HEADS)]
    kr = [_rotary(proj(_OFF_KR + h * RET_DK, RET_DK), cosf, sinf) * (RET_DK ** -0.5)
          for h in range(RET_HEADS)]
    cols_ref[0:GLA_QK, :] = ag.T
    cols_ref[GLA_QK:2 * GLA_QK, :] = kg.T
    cols_ref[2 * GLA_QK:3 * GLA_QK, :] = qg.T
    base = 3 * GLA_QK
    for h in range(RET_HEADS):
        cols_ref[base + h * RET_DK:base + (h + 1) * RET_DK, :] = kr[h].T
        cols_ref[base + RET_QK + h * RET_DK:base + RET_QK + (h + 1) * RET_DK, :] = qr[h].T
    rows_ref[:, :GLA_V] = proj(_OFF_VG, GLA_V)
    rows_ref[:, GLA_V:] = proj(_OFF_VR, RET_V)
    gates_ref[:, :GLA_V] = proj(_OFF_RG, GLA_V)
    gates_ref[:, GLA_V:] = proj(_OFF_GR, RET_V)


_COL_ROWS = 3 * GLA_QK + 2 * RET_QK


def _mix_sample_state_kernel(*refs, n_stacked_in):
    cols_ref, rows_ref, sg_ref, sr_ref, gam_ref = refs[:5]
    nsg_ref, nsr_ref, o_ref = refs[5 + n_stacked_in:]
    nb = sg_ref.shape[0]
    s = cols_ref.shape[1]
    blk = pl.program_id(0)
    lane = lax.broadcasted_iota(jnp.int32, (1, s), 1)

    def body(bb, carry):
        b = blk * nb + bb
        onehot = (lane == b).astype(F32)
        col = jnp.sum(cols_ref[...] * onehot, axis=1, keepdims=True)
        vrow = rows_ref[b]
        a_g = col[0:GLA_QK]
        k_g = col[GLA_QK:2 * GLA_QK]
        q_g = col[2 * GLA_QK:3 * GLA_QK]
        base = 3 * GLA_QK
        outs = []
        for h in range(GLA_HEADS):
            rs = slice(h * GLA_DK, (h + 1) * GLA_DK)
            v_h = vrow[:, h * GLA_DV:(h + 1) * GLA_DV]
            s_new = a_g[rs] * sg_ref[bb, h] + k_g[rs] * v_h
            nsg_ref[bb, h] = s_new
            outs.append(jnp.sum(q_g[rs] * s_new, axis=0, keepdims=True))
        for h in range(RET_HEADS):
            rs = slice(h * RET_DK, (h + 1) * RET_DK)
            k_h = col[base + h * RET_DK:base + (h + 1) * RET_DK]
            q_h = col[base + RET_QK + h * RET_DK:base + RET_QK + (h + 1) * RET_DK]
            v_h = vrow[:, GLA_V + h * RET_DV:GLA_V + (h + 1) * RET_DV]
            s_new = gam_ref[h] * sr_ref[bb, h] + k_h * v_h
            nsr_ref[bb, h] = s_new
            outs.append(jnp.sum(q_h * s_new, axis=0, keepdims=True))
        o_ref[b] = jnp.concatenate(outs, axis=1)
        return carry

    lax.fori_loop(0, nb, body, 0)


def _mix_sample_out_kernel(x_ref, o_ref, gates_ref, ggla_ref, gret_ref, bret_ref, wout_ref,
                           lng_ref, lnb_ref, y_ref):
    x = x_ref[...]
    o = o_ref[...]
    gates = gates_ref[...]
    ggla = ggla_ref[...]
    heads = []
    for h in range(GLA_HEADS):
        sl = slice(h * GLA_DV, (h + 1) * GLA_DV)
        oh = o[:, sl]
        oh = oh * lax.rsqrt(jnp.mean(oh * oh, axis=-1, keepdims=True) + NORM_EPS) * ggla
        heads.append(oh * _silu(gates[:, sl]))
    for h in range(RET_HEADS):
        sl = slice(GLA_V + h * RET_DV, GLA_V + (h + 1) * RET_DV)
        ps = slice(h * RET_DV, (h + 1) * RET_DV)
        oh = o[:, sl]
        mu = jnp.mean(oh, axis=-1, keepdims=True)
        oc = oh - mu
        var = jnp.mean(oc * oc, axis=-1, keepdims=True)
        oh = oc * lax.rsqrt(var + NORM_EPS) * gret_ref[:, ps] + bret_ref[:, ps]
        heads.append(oh * _silu(gates[:, sl]))
    mixed = jnp.concatenate(heads, axis=1).astype(BF16)
    y = _dot(mixed, wout_ref[...])
    y_ref[...] = _layer_norm(ALPHA * x + y, lng_ref[...], lnb_ref[...])


def _mix_sample(x, state_gla, state_ret, layer, li, stacked, win, wup, bgate, ggla, gret, bret, wout, lng, lnb,
                cos_s, sin_s, gam):
    s, d = x.shape
    whole = lambda a: pl.BlockSpec(a.shape, lambda i: (0,) * a.ndim)
    proj_out = [
        jax.ShapeDtypeStruct((_COL_ROWS, s), F32),
        jax.ShapeDtypeStruct((s, GLA_V + RET_V), F32),
        jax.ShapeDtypeStruct((s, GLA_V + RET_V), F32),
    ]
    cols, rows, gates = pl.pallas_call(
        _mix_sample_proj_kernel,
        grid=(1,),
        in_specs=[whole(x), _layer_spec(win.shape, li), _layer_spec(wup.shape, li),
                  _layer_spec(bgate.shape, li), whole(cos_s), whole(sin_s)],
        out_specs=[whole(o) for o in proj_out],
        out_shape=proj_out,
        compiler_params=pltpu.CompilerParams(
            dimension_semantics=("arbitrary",), vmem_limit_bytes=VMEM_LIMIT),
        name="mix_sample_proj",
    )(x, win, wup, bgate, cos_s, sin_s)
    rows = rows.reshape(s, 1, GLA_V + RET_V)

    nb = SUBLANES
    assert s % nb == 0
    gla_spec = pl.BlockSpec((None, nb, GLA_HEADS, GLA_DK, GLA_DV), lambda i: (li, i, 0, 0, 0))
    ret_spec = pl.BlockSpec((None, nb, RET_HEADS, RET_DK, RET_DV), lambda i: (li, i, 0, 0, 0))
    stacked = tuple(stacked)
    nsg, nsr, o = pl.pallas_call(
        functools.partial(_mix_sample_state_kernel, n_stacked_in=len(stacked)),
        grid=(s // nb,),
        in_specs=[
            pl.BlockSpec((_COL_ROWS, s), lambda i: (0, 0)),
            pl.BlockSpec((s, 1, GLA_V + RET_V), lambda i: (0, 0, 0)),
            gla_spec,
            ret_spec,
            pl.BlockSpec((RET_HEADS, RET_DK, RET_DV), lambda i: (0, 0, 0)),
        ] + [pl.BlockSpec(memory_space=pl.ANY)] * len(stacked),
        out_specs=[
            gla_spec,
            ret_spec,
            pl.BlockSpec((s, 1, GLA_V + RET_V), lambda i: (0, 0, 0)),
        ],
        out_shape=[
            jax.ShapeDtypeStruct(state_gla.shape, F32),
            jax.ShapeDtypeStruct(state_ret.shape, F32),
            jax.ShapeDtypeStruct((s, 1, GLA_V + RET_V), F32),
        ],
        input_output_aliases={5 + k: k for k in range(len(stacked))},
        compiler_params=pltpu.CompilerParams(
            dimension_semantics=("arbitrary",), vmem_limit_bytes=VMEM_LIMIT),
        name="mix_sample_state",
    )(cols, rows, state_gla, state_ret, gam, *stacked)
    o = o.reshape(s, GLA_V + RET_V)

    y = pl.pallas_call(
        _mix_sample_out_kernel,
        grid=(1,),
        in_specs=[whole(x), whole(o), whole(gates), _layer_spec(ggla.shape, li), _layer_spec(gret.shape, li),
                  _layer_spec(bret.shape, li), _layer_spec(wout.shape, li),
                  _layer_spec(lng.shape, layer), _layer_spec(lnb.shape, layer)],
        out_specs=whole(x),
        out_shape=jax.ShapeDtypeStruct((s, d), F32),
        compiler_params=pltpu.CompilerParams(
            dimension_semantics=("arbitrary",), vmem_limit_bytes=VMEM_LIMIT),
        name="mix_sample_out",
    )(x, o, gates, ggla, gret, bret, wout, lng, lnb)
    return y, nsg, nsr


def _prep_mix_in(w):
    a = 2 * GLA_QK + GLA_V
    lr = jnp.pad(w[:, :, a:a + GLA_RANK], ((0, 0), (0, 0), (0, PAD_RANK - GLA_RANK)))
    return jnp.concatenate([w[:, :, :a], w[:, :, a + GLA_RANK:], lr], axis=2).astype(BF16)


def kernel(x_prompt, x_sample, state_gla, state_ret, state_conv, state_ffn,
           w_mix_in, w_gla_gate_up, b_gla_gate, g_gla_norm, g_ret_norm, b_ret_norm, w_mix_out,
           w_sc_in, w_sc_dw, b_sc_dw, w_sc_out, ln1_g, ln1_b,
           w_ffn_in, w_ffn_dw, b_ffn_dw, w_ffn_out, ln2_g, ln2_b):
    bp, tp, d = x_prompt.shape
    s, ts, _ = x_sample.shape
    assert ts == 1 and d == D_MODEL
    xp = x_prompt.reshape(bp * tp, d)
    xs = x_sample.reshape(s, d)

    tables = _mix_tables(tp)
    cos_s, sin_s = _rotary_tables(PAST_LEN + jnp.arange(ts, dtype=F32))
    log_gamma = jnp.log1p(-jnp.exp2(-5.0 - jnp.arange(RET_HEADS, dtype=F32)))
    gam = jnp.broadcast_to(jnp.exp(log_gamma)[:, None, None], (RET_HEADS, RET_DK, RET_DV))

    rows = lambda v: v[:, None, :]
    mix_w = (_prep_mix_in(w_mix_in),
             jnp.pad(w_gla_gate_up, ((0, 0), (0, PAD_RANK - GLA_RANK), (0, 0))).astype(BF16),
             rows(b_gla_gate), rows(g_gla_norm), rows(g_ret_norm), rows(b_ret_norm), w_mix_out.astype(BF16))
    sc_w = (w_sc_in.astype(BF16), w_sc_dw, rows(b_sc_dw), w_sc_out.astype(BF16))
    ffn_w = (w_ffn_in.astype(BF16), w_ffn_dw, rows(b_ffn_dw), w_ffn_out.astype(BF16))
    ln1 = (rows(ln1_g), rows(ln1_b))
    ln2 = (rows(ln2_g), rows(ln2_b))

    gla_p, ret_p, conv_p, ffn_p = [], [], [], []
    mix_s = (jnp.zeros_like(state_gla), jnp.zeros_like(state_ret))
    conv_s = jnp.zeros_like(state_conv)
    ffn_s = jnp.zeros_like(state_ffn)
    for layer in range(DEPTH):
        li = layer // 2
        if layer % 2 == 0:
            xp, sg, sr = _mix_prompt(xp, bp, tp, layer, li, *mix_w, *ln1, tables)
            gla_p.append(sg)
            ret_p.append(sr)
            xs, *mix_s = _mix_sample(xs, state_gla, state_ret, layer, li, mix_s, *mix_w, *ln1,
                                     cos_s, sin_s, gam)
        else:
            xp, sc = _sc_prompt(xp, tp, layer, li, *sc_w, *ln1)
            conv_p.append(sc)
            xs, conv_s = _conv_sample(_sc_sample_kernel, "sc_sample", xs, state_conv, conv_s,
                                      layer, li, *sc_w, *ln1)
        xp, sf = _ffn_prompt(xp, tp, layer, *ffn_w, *ln2)
        ffn_p.append(sf)
        xs, ffn_s = _conv_sample(_ffn_sample_kernel, "ffn_sample", xs, state_ffn, ffn_s,
                                 layer, layer, *ffn_w, *ln2)
    gla_s, ret_s = mix_s

    return (xp.reshape(bp, tp, d), xs.reshape(s, ts, d),
            jnp.stack(gla_p), gla_s, jnp.stack(ret_p), ret_s,
            jnp.stack(conv_p), conv_s, jnp.stack(ffn_p), ffn_s)
```

```python
import functools

import jax
import jax.numpy as jnp
from jax import lax
from jax.experimental import pallas as pl
from jax.experimental.pallas import tpu as pltpu

F32 = jnp.float32
BF16 = jnp.bfloat16

D_MODEL = 1024
DEPTH = 4
PAST_LEN = 16384
GLA_HEADS = 4
GLA_DK = 64
GLA_DV = 128
GLA_RANK = 16
GLA_TAU = 16.0
RET_HEADS = 4
RET_DK = 128
RET_DV = 128
CONV_W = 3
D_FF = D_MODEL * 11 // 4
ROPE_BASE = 10000.0
NORM_EPS = 1e-5
ALPHA = (2 * DEPTH) ** 0.25
GLA_QK = GLA_HEADS * GLA_DK
GLA_V = GLA_HEADS * GLA_DV
RET_QK = RET_HEADS * RET_DK
RET_V = RET_HEADS * RET_DV

LANES = 128
SUBLANES = 8
VMEM_LIMIT = 56 * 1024 * 1024

GLA_CHUNK = 64
GLA_SUB = 32
MIX_TILE = 256
ROW_TILE = 512
CONV_ROWS = 32
FF_CHUNK = 768
SC_CHUNK = 256
PAD_RANK = LANES

_OFF_QG = 0
_OFF_KG = _OFF_QG + GLA_QK
_OFF_VG = _OFF_KG + GLA_QK
_OFF_RG = _OFF_VG + GLA_V
_OFF_QR = _OFF_RG + GLA_V
_OFF_KR = _OFF_QR + RET_QK
_OFF_VR = _OFF_KR + RET_QK
_OFF_GR = _OFF_VR + RET_V
_OFF_LR = _OFF_GR + RET_V
MIX_COLS = _OFF_LR + PAD_RANK


def _dot(a, b):
    return jnp.dot(a, b, preferred_element_type=F32)


def _dot_nt(a, b):
    return lax.dot_general(a, b, (((1,), (1,)), ((), ())), preferred_element_type=F32)


def _layer_norm(x, g, b):
    mu = jnp.mean(x, axis=-1, keepdims=True)
    xc = x - mu
    var = jnp.mean(xc * xc, axis=-1, keepdims=True)
    return xc * lax.rsqrt(var + NORM_EPS) * g + b


def _silu(x):
    return x * jax.nn.sigmoid(x)


def _log_sigmoid(x):
    return jnp.minimum(x, 0.0) - jnp.log1p(jnp.exp(-jnp.abs(x)))


def _split3(x):
    hi = x.astype(BF16)
    r1 = x - hi.astype(F32)
    mid = r1.astype(BF16)
    lo = (r1 - mid.astype(F32)).astype(BF16)
    return hi, mid, lo


def _causal_conv3_blocks(rows, n_rows, prev, w, b, post):
    w0, w1, w2 = w[0:1], w[1:2], w[2:3]
    out = []
    for lo in range(0, n_rows, CONV_ROWS):
        head = prev if lo == 0 else rows(lo - SUBLANES, lo)
        s = jnp.concatenate([head, rows(lo, lo + CONV_ROWS)], axis=0)
        s1 = pltpu.roll(s, 1, 0)[SUBLANES:]
        s2 = pltpu.roll(s, 2, 0)[SUBLANES:]
        out.append(post(lo, b + w0 * s2 + w1 * s1 + w2 * s[SUBLANES:]))
    return jnp.concatenate(out, axis=0)


def _ffn_prompt_kernel(x_ref, win_ref, wdw_ref, bdw_ref, wout_ref, g_ref, b_ref,
                       y_ref, st_ref, xb_ref, carry_ref, *, tiles_per_seq):
    i = pl.program_id(0)

    @pl.when(i % tiles_per_seq == 0)
    def _():
        carry_ref[...] = jnp.zeros_like(carry_ref)

    tm = x_ref.shape[0]
    ff = wout_ref.shape[0]
    bounds = list(range(0, ff, FF_CHUNK)) + [ff]
    n_chunks = len(bounds) - 1
    xb_ref[...] = x_ref[...].astype(BF16)

    def cols(ref, f):
        lo, hi = bounds[f], bounds[f + 1]
        return jnp.concatenate([ref[:, lo:hi], ref[:, ff + lo:ff + hi]], axis=1)

    def up(f):
        lo, hi = bounds[f], bounds[f + 1]
        xb = xb_ref[...]
        return jnp.concatenate([_dot(xb, win_ref[:, lo:hi]), _dot(xb, win_ref[:, ff + lo:ff + hi])], axis=1)

    def act(f, u):
        lo, hi = bounds[f], bounds[f + 1]
        fc = hi - lo
        h = _causal_conv3_blocks(
            lambda r0, r1: u[r0:r1], tm, cols(carry_ref, f), cols(wdw_ref, f), cols(bdw_ref, f),
            lambda r0, c: (_silu(c[:, fc:]) * c[:, :fc]).astype(BF16))
        for k in range(2):
            tail = u[tm - SUBLANES:, k * fc:(k + 1) * fc]
            carry_ref[:, k * ff + lo:k * ff + hi] = tail
            st_ref[0, :, k * ff + lo:k * ff + hi] = tail
        return h

    def down(f, h):
        return _dot(h, wout_ref[bounds[f]:bounds[f + 1], :])

    u = up(0)
    h_prev = None
    acc = None
    for f in range(n_chunks):
        u_next = up(f + 1) if f + 1 < n_chunks else None
        h = act(f, u)
        if h_prev is not None:
            p = down(f - 1, h_prev)
            acc = p if acc is None else acc + p
        u, h_prev = u_next, h
    half = tm // 2
    w_last = wout_ref[bounds[n_chunks - 1]:bounds[n_chunks], :]
    for lo_r in (0, half):
        rs = slice(lo_r, lo_r + half)
        tot = acc[rs] + _dot(h_prev[rs], w_last)
        y_ref[rs, :] = _layer_norm(ALPHA * x_ref[rs, :] + tot, g_ref[...], b_ref[...])


def _layer_spec(shape, layer):
    zeros = (0,) * (len(shape) - 1)
    return pl.BlockSpec((None,) + tuple(shape[1:]), lambda *_: (layer,) + zeros)


def _ffn_prompt(x, seq_len, layer, win, wdw, bdw, wout, g, b):
    n, d = x.shape
    tm = min(ROW_TILE, seq_len)
    assert seq_len % tm == 0 and n % seq_len == 0
    n_seq = n // seq_len
    tps = seq_len // tm
    ff2 = win.shape[2]
    buffered = lambda spec: pl.BlockSpec(spec.block_shape, spec.index_map, pipeline_mode=pl.Buffered(1))
    y, st = pl.pallas_call(
        functools.partial(_ffn_prompt_kernel, tiles_per_seq=tps),
        grid=(n // tm,),
        in_specs=[
            pl.BlockSpec((tm, d), lambda i: (i, 0)),
            buffered(_layer_spec(win.shape, layer)),
            _layer_spec(wdw.shape, layer),
            _layer_spec(bdw.shape, layer),
            buffered(_layer_spec(wout.shape, layer)),
            _layer_spec(g.shape, layer),
            _layer_spec(b.shape, layer),
        ],
        out_specs=[
            pl.BlockSpec((tm, d), lambda i: (i, 0)),
            pl.BlockSpec((1, SUBLANES, ff2), lambda i: (i // tps, 0, 0)),
        ],
        out_shape=[
            jax.ShapeDtypeStruct((n, d), F32),
            jax.ShapeDtypeStruct((n_seq, SUBLANES, ff2), F32),
        ],
        scratch_shapes=[
            pltpu.VMEM((tm, d), BF16),
            pltpu.VMEM((SUBLANES, ff2), F32),
        ],
        compiler_params=pltpu.CompilerParams(
            dimension_semantics=("arbitrary",), vmem_limit_bytes=VMEM_LIMIT),
        name="ffn_prompt",
    )(x, win, wdw, bdw, wout, g, b)
    return y, st[:, SUBLANES - 2:, :]


def _sc_prompt_kernel(x_ref, win_ref, wdw_ref, bdw_ref, wout_ref, g_ref, b_ref,
                      y_ref, st_ref, xb_ref, carry_ref, *, tiles_per_seq):
    i = pl.program_id(0)

    @pl.when(i % tiles_per_seq == 0)
    def _():
        carry_ref[...] = jnp.zeros_like(carry_ref)

    tm, d = x_ref.shape
    cw = SC_CHUNK
    n_chunks = d // cw
    xb_ref[...] = x_ref[...].astype(BF16)

    def up(f):
        xb = xb_ref[...]
        return jnp.concatenate([_dot(xb, win_ref[:, k * d + f * cw:k * d + (f + 1) * cw]) for k in range(3)],
                               axis=1)

    def act(f, z):
        cs = slice(f * cw, (f + 1) * cw)
        ch_rows = lambda lo, hi: z[lo:hi, cw:2 * cw] * z[lo:hi, 2 * cw:]
        m = _causal_conv3_blocks(
            ch_rows, tm, carry_ref[:, cs], wdw_ref[:, cs], bdw_ref[:, cs],
            lambda lo, conv: (z[lo:lo + CONV_ROWS, :cw] * conv).astype(BF16))
        tail = ch_rows(tm - SUBLANES, tm)
        carry_ref[:, cs] = tail
        st_ref[0, :, cs] = tail
        return m

    def down(f, m):
        return _dot(m, wout_ref[f * cw:(f + 1) * cw, :])

    z = up(0)
    m_prev = None
    acc = None
    for f in range(n_chunks):
        z_next = up(f + 1) if f + 1 < n_chunks else None
        m = act(f, z)
        if m_prev is not None:
            p = down(f - 1, m_prev)
            acc = p if acc is None else acc + p
        z, m_prev = z_next, m
    half = tm // 2
    w_last = wout_ref[(n_chunks - 1) * cw:n_chunks * cw, :]
    for lo_r in (0, half):
        rs = slice(lo_r, lo_r + half)
        tot = acc[rs] + _dot(m_prev[rs], w_last)
        y_ref[rs, :] = _layer_norm(ALPHA * x_ref[rs, :] + tot, g_ref[...], b_ref[...])


def _sc_prompt(x, seq_len, layer, li, win, wdw, bdw, wout, g, b):
    n, d = x.shape
    tm = min(ROW_TILE, seq_len)
    assert seq_len % tm == 0 and n % seq_len == 0
    n_seq = n // seq_len
    tps = seq_len // tm
    buffered = lambda spec: pl.BlockSpec(spec.block_shape, spec.index_map, pipeline_mode=pl.Buffered(1))
    y, st = pl.pallas_call(
        functools.partial(_sc_prompt_kernel, tiles_per_seq=tps),
        grid=(n // tm,),
        in_specs=[
            pl.BlockSpec((tm, d), lambda i: (i, 0)),
            buffered(_layer_spec(win.shape, li)),
            _layer_spec(wdw.shape, li),
            _layer_spec(bdw.shape, li),
            buffered(_layer_spec(wout.shape, li)),
            _layer_spec(g.shape, layer),
            _layer_spec(b.shape, layer),
        ],
        out_specs=[
            pl.BlockSpec((tm, d), lambda i: (i, 0)),
            pl.BlockSpec((1, SUBLANES, d), lambda i: (i // tps, 0, 0)),
        ],
        out_shape=[
            jax.ShapeDtypeStruct((n, d), F32),
            jax.ShapeDtypeStruct((n_seq, SUBLANES, d), F32),
        ],
        scratch_shapes=[
            pltpu.VMEM((tm, d), BF16),
            pltpu.VMEM((SUBLANES, d), F32),
        ],
        compiler_params=pltpu.CompilerParams(
            dimension_semantics=("arbitrary",), vmem_limit_bytes=VMEM_LIMIT),
        name="sc_prompt",
    )(x, win, wdw, bdw, wout, g, b)
    return y, st[:, SUBLANES - 2:, :]


def _rotary(x, cosf, sinf):
    return x * cosf + pltpu.roll(x, RET_DK // 2, 1) * sinf


def _gla_tile(qg, kg, vg, wn_all, r_all, st, mask_q, masks_p, mask_bd, sblock):
    c = GLA_CHUNK
    n_chunks = qg.shape[0] // c
    n_sub = c // GLA_SUB
    pre = []
    for ci in range(n_chunks):
        sl = slice(ci * c, (ci + 1) * c)
        q_c, k_c, v_c, wn, r = qg[sl], kg[sl], vg[sl], wn_all[sl], r_all[sl]
        cum = r + wn
        tot = cum[c - 1:c, :]
        vst = jnp.concatenate([v_c[:, h * GLA_DV:(h + 1) * GLA_DV] for h in range(GLA_HEADS)], axis=0)
        pre.append(dict(q=q_c, k=k_c, r=r, cum=cum, tot=tot, qw=q_c * jnp.exp(wn), vst=vst,
                        vst_b=vst.astype(BF16)))
    scores = []
    for p in pre:
        row = []
        for i in range(n_sub):
            r_i = p["r"][i * GLA_SUB:i * GLA_SUB + 1, :]
            kt = jnp.where(sblock <= i, p["k"] * jnp.exp(r_i - p["cum"]), 0.0)
            kbig = jnp.concatenate([kt] * GLA_HEADS, axis=0).astype(BF16)
            q16 = p["qw"][i * GLA_SUB:(i + 1) * GLA_SUB]
            qs = jnp.where(mask_q, jnp.concatenate([q16] * GLA_HEADS, axis=0), 0.0).astype(BF16)
            row.append(_dot_nt(qs, kbig))
        scores.append(row)
    intra = [[_dot(jnp.where(masks_p[i], scores[ci][i], 0.0).astype(BF16), p["vst_b"])
              for i in range(n_sub)] for ci, p in enumerate(pre)]
    upds, qcss = [], []
    for p in pre:
        kd = p["k"] * jnp.exp(p["tot"] - p["cum"])
        kdbig = jnp.where(mask_bd, jnp.concatenate([kd] * GLA_HEADS, axis=0), 0.0).astype(BF16)
        upds.append(_dot(p["vst"].T.astype(BF16), kdbig))
        qc = p["q"] * jnp.exp(p["cum"])
        qcss.append(jnp.where(mask_bd, jnp.concatenate([qc] * GLA_HEADS, axis=0), 0.0).astype(BF16))
    inters = []
    for ci, p in enumerate(pre):
        inters.append(_dot_nt(qcss[ci], st.astype(BF16)))
        st = jnp.exp(p["tot"]) * st + upds[ci]
    chunks = []
    for ci in range(n_chunks):
        heads = []
        for h in range(GLA_HEADS):
            intra_h = jnp.concatenate([intra[ci][i][h * GLA_SUB:(h + 1) * GLA_SUB] for i in range(n_sub)], axis=0)
            heads.append(intra_h + inters[ci][h * c:(h + 1) * c])
        chunks.append(jnp.concatenate(heads, axis=1))
    return jnp.concatenate(chunks, axis=0), st


_MIX_N_IN = 16


def _mix_prompt_kernel(*refs, n_cast):
    (x_ref, win_ref, wup_ref, bgate_ref, ggla_ref, gret_ref, bret_ref, wout_ref,
     lng_ref, lnb_ref, cos_ref, sin_ref, mcat_ref, dmask_ref, qdec_ref, kdec_ref) = refs[:_MIX_N_IN]
    cast_in = refs[_MIX_N_IN:_MIX_N_IN + n_cast]
    y_ref, sg_ref, sr_ref = refs[_MIX_N_IN + n_cast:_MIX_N_IN + n_cast + 3]
    cast_out = refs[_MIX_N_IN + n_cast + 3:_MIX_N_IN + 2 * n_cast + 3]
    stg_ref, str_ref = refs[_MIX_N_IN + 2 * n_cast + 3:]
    for src, dst in zip(cast_in, cast_out):
        dst[...] = src[...].astype(BF16)
    j = pl.program_id(1)
    nj = pl.num_programs(1)

    @pl.when(j == 0)
    def _():
        stg_ref[...] = jnp.zeros_like(stg_ref)
        str_ref[...] = jnp.zeros_like(str_ref)

    tc = x_ref.shape[0]
    x = x_ref[...]
    xb = x.astype(BF16)

    def proj(off, width):
        return _dot(xb, win_ref[:, off:off + width])

    lr = proj(_OFF_LR, PAD_RANK)
    gate = _dot(lr.astype(BF16), wup_ref[...]) + bgate_ref[...]
    lg = _log_sigmoid(gate) * (1.0 / GLA_TAU)
    hi, mid, lo = _split3(lg)
    qr_all = proj(_OFF_QR, RET_QK)
    kr_all = proj(_OFF_KR, RET_QK)
    vr_all = proj(_OFF_VR, RET_V)
    qg = proj(_OFF_QG, GLA_QK) * (GLA_DK ** -0.5)
    kg = proj(_OFF_KG, GLA_QK)
    vg = proj(_OFF_VG, GLA_V)
    cs = _dot(mcat_ref[...], jnp.concatenate([hi, mid, lo], axis=1))
    cs = cs[:, :GLA_QK] + cs[:, GLA_QK:2 * GLA_QK] + cs[:, 2 * GLA_QK:]
    wn_all = cs[:tc]
    r_all = cs[tc:]

    cosf = cos_ref[...]
    sinf = sin_ref[...]
    hs = range(RET_HEADS)
    hsl = [slice(h * RET_DK, (h + 1) * RET_DK) for h in hs]
    qhs = [_rotary(qr_all[:, hsl[h]], cosf, sinf) for h in hs]
    khs = [_rotary(kr_all[:, hsl[h]], cosf, sinf) * (RET_DK ** -0.5) for h in hs]
    vhs = [vr_all[:, hsl[h]].astype(BF16) for h in hs]
    scs = [_dot_nt(qhs[h].astype(BF16), khs[h].astype(BF16)) for h in hs]
    s_hs = [str_ref[h] for h in hs]
    inter_r = [_dot((qhs[h] * qdec_ref[h]).astype(BF16), s_hs[h].astype(BF16)) for h in hs]
    upd_r = [_dot((khs[h] * kdec_ref[h]).T.astype(BF16), vhs[h]) for h in hs]
    o_r = [_dot((scs[h] * dmask_ref[h]).astype(BF16), vhs[h]) + inter_r[h] for h in hs]
    for h in hs:
        str_ref[h] = qdec_ref[h][tc - 1:tc, :] * s_hs[h] + upd_r[h]
    gr_all = proj(_OFF_GR, RET_V)
    rg = proj(_OFF_RG, GLA_V)


    c = GLA_CHUNK
    hq = GLA_HEADS * GLA_SUB
    row64 = lax.broadcasted_iota(jnp.int32, (hq, GLA_QK), 0)
    lane64 = lax.broadcasted_iota(jnp.int32, (hq, GLA_QK), 1)
    lane_head = lax.shift_right_logical(lane64, 6)
    row_head = lax.shift_right_logical(row64, GLA_SUB.bit_length() - 1)
    mask_q = lane_head == row_head
    masks_p = [mask_q & ((lane64 & (c - 1)) <= (row64 & (GLA_SUB - 1)) + i * GLA_SUB)
               for i in range(c // GLA_SUB)]
    rowb = lax.broadcasted_iota(jnp.int32, (GLA_HEADS * c, GLA_QK), 0)
    laneb = lax.broadcasted_iota(jnp.int32, (GLA_HEADS * c, GLA_QK), 1)
    mask_bd = lax.shift_right_logical(rowb, 6) == lax.shift_right_logical(laneb, 6)
    sblock = lax.shift_right_logical(lax.broadcasted_iota(jnp.int32, (c, GLA_QK), 0), GLA_SUB.bit_length() - 1)

    og, st = _gla_tile(qg, kg, vg, wn_all, r_all, stg_ref[...], mask_q, masks_p, mask_bd, sblock)
    stg_ref[...] = st
    ggla = ggla_ref[...]
    og_heads = []
    for h in range(GLA_HEADS):
        sl = slice(h * GLA_DV, (h + 1) * GLA_DV)
        o = og[:, sl]
        o = o * lax.rsqrt(jnp.mean(o * o, axis=-1, keepdims=True) + NORM_EPS) * ggla
        og_heads.append(o * _silu(rg[:, sl]))
    or_heads = []
    for h in hs:
        o = o_r[h]
        mu = jnp.mean(o, axis=-1, keepdims=True)
        oc = o - mu
        var = jnp.mean(oc * oc, axis=-1, keepdims=True)
        o = oc * lax.rsqrt(var + NORM_EPS) * gret_ref[:, hsl[h]] + bret_ref[:, hsl[h]]
        or_heads.append(o * _silu(gr_all[:, hsl[h]]))

    mixed = jnp.concatenate(og_heads + or_heads, axis=1).astype(BF16)
    half = tc // 2
    for lo_r in (0, half):
        rs = slice(lo_r, lo_r + half)
        y = _dot(mixed[rs], wout_ref[...])
        y_ref[rs, :] = _layer_norm(ALPHA * x[rs] + y, lng_ref[...], lnb_ref[...])

    @pl.when(j == nj - 1)
    def _():
        sg_ref[0] = stg_ref[...].T
        sr_ref[0] = str_ref[...]


def _mix_prompt(x, n_seq, seq_len, layer, li, win, wup, bgate, ggla, gret, bret, wout, lng, lnb, tables,
                to_cast=()):
    n, d = x.shape
    tc = min(MIX_TILE, seq_len)
    assert seq_len % tc == 0 and tc % GLA_CHUNK == 0
    nt = seq_len // tc
    cosf, sinf, mcat, dmask, qdec, kdec = tables
    c2 = lambda b, j: (0, 0)
    c3 = lambda b, j: (0, 0, 0)
    buffered = lambda spec: pl.BlockSpec(spec.block_shape, spec.index_map, pipeline_mode=pl.Buffered(1))
    n_steps = n_seq * nt
    flat = [w.reshape(-1, w.shape[-1]) for w in to_cast]
    for w in flat:
        assert w.shape[0] % (n_steps * 2 * SUBLANES) == 0, w.shape
    cast_specs = [pl.BlockSpec((w.shape[0] // n_steps, w.shape[1]), lambda b, j: (b * nt + j, 0)) for w in flat]
    y, sg, sr, *casted = pl.pallas_call(
        functools.partial(_mix_prompt_kernel, n_cast=len(flat)),
        grid=(n_seq, nt),
        in_specs=[
            pl.BlockSpec((tc, d), lambda b, j: (b * nt + j, 0)),
            buffered(_layer_spec(win.shape, li)),
            _layer_spec(wup.shape, li),
            _layer_spec(bgate.shape, li),
            _layer_spec(ggla.shape, li),
            _layer_spec(gret.shape, li),
            _layer_spec(bret.shape, li),
            buffered(_layer_spec(wout.shape, li)),
            _layer_spec(lng.shape, layer),
            _layer_spec(lnb.shape, layer),
            pl.BlockSpec((tc, RET_DK), lambda b, j: (j, 0)),
            pl.BlockSpec((tc, RET_DK), lambda b, j: (j, 0)),
            pl.BlockSpec((2 * tc, tc), c2),
            pl.BlockSpec((RET_HEADS, tc, tc), c3),
            pl.BlockSpec((RET_HEADS, tc, RET_DK), c3),
            pl.BlockSpec((RET_HEADS, tc, RET_DK), c3),
        ] + cast_specs,
        out_specs=[
            pl.BlockSpec((tc, d), lambda b, j: (b * nt + j, 0)),
            pl.BlockSpec((1, GLA_QK, GLA_DV), lambda b, j: (b, 0, 0)),
            pl.BlockSpec((1, RET_HEADS, RET_DK, RET_DV), lambda b, j: (b, 0, 0, 0)),
        ] + cast_specs,
        out_shape=[
            jax.ShapeDtypeStruct((n, d), F32),
            jax.ShapeDtypeStruct((n_seq, GLA_QK, GLA_DV), F32),
            jax.ShapeDtypeStruct((n_seq, RET_HEADS, RET_DK, RET_DV), F32),
        ] + [jax.ShapeDtypeStruct(w.shape, BF16) for w in flat],
        scratch_shapes=[
            pltpu.VMEM((GLA_DV, GLA_QK), F32),
            pltpu.VMEM((RET_HEADS, RET_DK, RET_DV), F32),
        ],
        compiler_params=pltpu.CompilerParams(
            dimension_semantics=("arbitrary", "arbitrary"), vmem_limit_bytes=VMEM_LIMIT),
        name="mix_prompt",
    )(x, win, wup, bgate, ggla, gret, bret, wout, lng, lnb, cosf, sinf, mcat, dmask, qdec, kdec, *flat)
    casted = [c.reshape(w.shape) for c, w in zip(casted, to_cast)]
    return y, sg.reshape(n_seq, GLA_HEADS, GLA_DK, GLA_DV), sr, casted


def _mix_tables(seq_len):
    tc = min(MIX_TILE, seq_len)
    pos = jnp.arange(seq_len, dtype=F32)
    cosf, sinf = _rotary_tables(pos)
    t = jnp.arange(tc)
    same_sub = (t[:, None] // GLA_SUB) == (t[None, :] // GLA_SUB)
    same_chunk = (t[:, None] // GLA_CHUNK) == (t[None, :] // GLA_CHUNK)
    m_within = same_sub & (t[None, :] <= t[:, None])
    m_before = same_chunk & ((t[None, :] // GLA_SUB) < (t[:, None] // GLA_SUB))
    mcat = jnp.concatenate([m_within, m_before], axis=0).astype(BF16)
    log_gamma = jnp.log1p(-jnp.exp2(-5.0 - jnp.arange(RET_HEADS, dtype=F32)))
    tf = t.astype(F32)
    diff = tf[:, None] - tf[None, :]
    dmask = jnp.where(diff >= 0, jnp.exp(diff[None] * log_gamma[:, None, None]), 0.0)
    qdec = jnp.exp((tf[None, :, None] + 1.0) * log_gamma[:, None, None])
    kdec = jnp.exp((tc - 1.0 - tf[None, :, None]) * log_gamma[:, None, None])
    qdec = jnp.broadcast_to(qdec, (RET_HEADS, tc, RET_DK))
    kdec = jnp.broadcast_to(kdec, (RET_HEADS, tc, RET_DK))
    return cosf, sinf, mcat, dmask, qdec, kdec


def _rotary_tables(pos):
    inv_freq = 1.0 / (ROPE_BASE ** (jnp.arange(0, RET_DK, 2, dtype=F32) / RET_DK))
    ang = pos[:, None] * inv_freq[None, :]
    cos, sin = jnp.cos(ang), jnp.sin(ang)
    return jnp.concatenate([cos, cos], axis=1), jnp.concatenate([-sin, sin], axis=1)


def _ffn_sample_kernel(x_ref, st_ref, win_ref, wdw_ref, bdw_ref, wout_ref, g_ref, b_ref,
                       y_ref, nst_ref):
    x = x_ref[...]
    u = _dot(x.astype(BF16), win_ref[...])
    wdw = wdw_ref[...]
    b1 = st_ref[:, 1, :]
    c = bdw_ref[...] + wdw[0:1] * st_ref[:, 0, :] + wdw[1:2] * b1 + wdw[2:3] * u
    h = _silu(c[:, D_FF:]) * c[:, :D_FF]
    y = _dot(h.astype(BF16), wout_ref[...])
    y_ref[...] = _layer_norm(ALPHA * x + y, g_ref[...], b_ref[...])
    nst_ref[:, 0, :] = b1
    nst_ref[:, 1, :] = u


def _sc_sample_kernel(x_ref, st_ref, win_ref, wdw_ref, bdw_ref, wout_ref, g_ref, b_ref,
                      y_ref, nst_ref):
    x = x_ref[...]
    d = x.shape[1]
    z = _dot(x.astype(BF16), win_ref[...])
    ch = z[:, d:2 * d] * z[:, 2 * d:]
    wdw = wdw_ref[...]
    b1 = st_ref[:, 1, :]
    conv = bdw_ref[...] + wdw[0:1] * st_ref[:, 0, :] + wdw[1:2] * b1 + wdw[2:3] * ch
    y = _dot((z[:, :d] * conv).astype(BF16), wout_ref[...])
    y_ref[...] = _layer_norm(ALPHA * x + y, g_ref[...], b_ref[...])
    nst_ref[:, 0, :] = b1
    nst_ref[:, 1, :] = ch


def _conv_sample_kernel(body, *refs):
    body(*refs[:8], *refs[9:])


def _conv_sample(body, name, x, states, new_states, layer, li, win, wdw, bdw, wout, g, b):
    s, d = x.shape
    whole = lambda a: pl.BlockSpec(a.shape, lambda i: (0,) * a.ndim)
    st_spec = _layer_spec(states.shape, li)
    return pl.pallas_call(
        functools.partial(_conv_sample_kernel, body),
        grid=(1,),
        in_specs=[whole(x), st_spec, _layer_spec(win.shape, li), _layer_spec(wdw.shape, li),
                  _layer_spec(bdw.shape, li), _layer_spec(wout.shape, li),
                  _layer_spec(g.shape, layer), _layer_spec(b.shape, layer),
                  pl.BlockSpec(memory_space=pl.ANY)],
        out_specs=[whole(x), st_spec],
        out_shape=[jax.ShapeDtypeStruct((s, d), F32), jax.ShapeDtypeStruct(states.shape, F32)],
        input_output_aliases={8: 1},
        compiler_params=pltpu.CompilerParams(
            dimension_semantics=("arbitrary",), vmem_limit_bytes=VMEM_LIMIT),
        name=name,
    )(x, states, win, wdw, bdw, wout, g, b, new_states)


def _mix_sample_proj_kernel(x_ref, win_ref, wup_ref, bgate_ref, cos_ref, sin_ref,
                            cols_ref, rows_ref, gates_ref):
    xb = x_ref[...].astype(BF16)

    def proj(off, width):
        return _dot(xb, win_ref[:, off:off + width])

    qg = proj(_OFF_QG, GLA_QK) * (GLA_DK ** -0.5)
    kg = proj(_OFF_KG, GLA_QK)
    lr = proj(_OFF_LR, PAD_RANK)
    gate = _dot(lr.astype(BF16), wup_ref[...]) + bgate_ref[...]
    ag = jnp.exp(_log_sigmoid(gate) * (1.0 / GLA_TAU))
    cosf = cos_ref[...]
    sinf = sin_ref[...]
    qr = [_rotary(proj(_OFF_QR + h * RET_DK, RET_DK), cosf, sinf) for h in range(RET_HEADS)]
    kr = [_rotary(proj(_OFF_KR + h * RET_DK, RET_DK), cosf, sinf) * (RET_DK ** -0.5)
          for h in range(RET_HEADS)]
    cols_ref[0:GLA_QK, :] = ag.T
    cols_ref[GLA_QK:2 * GLA_QK, :] = kg.T
    cols_ref[2 * GLA_QK:3 * GLA_QK, :] = qg.T
    base = 3 * GLA_QK
    for h in range(RET_HEADS):
        cols_ref[base + h * RET_DK:base + (h + 1) * RET_DK, :] = kr[h].T
        cols_ref[base + RET_QK + h * RET_DK:base + RET_QK + (h + 1) * RET_DK, :] = qr[h].T
    rows_ref[:, :GLA_V] = proj(_OFF_VG, GLA_V)
    rows_ref[:, GLA_V:] = proj(_OFF_VR, RET_V)
    gates_ref[:, :GLA_V] = proj(_OFF_RG, GLA_V)
    gates_ref[:, GLA_V:] = proj(_OFF_GR, RET_V)


_COL_ROWS = 3 * GLA_QK + 2 * RET_QK


def _mix_sample_state_kernel(*refs, n_stacked_in):
    cols_ref, rows_ref, sg_ref, sr_ref, gam_ref = refs[:5]
    nsg_ref, nsr_ref, o_ref = refs[5 + n_stacked_in:]
    nb = sg_ref.shape[0]
    s = cols_ref.shape[1]
    blk = pl.program_id(0)
    lane = lax.broadcasted_iota(jnp.int32, (1, s), 1)

    def body(bb, carry):
        b = blk * nb + bb
        onehot = (lane == b).astype(F32)
        col = jnp.sum(cols_ref[...] * onehot, axis=1, keepdims=True)
        vrow = rows_ref[b]
        a_g = col[0:GLA_QK]
        k_g = col[GLA_QK:2 * GLA_QK]
        q_g = col[2 * GLA_QK:3 * GLA_QK]
        base = 3 * GLA_QK
        outs = []
        for h in range(GLA_HEADS):
            rs = slice(h * GLA_DK, (h + 1) * GLA_DK)
            v_h = vrow[:, h * GLA_DV:(h + 1) * GLA_DV]
            s_new = a_g[rs] * sg_ref[bb, h] + k_g[rs] * v_h
            nsg_ref[bb, h] = s_new
            outs.append(jnp.sum(q_g[rs] * s_new, axis=0, keepdims=True))
        for h in range(RET_HEADS):
            rs = slice(h * RET_DK, (h + 1) * RET_DK)
            k_h = col[base + h * RET_DK:base + (h + 1) * RET_DK]
            q_h = col[base + RET_QK + h * RET_DK:base + RET_QK + (h + 1) * RET_DK]
            v_h = vrow[:, GLA_V + h * RET_DV:GLA_V + (h + 1) * RET_DV]
            s_new = gam_ref[h] * sr_ref[bb, h] + k_h * v_h
            nsr_ref[bb, h] = s_new
            outs.append(jnp.sum(q_h * s_new, axis=0, keepdims=True))
        o_ref[b] = jnp.concatenate(outs, axis=1)
        return carry

    lax.fori_loop(0, nb, body, 0)


def _mix_sample_out_kernel(x_ref, o_ref, gates_ref, ggla_ref, gret_ref, bret_ref, wout_ref,
                           lng_ref, lnb_ref, y_ref):
    x = x_ref[...]
    o = o_ref[...]
    gates = gates_ref[...]
    ggla = ggla_ref[...]
    heads = []
    for h in range(GLA_HEADS):
        sl = slice(h * GLA_DV, (h + 1) * GLA_DV)
        oh = o[:, sl]
        oh = oh * lax.rsqrt(jnp.mean(oh * oh, axis=-1, keepdims=True) + NORM_EPS) * ggla
        heads.append(oh * _silu(gates[:, sl]))
    for h in range(RET_HEADS):
        sl = slice(GLA_V + h * RET_DV, GLA_V + (h + 1) * RET_DV)
        ps = slice(h * RET_DV, (h + 1) * RET_DV)
        oh = o[:, sl]
        mu = jnp.mean(oh, axis=-1, keepdims=True)
        oc = oh - mu
        var = jnp.mean(oc * oc, axis=-1, keepdims=True)
        oh = oc * lax.rsqrt(var + NORM_EPS) * gret_ref[:, ps] + bret_ref[:, ps]
        heads.append(oh * _silu(gates[:, sl]))
    mixed = jnp.concatenate(heads, axis=1).astype(BF16)
    y = _dot(mixed, wout_ref[...])
    y_ref[...] = _layer_norm(ALPHA * x + y, lng_ref[...], lnb_ref[...])


def _mix_sample(x, state_gla, state_ret, layer, li, stacked, win, wup, bgate, ggla, gret, bret, wout, lng, lnb,
                cos_s, sin_s, gam):
    s, d = x.shape
    whole = lambda a: pl.BlockSpec(a.shape, lambda i: (0,) * a.ndim)
    proj_out = [
        jax.ShapeDtypeStruct((_COL_ROWS, s), F32),
        jax.ShapeDtypeStruct((s, GLA_V + RET_V), F32),
        jax.ShapeDtypeStruct((s, GLA_V + RET_V), F32),
    ]
    cols, rows, gates = pl.pallas_call(
        _mix_sample_proj_kernel,
        grid=(1,),
        in_specs=[whole(x), _layer_spec(win.shape, li), _layer_spec(wup.shape, li),
                  _layer_spec(bgate.shape, li), whole(cos_s), whole(sin_s)],
        out_specs=[whole(o) for o in proj_out],
        out_shape=proj_out,
        compiler_params=pltpu.CompilerParams(
            dimension_semantics=("arbitrary",), vmem_limit_bytes=VMEM_LIMIT),
        name="mix_sample_proj",
    )(x, win, wup, bgate, cos_s, sin_s)
    rows = rows.reshape(s, 1, GLA_V + RET_V)

    nb = SUBLANES
    assert s % nb == 0
    gla_spec = pl.BlockSpec((None, nb, GLA_HEADS, GLA_DK, GLA_DV), lambda i: (li, i, 0, 0, 0))
    ret_spec = pl.BlockSpec((None, nb, RET_HEADS, RET_DK, RET_DV), lambda i: (li, i, 0, 0, 0))
    stacked = tuple(stacked)
    nsg, nsr, o = pl.pallas_call(
        functools.partial(_mix_sample_state_kernel, n_stacked_in=len(stacked)),
        grid=(s // nb,),
        in_specs=[
            pl.BlockSpec((_COL_ROWS, s), lambda i: (0, 0)),
            pl.BlockSpec((s, 1, GLA_V + RET_V), lambda i: (0, 0, 0)),
            gla_spec,
            ret_spec,
            pl.BlockSpec((RET_HEADS, RET_DK, RET_DV), lambda i: (0, 0, 0)),
        ] + [pl.BlockSpec(memory_space=pl.ANY)] * len(stacked),
        out_specs=[
            gla_spec,
            ret_spec,
            pl.BlockSpec((s, 1, GLA_V + RET_V), lambda i: (0, 0, 0)),
        ],
        out_shape=[
            jax.ShapeDtypeStruct(state_gla.shape, F32),
            jax.ShapeDtypeStruct(state_ret.shape, F32),
            jax.ShapeDtypeStruct((s, 1, GLA_V + RET_V), F32),
        ],
        input_output_aliases={5 + k: k for k in range(len(stacked))},
        compiler_params=pltpu.CompilerParams(
            dimension_semantics=("arbitrary",), vmem_limit_bytes=VMEM_LIMIT),
        name="mix_sample_state",
    )(cols, rows, state_gla, state_ret, gam, *stacked)
    o = o.reshape(s, GLA_V + RET_V)

    y = pl.pallas_call(
        _mix_sample_out_kernel,
        grid=(1,),
        in_specs=[whole(x), whole(o), whole(gates), _layer_spec(ggla.shape, li), _layer_spec(gret.shape, li),
                  _layer_spec(bret.shape, li), _layer_spec(wout.shape, li),
                  _layer_spec(lng.shape, layer), _layer_spec(lnb.shape, layer)],
        out_specs=whole(x),
        out_shape=jax.ShapeDtypeStruct((s, d), F32),
        compiler_params=pltpu.CompilerParams(
            dimension_semantics=("arbitrary",), vmem_limit_bytes=VMEM_LIMIT),
        name="mix_sample_out",
    )(x, o, gates, ggla, gret, bret, wout, lng, lnb)
    return y, nsg, nsr


def _prep_mix_in(w):
    a = 2 * GLA_QK + GLA_V
    lr = jnp.pad(w[:, :, a:a + GLA_RANK], ((0, 0), (0, 0), (0, PAD_RANK - GLA_RANK)))
    return jnp.concatenate([w[:, :, :a], w[:, :, a + GLA_RANK:], lr], axis=2).astype(BF16)


def kernel(x_prompt, x_sample, state_gla, state_ret, state_conv, state_ffn,
           w_mix_in, w_gla_gate_up, b_gla_gate, g_gla_norm, g_ret_norm, b_ret_norm, w_mix_out,
           w_sc_in, w_sc_dw, b_sc_dw, w_sc_out, ln1_g, ln1_b,
           w_ffn_in, w_ffn_dw, b_ffn_dw, w_ffn_out, ln2_g, ln2_b):
    bp, tp, d = x_prompt.shape
    s, ts, _ = x_sample.shape
    assert ts == 1 and d == D_MODEL
    xp = x_prompt.reshape(bp * tp, d)
    xs = x_sample.reshape(s, d)

    tables = _mix_tables(tp)
    cos_s, sin_s = _rotary_tables(PAST_LEN + jnp.arange(ts, dtype=F32))
    log_gamma = jnp.log1p(-jnp.exp2(-5.0 - jnp.arange(RET_HEADS, dtype=F32)))
    gam = jnp.broadcast_to(jnp.exp(log_gamma)[:, None, None], (RET_HEADS, RET_DK, RET_DV))

    rows = lambda v: v[:, None, :]
    mix_w = (_prep_mix_in(w_mix_in),
             jnp.pad(w_gla_gate_up, ((0, 0), (0, PAD_RANK - GLA_RANK), (0, 0))).astype(BF16),
             rows(b_gla_gate), rows(g_gla_norm), rows(g_ret_norm), rows(b_ret_norm), w_mix_out.astype(BF16))
    later_w = (w_sc_in, w_sc_out, w_ffn_in, w_ffn_out)
    ln1 = (rows(ln1_g), rows(ln1_b))
    ln2 = (rows(ln2_g), rows(ln2_b))

    gla_p, ret_p, conv_p, ffn_p = [], [], [], []
    mix_s = (jnp.zeros_like(state_gla), jnp.zeros_like(state_ret))
    conv_s = jnp.zeros_like(state_conv)
    ffn_s = jnp.zeros_like(state_ffn)
    for layer in range(DEPTH):
        li = layer // 2
        if layer % 2 == 0:
            xp, sg, sr, casted = _mix_prompt(xp, bp, tp, layer, li, *mix_w, *ln1, tables,
                                             to_cast=later_w if layer == 0 else ())
            if layer == 0:
                sc_in, sc_out, ffn_in, ffn_out = casted
                sc_w = (sc_in, w_sc_dw, rows(b_sc_dw), sc_out)
                ffn_w = (ffn_in, w_ffn_dw, rows(b_ffn_dw), ffn_out)
            gla_p.append(sg)
            ret_p.append(sr)
            xs, *mix_s = _mix_sample(xs, state_gla, state_ret, layer, li, mix_s, *mix_w, *ln1,
                                     cos_s, sin_s, gam)
        else:
            xp, sc = _sc_prompt(xp, tp, layer, li, *sc_w, *ln1)
            conv_p.append(sc)
            xs, conv_s = _conv_sample(_sc_sample_kernel, "sc_sample", xs, state_conv, conv_s,
                                      layer, li, *sc_w, *ln1)
        xp, sf = _ffn_prompt(xp, tp, layer, *ffn_w, *ln2)
        ffn_p.append(sf)
        xs, ffn_s = _conv_sample(_ffn_sample_kernel, "ffn_sample", xs, state_ffn, ffn_s,
                                 layer, layer, *ffn_w, *ln2)
    gla_s, ret_s = mix_s

    return (xp.reshape(bp, tp, d), xs.reshape(s, ts, d),
            jnp.stack(gla_p), gla_s, jnp.stack(ret_p), ret_s,
            jnp.stack(conv_p), conv_s, jnp.stack(ffn_p), ffn_s)
```

```python
import functools

import jax
import jax.numpy as jnp
from jax import lax
from jax.experimental import pallas as pl
from jax.experimental.pallas import tpu as pltpu

F32 = jnp.float32
BF16 = jnp.bfloat16

D_MODEL = 1024
DEPTH = 4
PAST_LEN = 16384
GLA_HEADS = 4
GLA_DK = 64
GLA_DV = 128
GLA_RANK = 16
GLA_TAU = 16.0
RET_HEADS = 4
RET_DK = 128
RET_DV = 128
CONV_W = 3
D_FF = D_MODEL * 11 // 4
ROPE_BASE = 10000.0
NORM_EPS = 1e-5
ALPHA = (2 * DEPTH) ** 0.25
GLA_QK = GLA_HEADS * GLA_DK
GLA_V = GLA_HEADS * GLA_DV
RET_QK = RET_HEADS * RET_DK
RET_V = RET_HEADS * RET_DV

LANES = 128
SUBLANES = 8
VMEM_LIMIT = 56 * 1024 * 1024

GLA_CHUNK = 64
GLA_SUB = 32
MIX_TILE = 256
ROW_TILE = 512
CONV_ROWS = 32
FF_CHUNK = 768
FF_FIRST_CHUNK = 512
SC_CHUNK = 256
PAD_RANK = LANES

_OFF_QG = 0
_OFF_KG = _OFF_QG + GLA_QK
_OFF_VG = _OFF_KG + GLA_QK
_OFF_RG = _OFF_VG + GLA_V
_OFF_QR = _OFF_RG + GLA_V
_OFF_KR = _OFF_QR + RET_QK
_OFF_VR = _OFF_KR + RET_QK
_OFF_GR = _OFF_VR + RET_V
_OFF_LR = _OFF_GR + RET_V
MIX_COLS = _OFF_LR + PAD_RANK


def _dot(a, b):
    return jnp.dot(a, b, preferred_element_type=F32)


def _dot_nt(a, b):
    return lax.dot_general(a, b, (((1,), (1,)), ((), ())), preferred_element_type=F32)


def _layer_norm(x, g, b):
    mu = jnp.mean(x, axis=-1, keepdims=True)
    xc = x - mu
    var = jnp.mean(xc * xc, axis=-1, keepdims=True)
    return xc * lax.rsqrt(var + NORM_EPS) * g + b


def _silu(x):
    return x * jax.nn.sigmoid(x)


def _log_sigmoid(x):
    return jnp.minimum(x, 0.0) - jnp.log1p(jnp.exp(-jnp.abs(x)))


def _split3(x):
    hi = x.astype(BF16)
    r1 = x - hi.astype(F32)
    mid = r1.astype(BF16)
    lo = (r1 - mid.astype(F32)).astype(BF16)
    return hi, mid, lo


def _causal_conv3_blocks(rows, n_rows, prev, w, b, post):
    w0, w1, w2 = w[0:1], w[1:2], w[2:3]
    out = []
    for lo in range(0, n_rows, CONV_ROWS):
        head = prev if lo == 0 else rows(lo - SUBLANES, lo)
        s = jnp.concatenate([head, rows(lo, lo + CONV_ROWS)], axis=0)
        s1 = pltpu.roll(s, 1, 0)[SUBLANES:]
        s2 = pltpu.roll(s, 2, 0)[SUBLANES:]
        out.append(post(lo, b + w0 * s2 + w1 * s1 + w2 * s[SUBLANES:]))
    return jnp.concatenate(out, axis=0)


def _ffn_prompt_kernel(x_ref, win_ref, wdw_ref, bdw_ref, wout_ref, g_ref, b_ref,
                       y_ref, st_ref, xb_ref, carry_ref, *, tiles_per_seq):
    i = pl.program_id(0)

    @pl.when(i % tiles_per_seq == 0)
    def _():
        carry_ref[...] = jnp.zeros_like(carry_ref)

    tm = x_ref.shape[0]
    ff = wout_ref.shape[0]
    first = min(FF_FIRST_CHUNK, ff)
    bounds = [0] + list(range(first, ff, FF_CHUNK)) + [ff]
    n_chunks = len(bounds) - 1
    xb_ref[...] = x_ref[...].astype(BF16)

    def cols(ref, f):
        lo, hi = bounds[f], bounds[f + 1]
        return jnp.concatenate([ref[:, lo:hi], ref[:, ff + lo:ff + hi]], axis=1)

    def up(f):
        lo, hi = bounds[f], bounds[f + 1]
        xb = xb_ref[...]
        return jnp.concatenate([_dot(xb, win_ref[:, lo:hi]), _dot(xb, win_ref[:, ff + lo:ff + hi])], axis=1)

    def act(f, u):
        lo, hi = bounds[f], bounds[f + 1]
        fc = hi - lo
        h = _causal_conv3_blocks(
            lambda r0, r1: u[r0:r1], tm, cols(carry_ref, f), cols(wdw_ref, f), cols(bdw_ref, f),
            lambda r0, c: (_silu(c[:, fc:]) * c[:, :fc]).astype(BF16))
        for k in range(2):
            tail = u[tm - SUBLANES:, k * fc:(k + 1) * fc]
            carry_ref[:, k * ff + lo:k * ff + hi] = tail
            st_ref[0, :, k * ff + lo:k * ff + hi] = tail
        return h

    def down(f, h):
        return _dot(h, wout_ref[bounds[f]:bounds[f + 1], :])

    u = up(0)
    h_prev = None
    acc = None
    for f in range(n_chunks):
        u_next = up(f + 1) if f + 1 < n_chunks else None
        h = act(f, u)
        if h_prev is not None:
            p = down(f - 1, h_prev)
            acc = p if acc is None else acc + p
        u, h_prev = u_next, h
    half = tm // 2
    w_last = wout_ref[bounds[n_chunks - 1]:bounds[n_chunks], :]
    for lo_r in (0, half):
        rs = slice(lo_r, lo_r + half)
        tot = acc[rs] + _dot(h_prev[rs], w_last)
        y_ref[rs, :] = _layer_norm(ALPHA * x_ref[rs, :] + tot, g_ref[...], b_ref[...])


def _layer_spec(shape, layer):
    zeros = (0,) * (len(shape) - 1)
    return pl.BlockSpec((None,) + tuple(shape[1:]), lambda *_: (layer,) + zeros)


def _ffn_prompt(x, seq_len, layer, win, wdw, bdw, wout, g, b):
    n, d = x.shape
    tm = min(ROW_TILE, seq_len)
    assert seq_len % tm == 0 and n % seq_len == 0
    n_seq = n // seq_len
    tps = seq_len // tm
    ff2 = win.shape[2]
    buffered = lambda spec: pl.BlockSpec(spec.block_shape, spec.index_map, pipeline_mode=pl.Buffered(1))
    y, st = pl.pallas_call(
        functools.partial(_ffn_prompt_kernel, tiles_per_seq=tps),
        grid=(n // tm,),
        in_specs=[
            pl.BlockSpec((tm, d), lambda i: (i, 0)),
            buffered(_layer_spec(win.shape, layer)),
            _layer_spec(wdw.shape, layer),
            _layer_spec(bdw.shape, layer),
            buffered(_layer_spec(wout.shape, layer)),
            _layer_spec(g.shape, layer),
            _layer_spec(b.shape, layer),
        ],
        out_specs=[
            pl.BlockSpec((tm, d), lambda i: (i, 0)),
            pl.BlockSpec((1, SUBLANES, ff2), lambda i: (i // tps, 0, 0)),
        ],
        out_shape=[
            jax.ShapeDtypeStruct((n, d), F32),
            jax.ShapeDtypeStruct((n_seq, SUBLANES, ff2), F32),
        ],
        scratch_shapes=[
            pltpu.VMEM((tm, d), BF16),
            pltpu.VMEM((SUBLANES, ff2), F32),
        ],
        compiler_params=pltpu.CompilerParams(
            dimension_semantics=("arbitrary",), vmem_limit_bytes=VMEM_LIMIT),
        name="ffn_prompt",
    )(x, win, wdw, bdw, wout, g, b)
    return y, st[:, SUBLANES - 2:, :]


def _sc_prompt_kernel(x_ref, win_ref, wdw_ref, bdw_ref, wout_ref, g_ref, b_ref,
                      y_ref, st_ref, xb_ref, carry_ref, *, tiles_per_seq):
    i = pl.program_id(0)

    @pl.when(i % tiles_per_seq == 0)
    def _():
        carry_ref[...] = jnp.zeros_like(carry_ref)

    tm, d = x_ref.shape
    cw = SC_CHUNK
    n_chunks = d // cw
    xb_ref[...] = x_ref[...].astype(BF16)

    def up(f):
        xb = xb_ref[...]
        return jnp.concatenate([_dot(xb, win_ref[:, k * d + f * cw:k * d + (f + 1) * cw]) for k in range(3)],
                               axis=1)

    def act(f, z):
        cs = slice(f * cw, (f + 1) * cw)
        ch_rows = lambda lo, hi: z[lo:hi, cw:2 * cw] * z[lo:hi, 2 * cw:]
        m = _causal_conv3_blocks(
            ch_rows, tm, carry_ref[:, cs], wdw_ref[:, cs], bdw_ref[:, cs],
            lambda lo, conv: (z[lo:lo + CONV_ROWS, :cw] * conv).astype(BF16))
        tail = ch_rows(tm - SUBLANES, tm)
        carry_ref[:, cs] = tail
        st_ref[0, :, cs] = tail
        return m

    def down(f, m):
        return _dot(m, wout_ref[f * cw:(f + 1) * cw, :])

    z = up(0)
    m_prev = None
    acc = None
    for f in range(n_chunks):
        z_next = up(f + 1) if f + 1 < n_chunks else None
        m = act(f, z)
        if m_prev is not None:
            p = down(f - 1, m_prev)
            acc = p if acc is None else acc + p
        z, m_prev = z_next, m
    half = tm // 2
    w_last = wout_ref[(n_chunks - 1) * cw:n_chunks * cw, :]
    for lo_r in (0, half):
        rs = slice(lo_r, lo_r + half)
        tot = acc[rs] + _dot(m_prev[rs], w_last)
        y_ref[rs, :] = _layer_norm(ALPHA * x_ref[rs, :] + tot, g_ref[...], b_ref[...])


def _sc_prompt(x, seq_len, layer, li, win, wdw, bdw, wout, g, b):
    n, d = x.shape
    tm = min(ROW_TILE, seq_len)
    assert seq_len % tm == 0 and n % seq_len == 0
    n_seq = n // seq_len
    tps = seq_len // tm
    buffered = lambda spec: pl.BlockSpec(spec.block_shape, spec.index_map, pipeline_mode=pl.Buffered(1))
    y, st = pl.pallas_call(
        functools.partial(_sc_prompt_kernel, tiles_per_seq=tps),
        grid=(n // tm,),
        in_specs=[
            pl.BlockSpec((tm, d), lambda i: (i, 0)),
            buffered(_layer_spec(win.shape, li)),
            _layer_spec(wdw.shape, li),
            _layer_spec(bdw.shape, li),
            buffered(_layer_spec(wout.shape, li)),
            _layer_spec(g.shape, layer),
            _layer_spec(b.shape, layer),
        ],
        out_specs=[
            pl.BlockSpec((tm, d), lambda i: (i, 0)),
            pl.BlockSpec((1, SUBLANES, d), lambda i: (i // tps, 0, 0)),
        ],
        out_shape=[
            jax.ShapeDtypeStruct((n, d), F32),
            jax.ShapeDtypeStruct((n_seq, SUBLANES, d), F32),
        ],
        scratch_shapes=[
            pltpu.VMEM((tm, d), BF16),
            pltpu.VMEM((SUBLANES, d), F32),
        ],
        compiler_params=pltpu.CompilerParams(
            dimension_semantics=("arbitrary",), vmem_limit_bytes=VMEM_LIMIT),
        name="sc_prompt",
    )(x, win, wdw, bdw, wout, g, b)
    return y, st[:, SUBLANES - 2:, :]


def _rotary(x, cosf, sinf):
    return x * cosf + pltpu.roll(x, RET_DK // 2, 1) * sinf


def _gla_tile(qg, kg, vg, wn_all, r_all, st, mask_q, masks_p, mask_bd, sblock):
    c = GLA_CHUNK
    n_chunks = qg.shape[0] // c
    n_sub = c // GLA_SUB
    pre = []
    for ci in range(n_chunks):
        sl = slice(ci * c, (ci + 1) * c)
        q_c, k_c, v_c, wn, r = qg[sl], kg[sl], vg[sl], wn_all[sl], r_all[sl]
        cum = r + wn
        tot = cum[c - 1:c, :]
        vst = jnp.concatenate([v_c[:, h * GLA_DV:(h + 1) * GLA_DV] for h in range(GLA_HEADS)], axis=0)
        pre.append(dict(q=q_c, k=k_c, r=r, cum=cum, tot=tot, qw=q_c * jnp.exp(wn), vst=vst,
                        vst_b=vst.astype(BF16)))
    scores = []
    for p in pre:
        row = []
        for i in range(n_sub):
            r_i = p["r"][i * GLA_SUB:i * GLA_SUB + 1, :]
            kt = jnp.where(sblock <= i, p["k"] * jnp.exp(r_i - p["cum"]), 0.0)
            kbig = jnp.concatenate([kt] * GLA_HEADS, axis=0).astype(BF16)
            q16 = p["qw"][i * GLA_SUB:(i + 1) * GLA_SUB]
            qs = jnp.where(mask_q, jnp.concatenate([q16] * GLA_HEADS, axis=0), 0.0).astype(BF16)
            row.append(_dot_nt(qs, kbig))
        scores.append(row)
    intra = [[_dot(jnp.where(masks_p[i], scores[ci][i], 0.0).astype(BF16), p["vst_b"])
              for i in range(n_sub)] for ci, p in enumerate(pre)]
    upds, qcss = [], []
    for p in pre:
        kd = p["k"] * jnp.exp(p["tot"] - p["cum"])
        kdbig = jnp.where(mask_bd, jnp.concatenate([kd] * GLA_HEADS, axis=0), 0.0).astype(BF16)
        upds.append(_dot(p["vst"].T.astype(BF16), kdbig))
        qc = p["q"] * jnp.exp(p["cum"])
        qcss.append(jnp.where(mask_bd, jnp.concatenate([qc] * GLA_HEADS, axis=0), 0.0).astype(BF16))
    inters = []
    for ci, p in enumerate(pre):
        inters.append(_dot_nt(qcss[ci], st.astype(BF16)))
        st = jnp.exp(p["tot"]) * st + upds[ci]
    chunks = []
    for ci in range(n_chunks):
        heads = []
        for h in range(GLA_HEADS):
            intra_h = jnp.concatenate([intra[ci][i][h * GLA_SUB:(h + 1) * GLA_SUB] for i in range(n_sub)], axis=0)
            heads.append(intra_h + inters[ci][h * c:(h + 1) * c])
        chunks.append(jnp.concatenate(heads, axis=1))
    return jnp.concatenate(chunks, axis=0), st


_MIX_N_IN = 16


def _mix_prompt_kernel(*refs, n_cast):
    (x_ref, win_ref, wup_ref, bgate_ref, ggla_ref, gret_ref, bret_ref, wout_ref,
     lng_ref, lnb_ref, cos_ref, sin_ref, mcat_ref, dmask_ref, qdec_ref, kdec_ref) = refs[:_MIX_N_IN]
    cast_in = refs[_MIX_N_IN:_MIX_N_IN + n_cast]
    y_ref, sg_ref, sr_ref = refs[_MIX_N_IN + n_cast:_MIX_N_IN + n_cast + 3]
    cast_out = refs[_MIX_N_IN + n_cast + 3:_MIX_N_IN + 2 * n_cast + 3]
    stg_ref, str_ref = refs[_MIX_N_IN + 2 * n_cast + 3:]
    for src, dst in zip(cast_in, cast_out):
        dst[...] = src[...].astype(BF16)
    j = pl.program_id(1)
    nj = pl.num_programs(1)

    @pl.when(j == 0)
    def _():
        stg_ref[...] = jnp.zeros_like(stg_ref)
        str_ref[...] = jnp.zeros_like(str_ref)

    tc = x_ref.shape[0]
    x = x_ref[...]
    xb = x.astype(BF16)

    def proj(off, width):
        return _dot(xb, win_ref[:, off:off + width])

    lr = proj(_OFF_LR, PAD_RANK)
    gate = _dot(lr.astype(BF16), wup_ref[...]) + bgate_ref[...]
    lg = _log_sigmoid(gate) * (1.0 / GLA_TAU)
    hi, mid, lo = _split3(lg)
    qr_all = proj(_OFF_QR, RET_QK)
    kr_all = proj(_OFF_KR, RET_QK)
    vr_all = proj(_OFF_VR, RET_V)
    qg = proj(_OFF_QG, GLA_QK) * (GLA_DK ** -0.5)
    kg = proj(_OFF_KG, GLA_QK)
    vg = proj(_OFF_VG, GLA_V)
    cs = _dot(mcat_ref[...], jnp.concatenate([hi, mid, lo], axis=1))
    cs = cs[:, :GLA_QK] + cs[:, GLA_QK:2 * GLA_QK] + cs[:, 2 * GLA_QK:]
    wn_all = cs[:tc]
    r_all = cs[tc:]

    cosf = cos_ref[...]
    sinf = sin_ref[...]
    hs = range(RET_HEADS)
    hsl = [slice(h * RET_DK, (h + 1) * RET_DK) for h in hs]
    qhs = [_rotary(qr_all[:, hsl[h]], cosf, sinf) for h in hs]
    khs = [_rotary(kr_all[:, hsl[h]], cosf, sinf) * (RET_DK ** -0.5) for h in hs]
    vhs = [vr_all[:, hsl[h]].astype(BF16) for h in hs]
    scs = [_dot_nt(qhs[h].astype(BF16), khs[h].astype(BF16)) for h in hs]
    s_hs = [str_ref[h] for h in hs]
    inter_r = [_dot((qhs[h] * qdec_ref[h]).astype(BF16), s_hs[h].astype(BF16)) for h in hs]
    upd_r = [_dot((khs[h] * kdec_ref[h]).T.astype(BF16), vhs[h]) for h in hs]
    o_r = [_dot((scs[h] * dmask_ref[h]).astype(BF16), vhs[h]) + inter_r[h] for h in hs]
    for h in hs:
        str_ref[h] = qdec_ref[h][tc - 1:tc, :] * s_hs[h] + upd_r[h]
    gr_all = proj(_OFF_GR, RET_V)
    rg = proj(_OFF_RG, GLA_V)


    c = GLA_CHUNK
    assert c == GLA_DK
    hq = GLA_HEADS * GLA_SUB
    log2 = lambda n: n.bit_length() - 1
    row64 = lax.broadcasted_iota(jnp.int32, (hq, GLA_QK), 0)
    lane64 = lax.broadcasted_iota(jnp.int32, (hq, GLA_QK), 1)
    lane_head = lax.shift_right_logical(lane64, log2(c))
    row_head = lax.shift_right_logical(row64, log2(GLA_SUB))
    mask_q = lane_head == row_head
    masks_p = [mask_q & ((lane64 & (c - 1)) <= (row64 & (GLA_SUB - 1)) + i * GLA_SUB)
               for i in range(c // GLA_SUB)]
    rowb = lax.broadcasted_iota(jnp.int32, (GLA_HEADS * c, GLA_QK), 0)
    laneb = lax.broadcasted_iota(jnp.int32, (GLA_HEADS * c, GLA_QK), 1)
    mask_bd = lax.shift_right_logical(rowb, log2(c)) == lax.shift_right_logical(laneb, log2(c))
    sblock = lax.shift_right_logical(lax.broadcasted_iota(jnp.int32, (c, GLA_QK), 0), log2(GLA_SUB))

    og, st = _gla_tile(qg, kg, vg, wn_all, r_all, stg_ref[...], mask_q, masks_p, mask_bd, sblock)
    stg_ref[...] = st
    ggla = ggla_ref[...]
    og_heads = []
    for h in range(GLA_HEADS):
        sl = slice(h * GLA_DV, (h + 1) * GLA_DV)
        o = og[:, sl]
        o = o * lax.rsqrt(jnp.mean(o * o, axis=-1, keepdims=True) + NORM_EPS) * ggla
        og_heads.append(o * _silu(rg[:, sl]))
    or_heads = []
    for h in hs:
        o = o_r[h]
        mu = jnp.mean(o, axis=-1, keepdims=True)
        oc = o - mu
        var = jnp.mean(oc * oc, axis=-1, keepdims=True)
        o = oc * lax.rsqrt(var + NORM_EPS) * gret_ref[:, hsl[h]] + bret_ref[:, hsl[h]]
        or_heads.append(o * _silu(gr_all[:, hsl[h]]))

    mixed = jnp.concatenate(og_heads + or_heads, axis=1).astype(BF16)
    half = tc // 2
    for lo_r in (0, half):
        rs = slice(lo_r, lo_r + half)
        y = _dot(mixed[rs], wout_ref[...])
        y_ref[rs, :] = _layer_norm(ALPHA * x[rs] + y, lng_ref[...], lnb_ref[...])

    @pl.when(j == nj - 1)
    def _():
        sg_ref[0] = stg_ref[...].T
        sr_ref[0] = str_ref[...]


def _mix_prompt(x, n_seq, seq_len, layer, li, win, wup, bgate, ggla, gret, bret, wout, lng, lnb, tables,
                to_cast=()):
    n, d = x.shape
    tc = min(MIX_TILE, seq_len)
    assert seq_len % tc == 0 and tc % GLA_CHUNK == 0
    nt = seq_len // tc
    cosf, sinf, mcat, dmask, qdec, kdec = tables
    c2 = lambda b, j: (0, 0)
    c3 = lambda b, j: (0, 0, 0)
    buffered = lambda spec: pl.BlockSpec(spec.block_shape, spec.index_map, pipeline_mode=pl.Buffered(1))
    n_steps = n_seq * nt
    flat = [w.reshape(-1, w.shape[-1]) for w in to_cast]
    for w in flat:
        assert w.shape[0] % (n_steps * 2 * SUBLANES) == 0, w.shape
    cast_specs = [pl.BlockSpec((w.shape[0] // n_steps, w.shape[1]), lambda b, j: (b * nt + j, 0)) for w in flat]
    y, sg, sr, *casted = pl.pallas_call(
        functools.partial(_mix_prompt_kernel, n_cast=len(flat)),
        grid=(n_seq, nt),
        in_specs=[
            pl.BlockSpec((tc, d), lambda b, j: (b * nt + j, 0)),
            buffered(_layer_spec(win.shape, li)),
            _layer_spec(wup.shape, li),
            _layer_spec(bgate.shape, li),
            _layer_spec(ggla.shape, li),
            _layer_spec(gret.shape, li),
            _layer_spec(bret.shape, li),
            buffered(_layer_spec(wout.shape, li)),
            _layer_spec(lng.shape, layer),
            _layer_spec(lnb.shape, layer),
            pl.BlockSpec((tc, RET_DK), lambda b, j: (j, 0)),
            pl.BlockSpec((tc, RET_DK), lambda b, j: (j, 0)),
            pl.BlockSpec((2 * tc, tc), c2),
            pl.BlockSpec((RET_HEADS, tc, tc), c3),
            pl.BlockSpec((RET_HEADS, tc, RET_DK), c3),
            pl.BlockSpec((RET_HEADS, tc, RET_DK), c3),
        ] + cast_specs,
        out_specs=[
            pl.BlockSpec((tc, d), lambda b, j: (b * nt + j, 0)),
            pl.BlockSpec((1, GLA_QK, GLA_DV), lambda b, j: (b, 0, 0)),
            pl.BlockSpec((1, RET_HEADS, RET_DK, RET_DV), lambda b, j: (b, 0, 0, 0)),
        ] + cast_specs,
        out_shape=[
            jax.ShapeDtypeStruct((n, d), F32),
            jax.ShapeDtypeStruct((n_seq, GLA_QK, GLA_DV), F32),
            jax.ShapeDtypeStruct((n_seq, RET_HEADS, RET_DK, RET_DV), F32),
        ] + [jax.ShapeDtypeStruct(w.shape, BF16) for w in flat],
        scratch_shapes=[
            pltpu.VMEM((GLA_DV, GLA_QK), F32),
            pltpu.VMEM((RET_HEADS, RET_DK, RET_DV), F32),
        ],
        compiler_params=pltpu.CompilerParams(
            dimension_semantics=("arbitrary", "arbitrary"), vmem_limit_bytes=VMEM_LIMIT),
        name="mix_prompt",
    )(x, win, wup, bgate, ggla, gret, bret, wout, lng, lnb, cosf, sinf, mcat, dmask, qdec, kdec, *flat)
    casted = [c.reshape(w.shape) for c, w in zip(casted, to_cast)]
    return y, sg.reshape(n_seq, GLA_HEADS, GLA_DK, GLA_DV), sr, casted


def _mix_tables(seq_len):
    tc = min(MIX_TILE, seq_len)
    pos = jnp.arange(seq_len, dtype=F32)
    cosf, sinf = _rotary_tables(pos)
    t = jnp.arange(tc)
    same_sub = (t[:, None] // GLA_SUB) == (t[None, :] // GLA_SUB)
    same_chunk = (t[:, None] // GLA_CHUNK) == (t[None, :] // GLA_CHUNK)
    m_within = same_sub & (t[None, :] <= t[:, None])
    m_before = same_chunk & ((t[None, :] // GLA_SUB) < (t[:, None] // GLA_SUB))
    mcat = jnp.concatenate([m_within, m_before], axis=0).astype(BF16)
    log_gamma = jnp.log1p(-jnp.exp2(-5.0 - jnp.arange(RET_HEADS, dtype=F32)))
    tf = t.astype(F32)
    diff = tf[:, None] - tf[None, :]
    dmask = jnp.where(diff >= 0, jnp.exp(diff[None] * log_gamma[:, None, None]), 0.0)
    qdec = jnp.exp((tf[None, :, None] + 1.0) * log_gamma[:, None, None])
    kdec = jnp.exp((tc - 1.0 - tf[None, :, None]) * log_gamma[:, None, None])
    qdec = jnp.broadcast_to(qdec, (RET_HEADS, tc, RET_DK))
    kdec = jnp.broadcast_to(kdec, (RET_HEADS, tc, RET_DK))
    return cosf, sinf, mcat, dmask, qdec, kdec


def _rotary_tables(pos):
    inv_freq = 1.0 / (ROPE_BASE ** (jnp.arange(0, RET_DK, 2, dtype=F32) / RET_DK))
    ang = pos[:, None] * inv_freq[None, :]
    cos, sin = jnp.cos(ang), jnp.sin(ang)
    return jnp.concatenate([cos, cos], axis=1), jnp.concatenate([-sin, sin], axis=1)


def _ffn_sample_kernel(x_ref, st_ref, win_ref, wdw_ref, bdw_ref, wout_ref, g_ref, b_ref,
                       y_ref, nst_ref):
    x = x_ref[...]
    u = _dot(x.astype(BF16), win_ref[...])
    wdw = wdw_ref[...]
    b1 = st_ref[:, 1, :]
    c = bdw_ref[...] + wdw[0:1] * st_ref[:, 0, :] + wdw[1:2] * b1 + wdw[2:3] * u
    h = _silu(c[:, D_FF:]) * c[:, :D_FF]
    y = _dot(h.astype(BF16), wout_ref[...])
    y_ref[...] = _layer_norm(ALPHA * x + y, g_ref[...], b_ref[...])
    nst_ref[:, 0, :] = b1
    nst_ref[:, 1, :] = u


def _sc_sample_kernel(x_ref, st_ref, win_ref, wdw_ref, bdw_ref, wout_ref, g_ref, b_ref,
                      y_ref, nst_ref):
    x = x_ref[...]
    d = x.shape[1]
    z = _dot(x.astype(BF16), win_ref[...])
    ch = z[:, d:2 * d] * z[:, 2 * d:]
    wdw = wdw_ref[...]
    b1 = st_ref[:, 1, :]
    conv = bdw_ref[...] + wdw[0:1] * st_ref[:, 0, :] + wdw[1:2] * b1 + wdw[2:3] * ch
    y = _dot((z[:, :d] * conv).astype(BF16), wout_ref[...])
    y_ref[...] = _layer_norm(ALPHA * x + y, g_ref[...], b_ref[...])
    nst_ref[:, 0, :] = b1
    nst_ref[:, 1, :] = ch


def _conv_sample_kernel(body, *refs):
    body(*refs[:8], *refs[9:])


def _conv_sample(body, name, x, states, new_states, layer, li, win, wdw, bdw, wout, g, b):
    s, d = x.shape
    whole = lambda a: pl.BlockSpec(a.shape, lambda i: (0,) * a.ndim)
    st_spec = _layer_spec(states.shape, li)
    return pl.pallas_call(
        functools.partial(_conv_sample_kernel, body),
        grid=(1,),
        in_specs=[whole(x), st_spec, _layer_spec(win.shape, li), _layer_spec(wdw.shape, li),
                  _layer_spec(bdw.shape, li), _layer_spec(wout.shape, li),
                  _layer_spec(g.shape, layer), _layer_spec(b.shape, layer),
                  pl.BlockSpec(memory_space=pl.ANY)],
        out_specs=[whole(x), st_spec],
        out_shape=[jax.ShapeDtypeStruct((s, d), F32), jax.ShapeDtypeStruct(states.shape, F32)],
        input_output_aliases={8: 1},
        compiler_params=pltpu.CompilerParams(
            dimension_semantics=("arbitrary",), vmem_limit_bytes=VMEM_LIMIT),
        name=name,
    )(x, states, win, wdw, bdw, wout, g, b, new_states)


def _mix_sample_proj_kernel(x_ref, win_ref, wup_ref, bgate_ref, cos_ref, sin_ref,
                            cols_ref, rows_ref, gates_ref):
    xb = x_ref[...].astype(BF16)

    def proj(off, width):
        return _dot(xb, win_ref[:, off:off + width])

    qg = proj(_OFF_QG, GLA_QK) * (GLA_DK ** -0.5)
    kg = proj(_OFF_KG, GLA_QK)
    lr = proj(_OFF_LR, PAD_RANK)
    gate = _dot(lr.astype(BF16), wup_ref[...]) + bgate_ref[...]
    ag = jnp.exp(_log_sigmoid(gate) * (1.0 / GLA_TAU))
    cosf = cos_ref[...]
    sinf = sin_ref[...]
    qr = [_rotary(proj(_OFF_QR + h * RET_DK, RET_DK), cosf, sinf) for h in range(RET_HEADS)]
    kr = [_rotary(proj(_OFF_KR + h * RET_DK, RET_DK), cosf, sinf) * (RET_DK ** -0.5)
          for h in range(RET_HEADS)]
    cols_ref[0:GLA_QK, :] = ag.T
    cols_ref[GLA_QK:2 * GLA_QK, :] = kg.T
    cols_ref[2 * GLA_QK:3 * GLA_QK, :] = qg.T
    base = 3 * GLA_QK
    for h in range(RET_HEADS):
        cols_ref[base + h * RET_DK:base + (h + 1) * RET_DK, :] = kr[h].T
        cols_ref[base + RET_QK + h * RET_DK:base + RET_QK + (h + 1) * RET_DK, :] = qr[h].T
    rows_ref[:, :GLA_V] = proj(_OFF_VG, GLA_V)
    rows_ref[:, GLA_V:] = proj(_OFF_VR, RET_V)
    gates_ref[:, :GLA_V] = proj(_OFF_RG, GLA_V)
    gates_ref[:, GLA_V:] = proj(_OFF_GR, RET_V)


_COL_ROWS = 3 * GLA_QK + 2 * RET_QK


def _mix_sample_state_kernel(*refs, n_stacked_in):
    cols_ref, rows_ref, sg_ref, sr_ref, gam_ref = refs[:5]
    nsg_ref, nsr_ref, o_ref = refs[5 + n_stacked_in:]
    nb = sg_ref.shape[0]
    s = cols_ref.shape[1]
    blk = pl.program_id(0)
    lane = lax.broadcasted_iota(jnp.int32, (1, s), 1)

    def body(bb, carry):
        b = blk * nb + bb
        onehot = (lane == b).astype(F32)
        col = jnp.sum(cols_ref[...] * onehot, axis=1, keepdims=True)
        vrow = rows_ref[b]
        a_g = col[0:GLA_QK]
        k_g = col[GLA_QK:2 * GLA_QK]
        q_g = col[2 * GLA_QK:3 * GLA_QK]
        base = 3 * GLA_QK
        outs = []
        for h in range(GLA_HEADS):
            rs = slice(h * GLA_DK, (h + 1) * GLA_DK)
            v_h = vrow[:, h * GLA_DV:(h + 1) * GLA_DV]
            s_new = a_g[rs] * sg_ref[bb, h] + k_g[rs] * v_h
            nsg_ref[bb, h] = s_new
            outs.append(jnp.sum(q_g[rs] * s_new, axis=0, keepdims=True))
        for h in range(RET_HEADS):
            rs = slice(h * RET_DK, (h + 1) * RET_DK)
            k_h = col[base + h * RET_DK:base + (h + 1) * RET_DK]
            q_h = col[base + RET_QK + h * RET_DK:base + RET_QK + (h + 1) * RET_DK]
            v_h = vrow[:, GLA_V + h * RET_DV:GLA_V + (h + 1) * RET_DV]
            s_new = gam_ref[h] * sr_ref[bb, h] + k_h * v_h
            nsr_ref[bb, h] = s_new
            outs.append(jnp.sum(q_h * s_new, axis=0, keepdims=True))
        o_ref[b] = jnp.concatenate(outs, axis=1)
        return carry

    lax.fori_loop(0, nb, body, 0)


def _mix_sample_out_kernel(x_ref, o_ref, gates_ref, ggla_ref, gret_ref, bret_ref, wout_ref,
                           lng_ref, lnb_ref, y_ref):
    x = x_ref[...]
    o = o_ref[...]
    gates = gates_ref[...]
    ggla = ggla_ref[...]
    heads = []
    for h in range(GLA_HEADS):
        sl = slice(h * GLA_DV, (h + 1) * GLA_DV)
        oh = o[:, sl]
        oh = oh * lax.rsqrt(jnp.mean(oh * oh, axis=-1, keepdims=True) + NORM_EPS) * ggla
        heads.append(oh * _silu(gates[:, sl]))
    for h in range(RET_HEADS):
        sl = slice(GLA_V + h * RET_DV, GLA_V + (h + 1) * RET_DV)
        ps = slice(h * RET_DV, (h + 1) * RET_DV)
        oh = o[:, sl]
        mu = jnp.mean(oh, axis=-1, keepdims=True)
        oc = oh - mu
        var = jnp.mean(oc * oc, axis=-1, keepdims=True)
        oh = oc * lax.rsqrt(var + NORM_EPS) * gret_ref[:, ps] + bret_ref[:, ps]
        heads.append(oh * _silu(gates[:, sl]))
    mixed = jnp.concatenate(heads, axis=1).astype(BF16)
    y = _dot(mixed, wout_ref[...])
    y_ref[...] = _layer_norm(ALPHA * x + y, lng_ref[...], lnb_ref[...])


def _mix_sample(x, state_gla, state_ret, layer, li, stacked, win, wup, bgate, ggla, gret, bret, wout, lng, lnb,
                cos_s, sin_s, gam):
    s, d = x.shape
    whole = lambda a: pl.BlockSpec(a.shape, lambda i: (0,) * a.ndim)
    proj_out = [
        jax.ShapeDtypeStruct((_COL_ROWS, s), F32),
        jax.ShapeDtypeStruct((s, GLA_V + RET_V), F32),
        jax.ShapeDtypeStruct((s, GLA_V + RET_V), F32),
    ]
    cols, rows, gates = pl.pallas_call(
        _mix_sample_proj_kernel,
        grid=(1,),
        in_specs=[whole(x), _layer_spec(win.shape, li), _layer_spec(wup.shape, li),
                  _layer_spec(bgate.shape, li), whole(cos_s), whole(sin_s)],
        out_specs=[whole(o) for o in proj_out],
        out_shape=proj_out,
        compiler_params=pltpu.CompilerParams(
            dimension_semantics=("arbitrary",), vmem_limit_bytes=VMEM_LIMIT),
        name="mix_sample_proj",
    )(x, win, wup, bgate, cos_s, sin_s)
    rows = rows.reshape(s, 1, GLA_V + RET_V)

    nb = SUBLANES
    assert s % nb == 0
    gla_spec = pl.BlockSpec((None, nb, GLA_HEADS, GLA_DK, GLA_DV), lambda i: (li, i, 0, 0, 0))
    ret_spec = pl.BlockSpec((None, nb, RET_HEADS, RET_DK, RET_DV), lambda i: (li, i, 0, 0, 0))
    stacked = tuple(stacked)
    nsg, nsr, o = pl.pallas_call(
        functools.partial(_mix_sample_state_kernel, n_stacked_in=len(stacked)),
        grid=(s // nb,),
        in_specs=[
            pl.BlockSpec((_COL_ROWS, s), lambda i: (0, 0)),
            pl.BlockSpec((s, 1, GLA_V + RET_V), lambda i: (0, 0, 0)),
            gla_spec,
            ret_spec,
            pl.BlockSpec((RET_HEADS, RET_DK, RET_DV), lambda i: (0, 0, 0)),
        ] + [pl.BlockSpec(memory_space=pl.ANY)] * len(stacked),
        out_specs=[
            gla_spec,
            ret_spec,
            pl.BlockSpec((s, 1, GLA_V + RET_V), lambda i: (0, 0, 0)),
        ],
        out_shape=[
            jax.ShapeDtypeStruct(state_gla.shape, F32),
            jax.ShapeDtypeStruct(state_ret.shape, F32),
            jax.ShapeDtypeStruct((s, 1, GLA_V + RET_V), F32),
        ],
        input_output_aliases={5 + k: k for k in range(len(stacked))},
        compiler_params=pltpu.CompilerParams(
            dimension_semantics=("arbitrary",), vmem_limit_bytes=VMEM_LIMIT),
        name="mix_sample_state",
    )(cols, rows, state_gla, state_ret, gam, *stacked)
    o = o.reshape(s, GLA_V + RET_V)

    y = pl.pallas_call(
        _mix_sample_out_kernel,
        grid=(1,),
        in_specs=[whole(x), whole(o), whole(gates), _layer_spec(ggla.shape, li), _layer_spec(gret.shape, li),
                  _layer_spec(bret.shape, li), _layer_spec(wout.shape, li),
                  _layer_spec(lng.shape, layer), _layer_spec(lnb.shape, layer)],
        out_specs=whole(x),
        out_shape=jax.ShapeDtypeStruct((s, d), F32),
        compiler_params=pltpu.CompilerParams(
            dimension_semantics=("arbitrary",), vmem_limit_bytes=VMEM_LIMIT),
        name="mix_sample_out",
    )(x, o, gates, ggla, gret, bret, wout, lng, lnb)
    return y, nsg, nsr


def _prep_mix_in(w):
    a = 2 * GLA_QK + GLA_V
    lr = jnp.pad(w[:, :, a:a + GLA_RANK], ((0, 0), (0, 0), (0, PAD_RANK - GLA_RANK)))
    return jnp.concatenate([w[:, :, :a], w[:, :, a + GLA_RANK:], lr], axis=2).astype(BF16)


def kernel(x_prompt, x_sample, state_gla, state_ret, state_conv, state_ffn,
           w_mix_in, w_gla_gate_up, b_gla_gate, g_gla_norm, g_ret_norm, b_ret_norm, w_mix_out,
           w_sc_in, w_sc_dw, b_sc_dw, w_sc_out, ln1_g, ln1_b,
           w_ffn_in, w_ffn_dw, b_ffn_dw, w_ffn_out, ln2_g, ln2_b):
    bp, tp, d = x_prompt.shape
    s, ts, _ = x_sample.shape
    assert ts == 1 and d == D_MODEL
    xp = x_prompt.reshape(bp * tp, d)
    xs = x_sample.reshape(s, d)

    tables = _mix_tables(tp)
    cos_s, sin_s = _rotary_tables(PAST_LEN + jnp.arange(ts, dtype=F32))
    log_gamma = jnp.log1p(-jnp.exp2(-5.0 - jnp.arange(RET_HEADS, dtype=F32)))
    gam = jnp.broadcast_to(jnp.exp(log_gamma)[:, None, None], (RET_HEADS, RET_DK, RET_DV))

    rows = lambda v: v[:, None, :]
    mix_w = (_prep_mix_in(w_mix_in),
             jnp.pad(w_gla_gate_up, ((0, 0), (0, PAD_RANK - GLA_RANK), (0, 0))).astype(BF16),
             rows(b_gla_gate), rows(g_gla_norm), rows(g_ret_norm), rows(b_ret_norm), w_mix_out.astype(BF16))
    later_w = (w_sc_in, w_sc_out, w_ffn_in, w_ffn_out)
    ln1 = (rows(ln1_g), rows(ln1_b))
    ln2 = (rows(ln2_g), rows(ln2_b))

    gla_p, ret_p, conv_p, ffn_p = [], [], [], []
    mix_s = (jnp.zeros_like(state_gla), jnp.zeros_like(state_ret))
    conv_s = jnp.zeros_like(state_conv)
    ffn_s = jnp.zeros_like(state_ffn)
    for layer in range(DEPTH):
        li = layer // 2
        if layer % 2 == 0:
            xp, sg, sr, casted = _mix_prompt(xp, bp, tp, layer, li, *mix_w, *ln1, tables,
                                             to_cast=later_w if layer == 0 else ())
            if layer == 0:
                sc_in, sc_out, ffn_in, ffn_out = casted
                sc_w = (sc_in, w_sc_dw, rows(b_sc_dw), sc_out)
                ffn_w = (ffn_in, w_ffn_dw, rows(b_ffn_dw), ffn_out)
            gla_p.append(sg)
            ret_p.append(sr)
            xs, *mix_s = _mix_sample(xs, state_gla, state_ret, layer, li, mix_s, *mix_w, *ln1,
                                     cos_s, sin_s, gam)
        else:
            xp, sc = _sc_prompt(xp, tp, layer, li, *sc_w, *ln1)
            conv_p.append(sc)
            xs, conv_s = _conv_sample(_sc_sample_kernel, "sc_sample", xs, state_conv, conv_s,
                                      layer, li, *sc_w, *ln1)
        xp, sf = _ffn_prompt(xp, tp, layer, *ffn_w, *ln2)
        ffn_p.append(sf)
        xs, ffn_s = _conv_sample(_ffn_sample_kernel, "ffn_sample", xs, state_ffn, ffn_s,
                                 layer, layer, *ffn_w, *ln2)
    gla_s, ret_s = mix_s

    return (xp.reshape(bp, tp, d), xs.reshape(s, ts, d),
            jnp.stack(gla_p), gla_s, jnp.stack(ret_p), ret_s,
            jnp.stack(conv_p), conv_s, jnp.stack(ffn_p), ffn_s)
```

```python
import functools

import jax
import jax.numpy as jnp
from jax import lax
from jax.experimental import pallas as pl
from jax.experimental.pallas import tpu as pltpu

F32 = jnp.float32
BF16 = jnp.bfloat16

D_MODEL = 1024
DEPTH = 4
PAST_LEN = 16384
GLA_HEADS = 4
GLA_DK = 64
GLA_DV = 128
GLA_RANK = 16
GLA_TAU = 16.0
RET_HEADS = 4
RET_DK = 128
RET_DV = 128
CONV_W = 3
D_FF = D_MODEL * 11 // 4
ROPE_BASE = 10000.0
NORM_EPS = 1e-5
ALPHA = (2 * DEPTH) ** 0.25
GLA_QK = GLA_HEADS * GLA_DK
GLA_V = GLA_HEADS * GLA_DV
RET_QK = RET_HEADS * RET_DK
RET_V = RET_HEADS * RET_DV

LANES = 128
SUBLANES = 8
VMEM_LIMIT = 56 * 1024 * 1024

GLA_CHUNK = 64
GLA_SUB = 32
MIX_TILE = 256
ROW_TILE = 512
CONV_ROWS = 32
FF_CHUNK = 768
FF_FIRST_CHUNK = 512
SC_CHUNK = 256
PAD_RANK = LANES

_OFF_QG = 0
_OFF_KG = _OFF_QG + GLA_QK
_OFF_VG = _OFF_KG + GLA_QK
_OFF_RG = _OFF_VG + GLA_V
_OFF_QR = _OFF_RG + GLA_V
_OFF_KR = _OFF_QR + RET_QK
_OFF_VR = _OFF_KR + RET_QK
_OFF_GR = _OFF_VR + RET_V
_OFF_LR = _OFF_GR + RET_V
MIX_COLS = _OFF_LR + PAD_RANK


def _dot(a, b):
    return jnp.dot(a, b, preferred_element_type=F32)


def _dot_nt(a, b):
    return lax.dot_general(a, b, (((1,), (1,)), ((), ())), preferred_element_type=F32)


def _layer_norm(x, g, b):
    mu = jnp.mean(x, axis=-1, keepdims=True)
    xc = x - mu
    var = jnp.mean(xc * xc, axis=-1, keepdims=True)
    return xc * lax.rsqrt(var + NORM_EPS) * g + b


def _silu(x):
    return x * jax.nn.sigmoid(x)


def _log_sigmoid(x):
    return jnp.minimum(x, 0.0) - jnp.log1p(jnp.exp(-jnp.abs(x)))


def _split3(x):
    hi = x.astype(BF16)
    r1 = x - hi.astype(F32)
    mid = r1.astype(BF16)
    lo = (r1 - mid.astype(F32)).astype(BF16)
    return hi, mid, lo


def _causal_conv3_blocks(rows, n_rows, prev, w, b, post):
    w0, w1, w2 = w[0:1], w[1:2], w[2:3]
    out = []
    for lo in range(0, n_rows, CONV_ROWS):
        head = prev if lo == 0 else rows(lo - SUBLANES, lo)
        s = jnp.concatenate([head, rows(lo, lo + CONV_ROWS)], axis=0)
        s1 = pltpu.roll(s, 1, 0)[SUBLANES:]
        s2 = pltpu.roll(s, 2, 0)[SUBLANES:]
        out.append(post(lo, b + w0 * s2 + w1 * s1 + w2 * s[SUBLANES:]))
    return jnp.concatenate(out, axis=0)


def _ffn_prompt_kernel(x_ref, win_ref, wdw_ref, bdw_ref, wout_ref, g_ref, b_ref,
                       y_ref, st_ref, xb_ref, carry_ref, *, tiles_per_seq):
    i = pl.program_id(0)

    @pl.when(i % tiles_per_seq == 0)
    def _():
        carry_ref[...] = jnp.zeros_like(carry_ref)

    tm = x_ref.shape[0]
    ff = wout_ref.shape[0]
    first = min(FF_FIRST_CHUNK, ff)
    bounds = [0] + list(range(first, ff, FF_CHUNK)) + [ff]
    n_chunks = len(bounds) - 1
    xb_ref[...] = x_ref[...].astype(BF16)

    def cols(ref, f):
        lo, hi = bounds[f], bounds[f + 1]
        return jnp.concatenate([ref[:, lo:hi], ref[:, ff + lo:ff + hi]], axis=1)

    def up(f):
        lo, hi = bounds[f], bounds[f + 1]
        xb = xb_ref[...]
        return jnp.concatenate([_dot(xb, win_ref[:, lo:hi]), _dot(xb, win_ref[:, ff + lo:ff + hi])], axis=1)

    def act(f, u):
        lo, hi = bounds[f], bounds[f + 1]
        fc = hi - lo
        h = _causal_conv3_blocks(
            lambda r0, r1: u[r0:r1], tm, cols(carry_ref, f), cols(wdw_ref, f), cols(bdw_ref, f),
            lambda r0, c: (_silu(c[:, fc:]) * c[:, :fc]).astype(BF16))
        for k in range(2):
            tail = u[tm - SUBLANES:, k * fc:(k + 1) * fc]
            carry_ref[:, k * ff + lo:k * ff + hi] = tail
            st_ref[0, :, k * ff + lo:k * ff + hi] = tail
        return h

    def down(f, h):
        return _dot(h, wout_ref[bounds[f]:bounds[f + 1], :])

    u = up(0)
    h_prev = None
    acc = None
    for f in range(n_chunks):
        u_next = up(f + 1) if f + 1 < n_chunks else None
        h = act(f, u)
        if h_prev is not None:
            p = down(f - 1, h_prev)
            acc = p if acc is None else acc + p
        u, h_prev = u_next, h
    half = tm // 2
    w_last = wout_ref[bounds[n_chunks - 1]:bounds[n_chunks], :]
    for lo_r in (0, half):
        rs = slice(lo_r, lo_r + half)
        tot = acc[rs] + _dot(h_prev[rs], w_last)
        y_ref[rs, :] = _layer_norm(ALPHA * x_ref[rs, :] + tot, g_ref[...], b_ref[...])


def _layer_spec(shape, layer):
    zeros = (0,) * (len(shape) - 1)
    return pl.BlockSpec((None,) + tuple(shape[1:]), lambda *_: (layer,) + zeros)


def _ffn_prompt(x, seq_len, layer, win, wdw, bdw, wout, g, b):
    n, d = x.shape
    tm = min(ROW_TILE, seq_len)
    assert seq_len % tm == 0 and n % seq_len == 0
    n_seq = n // seq_len
    tps = seq_len // tm
    ff2 = win.shape[2]
    buffered = lambda spec: pl.BlockSpec(spec.block_shape, spec.index_map, pipeline_mode=pl.Buffered(1))
    y, st = pl.pallas_call(
        functools.partial(_ffn_prompt_kernel, tiles_per_seq=tps),
        grid=(n // tm,),
        in_specs=[
            pl.BlockSpec((tm, d), lambda i: (i, 0)),
            buffered(_layer_spec(win.shape, layer)),
            _layer_spec(wdw.shape, layer),
            _layer_spec(bdw.shape, layer),
            buffered(_layer_spec(wout.shape, layer)),
            _layer_spec(g.shape, layer),
            _layer_spec(b.shape, layer),
        ],
        out_specs=[
            pl.BlockSpec((tm, d), lambda i: (i, 0)),
            pl.BlockSpec((1, SUBLANES, ff2), lambda i: (i // tps, 0, 0)),
        ],
        out_shape=[
            jax.ShapeDtypeStruct((n, d), F32),
            jax.ShapeDtypeStruct((n_seq, SUBLANES, ff2), F32),
        ],
        scratch_shapes=[
            pltpu.VMEM((tm, d), BF16),
            pltpu.VMEM((SUBLANES, ff2), F32),
        ],
        compiler_params=pltpu.CompilerParams(
            dimension_semantics=("arbitrary",), vmem_limit_bytes=VMEM_LIMIT),
        name="ffn_prompt",
    )(x, win, wdw, bdw, wout, g, b)
    return y, st[:, SUBLANES - 2:, :]


def _sc_prompt_kernel(x_ref, win_ref, wdw_ref, bdw_ref, wout_ref, g_ref, b_ref,
                      y_ref, st_ref, xb_ref, carry_ref, *, tiles_per_seq):
    i = pl.program_id(0)

    @pl.when(i % tiles_per_seq == 0)
    def _():
        carry_ref[...] = jnp.zeros_like(carry_ref)

    tm, d = x_ref.shape
    cw = SC_CHUNK
    n_chunks = d // cw
    xb_ref[...] = x_ref[...].astype(BF16)

    def up(f):
        xb = xb_ref[...]
        return jnp.concatenate([_dot(xb, win_ref[:, k * d + f * cw:k * d + (f + 1) * cw]) for k in range(3)],
                               axis=1)

    def act(f, z):
        cs = slice(f * cw, (f + 1) * cw)
        ch_rows = lambda lo, hi: z[lo:hi, cw:2 * cw] * z[lo:hi, 2 * cw:]
        m = _causal_conv3_blocks(
            ch_rows, tm, carry_ref[:, cs], wdw_ref[:, cs], bdw_ref[:, cs],
            lambda lo, conv: (z[lo:lo + CONV_ROWS, :cw] * conv).astype(BF16))
        tail = ch_rows(tm - SUBLANES, tm)
        carry_ref[:, cs] = tail
        st_ref[0, :, cs] = tail
        return m

    def down(f, m):
        return _dot(m, wout_ref[f * cw:(f + 1) * cw, :])

    z = up(0)
    m_prev = None
    acc = None
    for f in range(n_chunks):
        z_next = up(f + 1) if f + 1 < n_chunks else None
        m = act(f, z)
        if m_prev is not None:
            p = down(f - 1, m_prev)
            acc = p if acc is None else acc + p
        z, m_prev = z_next, m
    half = tm // 2
    w_last = wout_ref[(n_chunks - 1) * cw:n_chunks * cw, :]
    for lo_r in (0, half):
        rs = slice(lo_r, lo_r + half)
        tot = acc[rs] + _dot(m_prev[rs], w_last)
        y_ref[rs, :] = _layer_norm(ALPHA * x_ref[rs, :] + tot, g_ref[...], b_ref[...])


def _sc_prompt(x, seq_len, layer, li, win, wdw, bdw, wout, g, b):
    n, d = x.shape
    tm = min(ROW_TILE, seq_len)
    assert seq_len % tm == 0 and n % seq_len == 0
    n_seq = n // seq_len
    tps = seq_len // tm
    buffered = lambda spec: pl.BlockSpec(spec.block_shape, spec.index_map, pipeline_mode=pl.Buffered(1))
    y, st = pl.pallas_call(
        functools.partial(_sc_prompt_kernel, tiles_per_seq=tps),
        grid=(n // tm,),
        in_specs=[
            pl.BlockSpec((tm, d), lambda i: (i, 0)),
            buffered(_layer_spec(win.shape, li)),
            _layer_spec(wdw.shape, li),
            _layer_spec(bdw.shape, li),
            buffered(_layer_spec(wout.shape, li)),
            _layer_spec(g.shape, layer),
            _layer_spec(b.shape, layer),
        ],
        out_specs=[
            pl.BlockSpec((tm, d), lambda i: (i, 0)),
            pl.BlockSpec((1, SUBLANES, d), lambda i: (i // tps, 0, 0)),
        ],
        out_shape=[
            jax.ShapeDtypeStruct((n, d), F32),
            jax.ShapeDtypeStruct((n_seq, SUBLANES, d), F32),
        ],
        scratch_shapes=[
            pltpu.VMEM((tm, d), BF16),
            pltpu.VMEM((SUBLANES, d), F32),
        ],
        compiler_params=pltpu.CompilerParams(
            dimension_semantics=("arbitrary",), vmem_limit_bytes=VMEM_LIMIT),
        name="sc_prompt",
    )(x, win, wdw, bdw, wout, g, b)
    return y, st[:, SUBLANES - 2:, :]


def _rotary(x, cosf, sinf):
    return x * cosf + pltpu.roll(x, RET_DK // 2, 1) * sinf


def _gla_tile(qg, kg, vg, wn_all, r_all, st, mask_q, masks_p, mask_bd, sblock):
    c = GLA_CHUNK
    n_chunks = qg.shape[0] // c
    n_sub = c // GLA_SUB
    pre = []
    for ci in range(n_chunks):
        sl = slice(ci * c, (ci + 1) * c)
        q_c, k_c, v_c, wn, r = qg[sl], kg[sl], vg[sl], wn_all[sl], r_all[sl]
        cum = r + wn
        tot = cum[c - 1:c, :]
        vst = jnp.concatenate([v_c[:, h * GLA_DV:(h + 1) * GLA_DV] for h in range(GLA_HEADS)], axis=0)
        pre.append(dict(q=q_c, k=k_c, r=r, cum=cum, tot=tot, qw=q_c * jnp.exp(wn), vst=vst,
                        vst_b=vst.astype(BF16)))
    scores = []
    for p in pre:
        row = []
        for i in range(n_sub):
            r_i = p["r"][i * GLA_SUB:i * GLA_SUB + 1, :]
            kt = jnp.where(sblock <= i, p["k"] * jnp.exp(r_i - p["cum"]), 0.0)
            kbig = jnp.concatenate([kt] * GLA_HEADS, axis=0).astype(BF16)
            q16 = p["qw"][i * GLA_SUB:(i + 1) * GLA_SUB]
            qs = jnp.where(mask_q, jnp.concatenate([q16] * GLA_HEADS, axis=0), 0.0).astype(BF16)
            row.append(_dot_nt(qs, kbig))
        scores.append(row)
    intra = [[_dot(jnp.where(masks_p[i], scores[ci][i], 0.0).astype(BF16), p["vst_b"])
              for i in range(n_sub)] for ci, p in enumerate(pre)]
    upds, qcss = [], []
    for p in pre:
        kd = p["k"] * jnp.exp(p["tot"] - p["cum"])
        kdbig = jnp.where(mask_bd, jnp.concatenate([kd] * GLA_HEADS, axis=0), 0.0).astype(BF16)
        upds.append(_dot(p["vst"].T.astype(BF16), kdbig))
        qc = p["q"] * jnp.exp(p["cum"])
        qcss.append(jnp.where(mask_bd, jnp.concatenate([qc] * GLA_HEADS, axis=0), 0.0).astype(BF16))
    inters = []
    for ci, p in enumerate(pre):
        inters.append(_dot_nt(qcss[ci], st.astype(BF16)))
        st = jnp.exp(p["tot"]) * st + upds[ci]
    chunks = []
    for ci in range(n_chunks):
        heads = []
        for h in range(GLA_HEADS):
            intra_h = jnp.concatenate([intra[ci][i][h * GLA_SUB:(h + 1) * GLA_SUB] for i in range(n_sub)], axis=0)
            heads.append(intra_h + inters[ci][h * c:(h + 1) * c])
        chunks.append(jnp.concatenate(heads, axis=1))
    return jnp.concatenate(chunks, axis=0), st


_MIX_N_IN = 16


def _mix_prompt_kernel(*refs, n_cast, n_zero):
    (x_ref, win_ref, wup_ref, bgate_ref, ggla_ref, gret_ref, bret_ref, wout_ref,
     lng_ref, lnb_ref, cos_ref, sin_ref, mcat_ref, dmask_ref, qdec_ref, kdec_ref) = refs[:_MIX_N_IN]
    cast_in = refs[_MIX_N_IN:_MIX_N_IN + n_cast]
    n_in = _MIX_N_IN + n_cast
    y_ref, sg_ref, sr_ref = refs[n_in:n_in + 3]
    cast_out = refs[n_in + 3:n_in + 3 + n_cast]
    zero_out = refs[n_in + 3 + n_cast:n_in + 3 + n_cast + n_zero]
    stg_ref, str_ref = refs[n_in + 3 + n_cast + n_zero:]
    for src, dst in zip(cast_in, cast_out):
        dst[...] = src[...].astype(BF16)
    for dst in zero_out:
        dst[...] = jnp.zeros_like(dst)
    j = pl.program_id(1)
    nj = pl.num_programs(1)

    @pl.when(j == 0)
    def _():
        stg_ref[...] = jnp.zeros_like(stg_ref)
        str_ref[...] = jnp.zeros_like(str_ref)

    tc = x_ref.shape[0]
    x = x_ref[...]
    xb = x.astype(BF16)

    def proj(off, width):
        return _dot(xb, win_ref[:, off:off + width])

    lr = proj(_OFF_LR, PAD_RANK)
    gate = _dot(lr.astype(BF16), wup_ref[...]) + bgate_ref[...]
    lg = _log_sigmoid(gate) * (1.0 / GLA_TAU)
    hi, mid, lo = _split3(lg)
    qr_all = proj(_OFF_QR, RET_QK)
    kr_all = proj(_OFF_KR, RET_QK)
    vr_all = proj(_OFF_VR, RET_V)
    qg = proj(_OFF_QG, GLA_QK) * (GLA_DK ** -0.5)
    kg = proj(_OFF_KG, GLA_QK)
    vg = proj(_OFF_VG, GLA_V)
    cs = _dot(mcat_ref[...], jnp.concatenate([hi, mid, lo], axis=1))
    cs = cs[:, :GLA_QK] + cs[:, GLA_QK:2 * GLA_QK] + cs[:, 2 * GLA_QK:]
    wn_all = cs[:tc]
    r_all = cs[tc:]

    cosf = cos_ref[...]
    sinf = sin_ref[...]
    hs = range(RET_HEADS)
    hsl = [slice(h * RET_DK, (h + 1) * RET_DK) for h in hs]
    qhs = [_rotary(qr_all[:, hsl[h]], cosf, sinf) for h in hs]
    khs = [_rotary(kr_all[:, hsl[h]], cosf, sinf) * (RET_DK ** -0.5) for h in hs]
    vhs = [vr_all[:, hsl[h]].astype(BF16) for h in hs]
    scs = [_dot_nt(qhs[h].astype(BF16), khs[h].astype(BF16)) for h in hs]
    s_hs = [str_ref[h] for h in hs]
    inter_r = [_dot((qhs[h] * qdec_ref[h]).astype(BF16), s_hs[h].astype(BF16)) for h in hs]
    upd_r = [_dot((khs[h] * kdec_ref[h]).T.astype(BF16), vhs[h]) for h in hs]
    o_r = [_dot((scs[h] * dmask_ref[h]).astype(BF16), vhs[h]) + inter_r[h] for h in hs]
    for h in hs:
        str_ref[h] = qdec_ref[h][tc - 1:tc, :] * s_hs[h] + upd_r[h]
    gr_all = proj(_OFF_GR, RET_V)
    rg = proj(_OFF_RG, GLA_V)


    c = GLA_CHUNK
    assert c == GLA_DK
    hq = GLA_HEADS * GLA_SUB
    log2 = lambda n: n.bit_length() - 1
    row64 = lax.broadcasted_iota(jnp.int32, (hq, GLA_QK), 0)
    lane64 = lax.broadcasted_iota(jnp.int32, (hq, GLA_QK), 1)
    lane_head = lax.shift_right_logical(lane64, log2(c))
    row_head = lax.shift_right_logical(row64, log2(GLA_SUB))
    mask_q = lane_head == row_head
    masks_p = [mask_q & ((lane64 & (c - 1)) <= (row64 & (GLA_SUB - 1)) + i * GLA_SUB)
               for i in range(c // GLA_SUB)]
    rowb = lax.broadcasted_iota(jnp.int32, (GLA_HEADS * c, GLA_QK), 0)
    laneb = lax.broadcasted_iota(jnp.int32, (GLA_HEADS * c, GLA_QK), 1)
    mask_bd = lax.shift_right_logical(rowb, log2(c)) == lax.shift_right_logical(laneb, log2(c))
    sblock = lax.shift_right_logical(lax.broadcasted_iota(jnp.int32, (c, GLA_QK), 0), log2(GLA_SUB))

    og, st = _gla_tile(qg, kg, vg, wn_all, r_all, stg_ref[...], mask_q, masks_p, mask_bd, sblock)
    stg_ref[...] = st
    ggla = ggla_ref[...]
    og_heads = []
    for h in range(GLA_HEADS):
        sl = slice(h * GLA_DV, (h + 1) * GLA_DV)
        o = og[:, sl]
        o = o * lax.rsqrt(jnp.mean(o * o, axis=-1, keepdims=True) + NORM_EPS) * ggla
        og_heads.append(o * _silu(rg[:, sl]))
    or_heads = []
    for h in hs:
        o = o_r[h]
        mu = jnp.mean(o, axis=-1, keepdims=True)
        oc = o - mu
        var = jnp.mean(oc * oc, axis=-1, keepdims=True)
        o = oc * lax.rsqrt(var + NORM_EPS) * gret_ref[:, hsl[h]] + bret_ref[:, hsl[h]]
        or_heads.append(o * _silu(gr_all[:, hsl[h]]))

    mixed = jnp.concatenate(og_heads + or_heads, axis=1).astype(BF16)
    half = tc // 2
    for lo_r in (0, half):
        rs = slice(lo_r, lo_r + half)
        y = _dot(mixed[rs], wout_ref[...])
        y_ref[rs, :] = _layer_norm(ALPHA * x[rs] + y, lng_ref[...], lnb_ref[...])

    @pl.when(j == nj - 1)
    def _():
        sg_ref[0] = stg_ref[...].T
        sr_ref[0] = str_ref[...]


def _mix_prompt(x, n_seq, seq_len, layer, li, win, wup, bgate, ggla, gret, bret, wout, lng, lnb, tables,
                to_cast=(), zeros_like=()):
    n, d = x.shape
    tc = min(MIX_TILE, seq_len)
    assert seq_len % tc == 0 and tc % GLA_CHUNK == 0
    nt = seq_len // tc
    cosf, sinf, mcat, dmask, qdec, kdec = tables
    c2 = lambda b, j: (0, 0)
    c3 = lambda b, j: (0, 0, 0)
    buffered = lambda spec: pl.BlockSpec(spec.block_shape, spec.index_map, pipeline_mode=pl.Buffered(1))
    n_steps = n_seq * nt
    flat = [w.reshape(-1, w.shape[-1]) for w in to_cast]
    for w in flat:
        assert w.shape[0] % (n_steps * 2 * SUBLANES) == 0, w.shape
    row_block = lambda w: pl.BlockSpec((w.shape[0] // n_steps, w.shape[1]), lambda b, j: (b * nt + j, 0))
    cast_specs = [row_block(w) for w in flat]
    zflat = [jax.ShapeDtypeStruct((z.size // z.shape[-1], z.shape[-1]), F32) for z in zeros_like]
    for z in zflat:
        assert z.shape[0] % (n_steps * SUBLANES) == 0, z.shape
    zero_specs = [row_block(z) for z in zflat]
    y, sg, sr, *side = pl.pallas_call(
        functools.partial(_mix_prompt_kernel, n_cast=len(flat), n_zero=len(zflat)),
        grid=(n_seq, nt),
        in_specs=[
            pl.BlockSpec((tc, d), lambda b, j: (b * nt + j, 0)),
            buffered(_layer_spec(win.shape, li)),
            _layer_spec(wup.shape, li),
            _layer_spec(bgate.shape, li),
            _layer_spec(ggla.shape, li),
            _layer_spec(gret.shape, li),
            _layer_spec(bret.shape, li),
            buffered(_layer_spec(wout.shape, li)),
            _layer_spec(lng.shape, layer),
            _layer_spec(lnb.shape, layer),
            pl.BlockSpec((tc, RET_DK), lambda b, j: (j, 0)),
            pl.BlockSpec((tc, RET_DK), lambda b, j: (j, 0)),
            pl.BlockSpec((2 * tc, tc), c2),
            pl.BlockSpec((RET_HEADS, tc, tc), c3),
            pl.BlockSpec((RET_HEADS, tc, RET_DK), c3),
            pl.BlockSpec((RET_HEADS, tc, RET_DK), c3),
        ] + cast_specs,
        out_specs=[
            pl.BlockSpec((tc, d), lambda b, j: (b * nt + j, 0)),
            pl.BlockSpec((1, GLA_QK, GLA_DV), lambda b, j: (b, 0, 0)),
            pl.BlockSpec((1, RET_HEADS, RET_DK, RET_DV), lambda b, j: (b, 0, 0, 0)),
        ] + cast_specs + zero_specs,
        out_shape=[
            jax.ShapeDtypeStruct((n, d), F32),
            jax.ShapeDtypeStruct((n_seq, GLA_QK, GLA_DV), F32),
            jax.ShapeDtypeStruct((n_seq, RET_HEADS, RET_DK, RET_DV), F32),
        ] + [jax.ShapeDtypeStruct(w.shape, BF16) for w in flat] + zflat,
        scratch_shapes=[
            pltpu.VMEM((GLA_DV, GLA_QK), F32),
            pltpu.VMEM((RET_HEADS, RET_DK, RET_DV), F32),
        ],
        compiler_params=pltpu.CompilerParams(
            dimension_semantics=("arbitrary", "arbitrary"), vmem_limit_bytes=VMEM_LIMIT),
        name="mix_prompt",
    )(x, win, wup, bgate, ggla, gret, bret, wout, lng, lnb, cosf, sinf, mcat, dmask, qdec, kdec, *flat)
    casted = [c.reshape(w.shape) for c, w in zip(side, to_cast)]
    zeros = [z.reshape(a.shape) for z, a in zip(side[len(flat):], zeros_like)]
    return y, sg.reshape(n_seq, GLA_HEADS, GLA_DK, GLA_DV), sr, casted, zeros


def _mix_tables(seq_len):
    tc = min(MIX_TILE, seq_len)
    pos = jnp.arange(seq_len, dtype=F32)
    cosf, sinf = _rotary_tables(pos)
    t = jnp.arange(tc)
    same_sub = (t[:, None] // GLA_SUB) == (t[None, :] // GLA_SUB)
    same_chunk = (t[:, None] // GLA_CHUNK) == (t[None, :] // GLA_CHUNK)
    m_within = same_sub & (t[None, :] <= t[:, None])
    m_before = same_chunk & ((t[None, :] // GLA_SUB) < (t[:, None] // GLA_SUB))
    mcat = jnp.concatenate([m_within, m_before], axis=0).astype(BF16)
    log_gamma = jnp.log1p(-jnp.exp2(-5.0 - jnp.arange(RET_HEADS, dtype=F32)))
    tf = t.astype(F32)
    diff = tf[:, None] - tf[None, :]
    dmask = jnp.where(diff >= 0, jnp.exp(diff[None] * log_gamma[:, None, None]), 0.0)
    qdec = jnp.exp((tf[None, :, None] + 1.0) * log_gamma[:, None, None])
    kdec = jnp.exp((tc - 1.0 - tf[None, :, None]) * log_gamma[:, None, None])
    qdec = jnp.broadcast_to(qdec, (RET_HEADS, tc, RET_DK))
    kdec = jnp.broadcast_to(kdec, (RET_HEADS, tc, RET_DK))
    return cosf, sinf, mcat, dmask, qdec, kdec


def _rotary_tables(pos):
    inv_freq = 1.0 / (ROPE_BASE ** (jnp.arange(0, RET_DK, 2, dtype=F32) / RET_DK))
    ang = pos[:, None] * inv_freq[None, :]
    cos, sin = jnp.cos(ang), jnp.sin(ang)
    return jnp.concatenate([cos, cos], axis=1), jnp.concatenate([-sin, sin], axis=1)


def _ffn_sample_kernel(x_ref, st_ref, win_ref, wdw_ref, bdw_ref, wout_ref, g_ref, b_ref,
                       y_ref, nst_ref):
    x = x_ref[...]
    u = _dot(x.astype(BF16), win_ref[...])
    wdw = wdw_ref[...]
    b1 = st_ref[:, 1, :]
    c = bdw_ref[...] + wdw[0:1] * st_ref[:, 0, :] + wdw[1:2] * b1 + wdw[2:3] * u
    h = _silu(c[:, D_FF:]) * c[:, :D_FF]
    y = _dot(h.astype(BF16), wout_ref[...])
    y_ref[...] = _layer_norm(ALPHA * x + y, g_ref[...], b_ref[...])
    nst_ref[:, 0, :] = b1
    nst_ref[:, 1, :] = u


def _sc_sample_kernel(x_ref, st_ref, win_ref, wdw_ref, bdw_ref, wout_ref, g_ref, b_ref,
                      y_ref, nst_ref):
    x = x_ref[...]
    d = x.shape[1]
    z = _dot(x.astype(BF16), win_ref[...])
    ch = z[:, d:2 * d] * z[:, 2 * d:]
    wdw = wdw_ref[...]
    b1 = st_ref[:, 1, :]
    conv = bdw_ref[...] + wdw[0:1] * st_ref[:, 0, :] + wdw[1:2] * b1 + wdw[2:3] * ch
    y = _dot((z[:, :d] * conv).astype(BF16), wout_ref[...])
    y_ref[...] = _layer_norm(ALPHA * x + y, g_ref[...], b_ref[...])
    nst_ref[:, 0, :] = b1
    nst_ref[:, 1, :] = ch


def _conv_sample_kernel(body, *refs):
    body(*refs[:8], *refs[9:])


def _conv_sample(body, name, x, states, new_states, layer, li, win, wdw, bdw, wout, g, b):
    s, d = x.shape
    whole = lambda a: pl.BlockSpec(a.shape, lambda i: (0,) * a.ndim)
    st_spec = _layer_spec(states.shape, li)
    return pl.pallas_call(
        functools.partial(_conv_sample_kernel, body),
        grid=(1,),
        in_specs=[whole(x), st_spec, _layer_spec(win.shape, li), _layer_spec(wdw.shape, li),
                  _layer_spec(bdw.shape, li), _layer_spec(wout.shape, li),
                  _layer_spec(g.shape, layer), _layer_spec(b.shape, layer),
                  pl.BlockSpec(memory_space=pl.ANY)],
        out_specs=[whole(x), st_spec],
        out_shape=[jax.ShapeDtypeStruct((s, d), F32), jax.ShapeDtypeStruct(states.shape, F32)],
        input_output_aliases={8: 1},
        compiler_params=pltpu.CompilerParams(
            dimension_semantics=("arbitrary",), vmem_limit_bytes=VMEM_LIMIT),
        name=name,
    )(x, states, win, wdw, bdw, wout, g, b, new_states)


def _mix_sample_proj_kernel(x_ref, win_ref, wup_ref, bgate_ref, cos_ref, sin_ref,
                            cols_ref, rows_ref, gates_ref):
    xb = x_ref[...].astype(BF16)

    def proj(off, width):
        return _dot(xb, win_ref[:, off:off + width])

    qg = proj(_OFF_QG, GLA_QK) * (GLA_DK ** -0.5)
    kg = proj(_OFF_KG, GLA_QK)
    lr = proj(_OFF_LR, PAD_RANK)
    gate = _dot(lr.astype(BF16), wup_ref[...]) + bgate_ref[...]
    ag = jnp.exp(_log_sigmoid(gate) * (1.0 / GLA_TAU))
    cosf = cos_ref[...]
    sinf = sin_ref[...]
    qr = [_rotary(proj(_OFF_QR + h * RET_DK, RET_DK), cosf, sinf) for h in range(RET_HEADS)]
    kr = [_rotary(proj(_OFF_KR + h * RET_DK, RET_DK), cosf, sinf) * (RET_DK ** -0.5)
          for h in range(RET_HEADS)]
    cols_ref[0:GLA_QK, :] = ag.T
    cols_ref[GLA_QK:2 * GLA_QK, :] = kg.T
    cols_ref[2 * GLA_QK:3 * GLA_QK, :] = qg.T
    base = 3 * GLA_QK
    for h in range(RET_HEADS):
        cols_ref[base + h * RET_DK:base + (h + 1) * RET_DK, :] = kr[h].T
        cols_ref[base + RET_QK + h * RET_DK:base + RET_QK + (h + 1) * RET_DK, :] = qr[h].T
    rows_ref[:, :GLA_V] = proj(_OFF_VG, GLA_V)
    rows_ref[:, GLA_V:] = proj(_OFF_VR, RET_V)
    gates_ref[:, :GLA_V] = proj(_OFF_RG, GLA_V)
    gates_ref[:, GLA_V:] = proj(_OFF_GR, RET_V)


_COL_ROWS = 3 * GLA_QK + 2 * RET_QK


def _mix_sample_state_kernel(*refs, n_stacked_in):
    cols_ref, rows_ref, sg_ref, sr_ref, gam_ref = refs[:5]
    nsg_ref, nsr_ref, o_ref = refs[5 + n_stacked_in:]
    nb = sg_ref.shape[0]
    s = cols_ref.shape[1]
    blk = pl.program_id(0)
    lane = lax.broadcasted_iota(jnp.int32, (1, s), 1)

    def body(bb, carry):
        b = blk * nb + bb
        onehot = (lane == b).astype(F32)
        col = jnp.sum(cols_ref[...] * onehot, axis=1, keepdims=True)
        vrow = rows_ref[b]
        a_g = col[0:GLA_QK]
        k_g = col[GLA_QK:2 * GLA_QK]
        q_g = col[2 * GLA_QK:3 * GLA_QK]
        base = 3 * GLA_QK
        outs = []
        for h in range(GLA_HEADS):
            rs = slice(h * GLA_DK, (h + 1) * GLA_DK)
            v_h = vrow[:, h * GLA_DV:(h + 1) * GLA_DV]
            s_new = a_g[rs] * sg_ref[bb, h] + k_g[rs] * v_h
            nsg_ref[bb, h] = s_new
            outs.append(jnp.sum(q_g[rs] * s_new, axis=0, keepdims=True))
        for h in range(RET_HEADS):
            rs = slice(h * RET_DK, (h + 1) * RET_DK)
            k_h = col[base + h * RET_DK:base + (h + 1) * RET_DK]
            q_h = col[base + RET_QK + h * RET_DK:base + RET_QK + (h + 1) * RET_DK]
            v_h = vrow[:, GLA_V + h * RET_DV:GLA_V + (h + 1) * RET_DV]
            s_new = gam_ref[h] * sr_ref[bb, h] + k_h * v_h
            nsr_ref[bb, h] = s_new
            outs.append(jnp.sum(q_h * s_new, axis=0, keepdims=True))
        o_ref[b] = jnp.concatenate(outs, axis=1)
        return carry

    lax.fori_loop(0, nb, body, 0)


def _mix_sample_out_kernel(x_ref, o_ref, gates_ref, ggla_ref, gret_ref, bret_ref, wout_ref,
                           lng_ref, lnb_ref, y_ref):
    x = x_ref[...]
    o = o_ref[...]
    gates = gates_ref[...]
    ggla = ggla_ref[...]
    heads = []
    for h in range(GLA_HEADS):
        sl = slice(h * GLA_DV, (h + 1) * GLA_DV)
        oh = o[:, sl]
        oh = oh * lax.rsqrt(jnp.mean(oh * oh, axis=-1, keepdims=True) + NORM_EPS) * ggla
        heads.append(oh * _silu(gates[:, sl]))
    for h in range(RET_HEADS):
        sl = slice(GLA_V + h * RET_DV, GLA_V + (h + 1) * RET_DV)
        ps = slice(h * RET_DV, (h + 1) * RET_DV)
        oh = o[:, sl]
        mu = jnp.mean(oh, axis=-1, keepdims=True)
        oc = oh - mu
        var = jnp.mean(oc * oc, axis=-1, keepdims=True)
        oh = oc * lax.rsqrt(var + NORM_EPS) * gret_ref[:, ps] + bret_ref[:, ps]
        heads.append(oh * _silu(gates[:, sl]))
    mixed = jnp.concatenate(heads, axis=1).astype(BF16)
    y = _dot(mixed, wout_ref[...])
    y_ref[...] = _layer_norm(ALPHA * x + y, lng_ref[...], lnb_ref[...])


def _mix_sample(x, state_gla, state_ret, layer, li, stacked, win, wup, bgate, ggla, gret, bret, wout, lng, lnb,
                cos_s, sin_s, gam):
    s, d = x.shape
    whole = lambda a: pl.BlockSpec(a.shape, lambda i: (0,) * a.ndim)
    proj_out = [
        jax.ShapeDtypeStruct((_COL_ROWS, s), F32),
        jax.ShapeDtypeStruct((s, GLA_V + RET_V), F32),
        jax.ShapeDtypeStruct((s, GLA_V + RET_V), F32),
    ]
    cols, rows, gates = pl.pallas_call(
        _mix_sample_proj_kernel,
        grid=(1,),
        in_specs=[whole(x), _layer_spec(win.shape, li), _layer_spec(wup.shape, li),
                  _layer_spec(bgate.shape, li), whole(cos_s), whole(sin_s)],
        out_specs=[whole(o) for o in proj_out],
        out_shape=proj_out,
        compiler_params=pltpu.CompilerParams(
            dimension_semantics=("arbitrary",), vmem_limit_bytes=VMEM_LIMIT),
        name="mix_sample_proj",
    )(x, win, wup, bgate, cos_s, sin_s)
    rows = rows.reshape(s, 1, GLA_V + RET_V)

    nb = SUBLANES
    assert s % nb == 0
    gla_spec = pl.BlockSpec((None, nb, GLA_HEADS, GLA_DK, GLA_DV), lambda i: (li, i, 0, 0, 0))
    ret_spec = pl.BlockSpec((None, nb, RET_HEADS, RET_DK, RET_DV), lambda i: (li, i, 0, 0, 0))
    stacked = tuple(stacked)
    nsg, nsr, o = pl.pallas_call(
        functools.partial(_mix_sample_state_kernel, n_stacked_in=len(stacked)),
        grid=(s // nb,),
        in_specs=[
            pl.BlockSpec((_COL_ROWS, s), lambda i: (0, 0)),
            pl.BlockSpec((s, 1, GLA_V + RET_V), lambda i: (0, 0, 0)),
            gla_spec,
            ret_spec,
            pl.BlockSpec((RET_HEADS, RET_DK, RET_DV), lambda i: (0, 0, 0)),
        ] + [pl.BlockSpec(memory_space=pl.ANY)] * len(stacked),
        out_specs=[
            gla_spec,
            ret_spec,
            pl.BlockSpec((s, 1, GLA_V + RET_V), lambda i: (0, 0, 0)),
        ],
        out_shape=[
            jax.ShapeDtypeStruct(state_gla.shape, F32),
            jax.ShapeDtypeStruct(state_ret.shape, F32),
            jax.ShapeDtypeStruct((s, 1, GLA_V + RET_V), F32),
        ],
        input_output_aliases={5 + k: k for k in range(len(stacked))},
        compiler_params=pltpu.CompilerParams(
            dimension_semantics=("arbitrary",), vmem_limit_bytes=VMEM_LIMIT),
        name="mix_sample_state",
    )(cols, rows, state_gla, state_ret, gam, *stacked)
    o = o.reshape(s, GLA_V + RET_V)

    y = pl.pallas_call(
        _mix_sample_out_kernel,
        grid=(1,),
        in_specs=[whole(x), whole(o), whole(gates), _layer_spec(ggla.shape, li), _layer_spec(gret.shape, li),
                  _layer_spec(bret.shape, li), _layer_spec(wout.shape, li),
                  _layer_spec(lng.shape, layer), _layer_spec(lnb.shape, layer)],
        out_specs=whole(x),
        out_shape=jax.ShapeDtypeStruct((s, d), F32),
        compiler_params=pltpu.CompilerParams(
            dimension_semantics=("arbitrary",), vmem_limit_bytes=VMEM_LIMIT),
        name="mix_sample_out",
    )(x, o, gates, ggla, gret, bret, wout, lng, lnb)
    return y, nsg, nsr


def _prep_mix_in(w):
    a = 2 * GLA_QK + GLA_V
    lr = jnp.pad(w[:, :, a:a + GLA_RANK], ((0, 0), (0, 0), (0, PAD_RANK - GLA_RANK)))
    return jnp.concatenate([w[:, :, :a], w[:, :, a + GLA_RANK:], lr], axis=2).astype(BF16)


def kernel(x_prompt, x_sample, state_gla, state_ret, state_conv, state_ffn,
           w_mix_in, w_gla_gate_up, b_gla_gate, g_gla_norm, g_ret_norm, b_ret_norm, w_mix_out,
           w_sc_in, w_sc_dw, b_sc_dw, w_sc_out, ln1_g, ln1_b,
           w_ffn_in, w_ffn_dw, b_ffn_dw, w_ffn_out, ln2_g, ln2_b):
    bp, tp, d = x_prompt.shape
    s, ts, _ = x_sample.shape
    assert ts == 1 and d == D_MODEL
    xp = x_prompt.reshape(bp * tp, d)
    xs = x_sample.reshape(s, d)

    tables = _mix_tables(tp)
    cos_s, sin_s = _rotary_tables(PAST_LEN + jnp.arange(ts, dtype=F32))
    log_gamma = jnp.log1p(-jnp.exp2(-5.0 - jnp.arange(RET_HEADS, dtype=F32)))
    gam = jnp.broadcast_to(jnp.exp(log_gamma)[:, None, None], (RET_HEADS, RET_DK, RET_DV))

    rows = lambda v: v[:, None, :]
    mix_w = (_prep_mix_in(w_mix_in),
             jnp.pad(w_gla_gate_up, ((0, 0), (0, PAD_RANK - GLA_RANK), (0, 0))).astype(BF16),
             rows(b_gla_gate), rows(g_gla_norm), rows(g_ret_norm), rows(b_ret_norm), w_mix_out.astype(BF16))
    later_w = (w_sc_in, w_sc_out, w_ffn_in, w_ffn_out)
    ln1 = (rows(ln1_g), rows(ln1_b))
    ln2 = (rows(ln2_g), rows(ln2_b))

    gla_p, ret_p, conv_p, ffn_p = [], [], [], []
    mix_s = None
    conv_s = jnp.zeros_like(state_conv)
    ffn_s = jnp.zeros_like(state_ffn)
    for layer in range(DEPTH):
        li = layer // 2
        if layer % 2 == 0:
            xp, sg, sr, casted, zeros = _mix_prompt(
                xp, bp, tp, layer, li, *mix_w, *ln1, tables,
                to_cast=later_w if layer == 0 else (), zeros_like=(state_gla, state_ret) if layer == 0 else ())
            if layer == 0:
                sc_in, sc_out, ffn_in, ffn_out = casted
                sc_w = (sc_in, w_sc_dw, rows(b_sc_dw), sc_out)
                ffn_w = (ffn_in, w_ffn_dw, rows(b_ffn_dw), ffn_out)
                mix_s = zeros
            gla_p.append(sg)
            ret_p.append(sr)
            xs, *mix_s = _mix_sample(xs, state_gla, state_ret, layer, li, mix_s, *mix_w, *ln1,
                                     cos_s, sin_s, gam)
        else:
            xp, sc = _sc_prompt(xp, tp, layer, li, *sc_w, *ln1)
            conv_p.append(sc)
            xs, conv_s = _conv_sample(_sc_sample_kernel, "sc_sample", xs, state_conv, conv_s,
                                      layer, li, *sc_w, *ln1)
        xp, sf = _ffn_prompt(xp, tp, layer, *ffn_w, *ln2)
        ffn_p.append(sf)
        xs, ffn_s = _conv_sample(_ffn_sample_kernel, "ffn_sample", xs, state_ffn, ffn_s,
                                 layer, layer, *ffn_w, *ln2)
    gla_s, ret_s = mix_s

    return (xp.reshape(bp, tp, d), xs.reshape(s, ts, d),
            jnp.stack(gla_p), gla_s, jnp.stack(ret_p), ret_s,
            jnp.stack(conv_p), conv_s, jnp.stack(ffn_p), ffn_s)
```

```python
import functools

import jax
import jax.numpy as jnp
from jax import lax
from jax.experimental import pallas as pl
from jax.experimental.pallas import tpu as pltpu

F32 = jnp.float32
BF16 = jnp.bfloat16

D_MODEL = 1024
DEPTH = 4
PAST_LEN = 16384
GLA_HEADS = 4
GLA_DK = 64
GLA_DV = 128
GLA_RANK = 16
GLA_TAU = 16.0
RET_HEADS = 4
RET_DK = 128
RET_DV = 128
CONV_W = 3
D_FF = D_MODEL * 11 // 4
ROPE_BASE = 10000.0
NORM_EPS = 1e-5
ALPHA = (2 * DEPTH) ** 0.25
GLA_QK = GLA_HEADS * GLA_DK
GLA_V = GLA_HEADS * GLA_DV
RET_QK = RET_HEADS * RET_DK
RET_V = RET_HEADS * RET_DV

LANES = 128
SUBLANES = 8
VMEM_LIMIT = 56 * 1024 * 1024

GLA_CHUNK = 64
GLA_SUB = 32
MIX_TILE = 256
ROW_TILE = 512
SC_ROW_TILE = 1024
CONV_ROWS = 32
FF_CHUNK = 768
FF_FIRST_CHUNK = 512
SC_CHUNK = 256
PAD_RANK = LANES

_OFF_QG = 0
_OFF_KG = _OFF_QG + GLA_QK
_OFF_VG = _OFF_KG + GLA_QK
_OFF_RG = _OFF_VG + GLA_V
_OFF_QR = _OFF_RG + GLA_V
_OFF_KR = _OFF_QR + RET_QK
_OFF_VR = _OFF_KR + RET_QK
_OFF_GR = _OFF_VR + RET_V
_OFF_LR = _OFF_GR + RET_V
MIX_COLS = _OFF_LR + PAD_RANK


def _dot(a, b):
    return jnp.dot(a, b, preferred_element_type=F32)


def _dot_nt(a, b):
    return lax.dot_general(a, b, (((1,), (1,)), ((), ())), preferred_element_type=F32)


def _layer_norm(x, g, b):
    mu = jnp.mean(x, axis=-1, keepdims=True)
    xc = x - mu
    var = jnp.mean(xc * xc, axis=-1, keepdims=True)
    return xc * lax.rsqrt(var + NORM_EPS) * g + b


def _silu(x):
    return x * jax.nn.sigmoid(x)


def _log_sigmoid(x):
    return jnp.minimum(x, 0.0) - jnp.log1p(jnp.exp(-jnp.abs(x)))


def _split3(x):
    hi = x.astype(BF16)
    r1 = x - hi.astype(F32)
    mid = r1.astype(BF16)
    lo = (r1 - mid.astype(F32)).astype(BF16)
    return hi, mid, lo


def _causal_conv3_blocks(rows, n_rows, prev, w, b, post):
    w0, w1, w2 = w[0:1], w[1:2], w[2:3]
    out = []
    for lo in range(0, n_rows, CONV_ROWS):
        head = prev if lo == 0 else rows(lo - SUBLANES, lo)
        s = jnp.concatenate([head, rows(lo, lo + CONV_ROWS)], axis=0)
        s1 = pltpu.roll(s, 1, 0)[SUBLANES:]
        s2 = pltpu.roll(s, 2, 0)[SUBLANES:]
        out.append(post(lo, b + w0 * s2 + w1 * s1 + w2 * s[SUBLANES:]))
    return jnp.concatenate(out, axis=0)


def _ffn_prompt_kernel(x_ref, win_ref, wdw_ref, bdw_ref, wout_ref, g_ref, b_ref,
                       y_ref, st_ref, xb_ref, carry_ref, *, tiles_per_seq):
    i = pl.program_id(0)

    @pl.when(i % tiles_per_seq == 0)
    def _():
        carry_ref[...] = jnp.zeros_like(carry_ref)

    tm = x_ref.shape[0]
    ff = wout_ref.shape[0]
    first = min(FF_FIRST_CHUNK, ff)
    bounds = [0] + list(range(first, ff, FF_CHUNK)) + [ff]
    n_chunks = len(bounds) - 1
    xb_ref[...] = x_ref[...].astype(BF16)

    def cols(ref, f):
        lo, hi = bounds[f], bounds[f + 1]
        return jnp.concatenate([ref[:, lo:hi], ref[:, ff + lo:ff + hi]], axis=1)

    def up(f):
        lo, hi = bounds[f], bounds[f + 1]
        xb = xb_ref[...]
        return jnp.concatenate([_dot(xb, win_ref[:, lo:hi]), _dot(xb, win_ref[:, ff + lo:ff + hi])], axis=1)

    def act(f, u):
        lo, hi = bounds[f], bounds[f + 1]
        fc = hi - lo
        h = _causal_conv3_blocks(
            lambda r0, r1: u[r0:r1], tm, cols(carry_ref, f), cols(wdw_ref, f), cols(bdw_ref, f),
            lambda r0, c: (_silu(c[:, fc:]) * c[:, :fc]).astype(BF16))
        for k in range(2):
            tail = u[tm - SUBLANES:, k * fc:(k + 1) * fc]
            carry_ref[:, k * ff + lo:k * ff + hi] = tail
            st_ref[0, :, k * ff + lo:k * ff + hi] = tail
        return h

    def down(f, h):
        return _dot(h, wout_ref[bounds[f]:bounds[f + 1], :])

    u = up(0)
    h_prev = None
    acc = None
    for f in range(n_chunks):
        u_next = up(f + 1) if f + 1 < n_chunks else None
        h = act(f, u)
        if h_prev is not None:
            p = down(f - 1, h_prev)
            acc = p if acc is None else acc + p
        u, h_prev = u_next, h
    half = tm // 2
    w_last = wout_ref[bounds[n_chunks - 1]:bounds[n_chunks], :]
    for lo_r in (0, half):
        rs = slice(lo_r, lo_r + half)
        tot = acc[rs] + _dot(h_prev[rs], w_last)
        y_ref[rs, :] = _layer_norm(ALPHA * x_ref[rs, :] + tot, g_ref[...], b_ref[...])


def _layer_spec(shape, layer):
    zeros = (0,) * (len(shape) - 1)
    return pl.BlockSpec((None,) + tuple(shape[1:]), lambda *_: (layer,) + zeros)


def _ffn_prompt(x, seq_len, layer, win, wdw, bdw, wout, g, b):
    n, d = x.shape
    tm = min(ROW_TILE, seq_len)
    assert seq_len % tm == 0 and n % seq_len == 0
    n_seq = n // seq_len
    tps = seq_len // tm
    ff2 = win.shape[2]
    buffered = lambda spec: pl.BlockSpec(spec.block_shape, spec.index_map, pipeline_mode=pl.Buffered(1))
    y, st = pl.pallas_call(
        functools.partial(_ffn_prompt_kernel, tiles_per_seq=tps),
        grid=(n // tm,),
        in_specs=[
            pl.BlockSpec((tm, d), lambda i: (i, 0)),
            buffered(_layer_spec(win.shape, layer)),
            _layer_spec(wdw.shape, layer),
            _layer_spec(bdw.shape, layer),
            buffered(_layer_spec(wout.shape, layer)),
            _layer_spec(g.shape, layer),
            _layer_spec(b.shape, layer),
        ],
        out_specs=[
            pl.BlockSpec((tm, d), lambda i: (i, 0)),
            pl.BlockSpec((1, SUBLANES, ff2), lambda i: (i // tps, 0, 0)),
        ],
        out_shape=[
            jax.ShapeDtypeStruct((n, d), F32),
            jax.ShapeDtypeStruct((n_seq, SUBLANES, ff2), F32),
        ],
        scratch_shapes=[
            pltpu.VMEM((tm, d), BF16),
            pltpu.VMEM((SUBLANES, ff2), F32),
        ],
        compiler_params=pltpu.CompilerParams(
            dimension_semantics=("arbitrary",), vmem_limit_bytes=VMEM_LIMIT),
        name="ffn_prompt",
    )(x, win, wdw, bdw, wout, g, b)
    return y, st[:, SUBLANES - 2:, :]


def _sc_prompt_kernel(x_ref, win_ref, wdw_ref, bdw_ref, wout_ref, g_ref, b_ref,
                      y_ref, st_ref, xb_ref, carry_ref, *, tiles_per_seq):
    i = pl.program_id(0)

    @pl.when(i % tiles_per_seq == 0)
    def _():
        carry_ref[...] = jnp.zeros_like(carry_ref)

    tm, d = x_ref.shape
    cw = SC_CHUNK
    n_chunks = d // cw
    xb_ref[...] = x_ref[...].astype(BF16)

    def up(f):
        xb = xb_ref[...]
        return jnp.concatenate([_dot(xb, win_ref[:, k * d + f * cw:k * d + (f + 1) * cw]) for k in range(3)],
                               axis=1)

    def act(f, z):
        cs = slice(f * cw, (f + 1) * cw)
        ch_rows = lambda lo, hi: z[lo:hi, cw:2 * cw] * z[lo:hi, 2 * cw:]
        m = _causal_conv3_blocks(
            ch_rows, tm, carry_ref[:, cs], wdw_ref[:, cs], bdw_ref[:, cs],
            lambda lo, conv: (z[lo:lo + CONV_ROWS, :cw] * conv).astype(BF16))
        tail = ch_rows(tm - SUBLANES, tm)
        carry_ref[:, cs] = tail
        st_ref[0, :, cs] = tail
        return m

    def down(f, m):
        return _dot(m, wout_ref[f * cw:(f + 1) * cw, :])

    z = up(0)
    m_prev = None
    acc = None
    for f in range(n_chunks):
        z_next = up(f + 1) if f + 1 < n_chunks else None
        m = act(f, z)
        if m_prev is not None:
            p = down(f - 1, m_prev)
            acc = p if acc is None else acc + p
        z, m_prev = z_next, m
    half = tm // 2
    w_last = wout_ref[(n_chunks - 1) * cw:n_chunks * cw, :]
    for lo_r in (0, half):
        rs = slice(lo_r, lo_r + half)
        tot = acc[rs] + _dot(m_prev[rs], w_last)
        y_ref[rs, :] = _layer_norm(ALPHA * x_ref[rs, :] + tot, g_ref[...], b_ref[...])


def _sc_prompt(x, seq_len, layer, li, win, wdw, bdw, wout, g, b):
    n, d = x.shape
    tm = min(SC_ROW_TILE, seq_len)
    assert seq_len % tm == 0 and n % seq_len == 0
    n_seq = n // seq_len
    tps = seq_len // tm
    buffered = lambda spec: pl.BlockSpec(spec.block_shape, spec.index_map, pipeline_mode=pl.Buffered(1))
    y, st = pl.pallas_call(
        functools.partial(_sc_prompt_kernel, tiles_per_seq=tps),
        grid=(n // tm,),
        in_specs=[
            pl.BlockSpec((tm, d), lambda i: (i, 0)),
            buffered(_layer_spec(win.shape, li)),
            _layer_spec(wdw.shape, li),
            _layer_spec(bdw.shape, li),
            buffered(_layer_spec(wout.shape, li)),
            _layer_spec(g.shape, layer),
            _layer_spec(b.shape, layer),
        ],
        out_specs=[
            pl.BlockSpec((tm, d), lambda i: (i, 0)),
            pl.BlockSpec((1, SUBLANES, d), lambda i: (i // tps, 0, 0)),
        ],
        out_shape=[
            jax.ShapeDtypeStruct((n, d), F32),
            jax.ShapeDtypeStruct((n_seq, SUBLANES, d), F32),
        ],
        scratch_shapes=[
            pltpu.VMEM((tm, d), BF16),
            pltpu.VMEM((SUBLANES, d), F32),
        ],
        compiler_params=pltpu.CompilerParams(
            dimension_semantics=("arbitrary",), vmem_limit_bytes=VMEM_LIMIT),
        name="sc_prompt",
    )(x, win, wdw, bdw, wout, g, b)
    return y, st[:, SUBLANES - 2:, :]


def _rotary(x, cosf, sinf):
    return x * cosf + pltpu.roll(x, RET_DK // 2, 1) * sinf


def _gla_tile(qg, kg, vg, wn_all, r_all, st, mask_q, masks_p, mask_bd, sblock):
    c = GLA_CHUNK
    n_chunks = qg.shape[0] // c
    n_sub = c // GLA_SUB
    pre = []
    for ci in range(n_chunks):
        sl = slice(ci * c, (ci + 1) * c)
        q_c, k_c, v_c, wn, r = qg[sl], kg[sl], vg[sl], wn_all[sl], r_all[sl]
        cum = r + wn
        tot = cum[c - 1:c, :]
        vst = jnp.concatenate([v_c[:, h * GLA_DV:(h + 1) * GLA_DV] for h in range(GLA_HEADS)], axis=0)
        pre.append(dict(q=q_c, k=k_c, r=r, cum=cum, tot=tot, qw=q_c * jnp.exp(wn), vst=vst,
                        vst_b=vst.astype(BF16)))
    scores = []
    for p in pre:
        row = []
        for i in range(n_sub):
            r_i = p["r"][i * GLA_SUB:i * GLA_SUB + 1, :]
            kt = jnp.where(sblock <= i, p["k"] * jnp.exp(r_i - p["cum"]), 0.0)
            kbig = jnp.concatenate([kt] * GLA_HEADS, axis=0).astype(BF16)
            q16 = p["qw"][i * GLA_SUB:(i + 1) * GLA_SUB]
            qs = jnp.where(mask_q, jnp.concatenate([q16] * GLA_HEADS, axis=0), 0.0).astype(BF16)
            row.append(_dot_nt(qs, kbig))
        scores.append(row)
    intra = [[_dot(jnp.where(masks_p[i], scores[ci][i], 0.0).astype(BF16), p["vst_b"])
              for i in range(n_sub)] for ci, p in enumerate(pre)]
    upds, qcss = [], []
    for p in pre:
        kd = p["k"] * jnp.exp(p["tot"] - p["cum"])
        kdbig = jnp.where(mask_bd, jnp.concatenate([kd] * GLA_HEADS, axis=0), 0.0).astype(BF16)
        upds.append(_dot(p["vst"].T.astype(BF16), kdbig))
        qc = p["q"] * jnp.exp(p["cum"])
        qcss.append(jnp.where(mask_bd, jnp.concatenate([qc] * GLA_HEADS, axis=0), 0.0).astype(BF16))
    inters = []
    for ci, p in enumerate(pre):
        inters.append(_dot_nt(qcss[ci], st.astype(BF16)))
        st = jnp.exp(p["tot"]) * st + upds[ci]
    chunks = []
    for ci in range(n_chunks):
        heads = []
        for h in range(GLA_HEADS):
            intra_h = jnp.concatenate([intra[ci][i][h * GLA_SUB:(h + 1) * GLA_SUB] for i in range(n_sub)], axis=0)
            heads.append(intra_h + inters[ci][h * c:(h + 1) * c])
        chunks.append(jnp.concatenate(heads, axis=1))
    return jnp.concatenate(chunks, axis=0), st


_MIX_N_IN = 16


def _mix_prompt_kernel(*refs, n_cast, n_zero):
    (x_ref, win_ref, wup_ref, bgate_ref, ggla_ref, gret_ref, bret_ref, wout_ref,
     lng_ref, lnb_ref, cos_ref, sin_ref, mcat_ref, dmask_ref, qdec_ref, kdec_ref) = refs[:_MIX_N_IN]
    cast_in = refs[_MIX_N_IN:_MIX_N_IN + n_cast]
    n_in = _MIX_N_IN + n_cast
    y_ref, sg_ref, sr_ref = refs[n_in:n_in + 3]
    cast_out = refs[n_in + 3:n_in + 3 + n_cast]
    zero_out = refs[n_in + 3 + n_cast:n_in + 3 + n_cast + n_zero]
    stg_ref, str_ref = refs[n_in + 3 + n_cast + n_zero:]
    for src, dst in zip(cast_in, cast_out):
        dst[...] = src[...].astype(BF16)
    for dst in zero_out:
        dst[...] = jnp.zeros_like(dst)
    j = pl.program_id(1)
    nj = pl.num_programs(1)

    @pl.when(j == 0)
    def _():
        stg_ref[...] = jnp.zeros_like(stg_ref)
        str_ref[...] = jnp.zeros_like(str_ref)

    tc = x_ref.shape[0]
    x = x_ref[...]
    xb = x.astype(BF16)

    def proj(off, width):
        return _dot(xb, win_ref[:, off:off + width])

    lr = proj(_OFF_LR, PAD_RANK)
    gate = _dot(lr.astype(BF16), wup_ref[...]) + bgate_ref[...]
    lg = _log_sigmoid(gate) * (1.0 / GLA_TAU)
    hi, mid, lo = _split3(lg)
    qr_all = proj(_OFF_QR, RET_QK)
    kr_all = proj(_OFF_KR, RET_QK)
    vr_all = proj(_OFF_VR, RET_V)
    qg = proj(_OFF_QG, GLA_QK) * (GLA_DK ** -0.5)
    kg = proj(_OFF_KG, GLA_QK)
    vg = proj(_OFF_VG, GLA_V)
    cs = _dot(mcat_ref[...], jnp.concatenate([hi, mid, lo], axis=1))
    cs = cs[:, :GLA_QK] + cs[:, GLA_QK:2 * GLA_QK] + cs[:, 2 * GLA_QK:]
    wn_all = cs[:tc]
    r_all = cs[tc:]

    cosf = cos_ref[...]
    sinf = sin_ref[...]
    hs = range(RET_HEADS)
    hsl = [slice(h * RET_DK, (h + 1) * RET_DK) for h in hs]
    qhs = [_rotary(qr_all[:, hsl[h]], cosf, sinf) for h in hs]
    khs = [_rotary(kr_all[:, hsl[h]], cosf, sinf) * (RET_DK ** -0.5) for h in hs]
    vhs = [vr_all[:, hsl[h]].astype(BF16) for h in hs]
    scs = [_dot_nt(qhs[h].astype(BF16), khs[h].astype(BF16)) for h in hs]
    s_hs = [str_ref[h] for h in hs]
    inter_r = [_dot((qhs[h] * qdec_ref[h]).astype(BF16), s_hs[h].astype(BF16)) for h in hs]
    upd_r = [_dot((khs[h] * kdec_ref[h]).T.astype(BF16), vhs[h]) for h in hs]
    o_r = [_dot((scs[h] * dmask_ref[h]).astype(BF16), vhs[h]) + inter_r[h] for h in hs]
    for h in hs:
        str_ref[h] = qdec_ref[h][tc - 1:tc, :] * s_hs[h] + upd_r[h]
    gr_all = proj(_OFF_GR, RET_V)
    rg = proj(_OFF_RG, GLA_V)


    c = GLA_CHUNK
    assert c == GLA_DK
    hq = GLA_HEADS * GLA_SUB
    log2 = lambda n: n.bit_length() - 1
    row64 = lax.broadcasted_iota(jnp.int32, (hq, GLA_QK), 0)
    lane64 = lax.broadcasted_iota(jnp.int32, (hq, GLA_QK), 1)
    lane_head = lax.shift_right_logical(lane64, log2(c))
    row_head = lax.shift_right_logical(row64, log2(GLA_SUB))
    mask_q = lane_head == row_head
    masks_p = [mask_q & ((lane64 & (c - 1)) <= (row64 & (GLA_SUB - 1)) + i * GLA_SUB)
               for i in range(c // GLA_SUB)]
    rowb = lax.broadcasted_iota(jnp.int32, (GLA_HEADS * c, GLA_QK), 0)
    laneb = lax.broadcasted_iota(jnp.int32, (GLA_HEADS * c, GLA_QK), 1)
    mask_bd = lax.shift_right_logical(rowb, log2(c)) == lax.shift_right_logical(laneb, log2(c))
    sblock = lax.shift_right_logical(lax.broadcasted_iota(jnp.int32, (c, GLA_QK), 0), log2(GLA_SUB))

    og, st = _gla_tile(qg, kg, vg, wn_all, r_all, stg_ref[...], mask_q, masks_p, mask_bd, sblock)
    stg_ref[...] = st
    ggla = ggla_ref[...]
    og_heads = []
    for h in range(GLA_HEADS):
        sl = slice(h * GLA_DV, (h + 1) * GLA_DV)
        o = og[:, sl]
        o = o * lax.rsqrt(jnp.mean(o * o, axis=-1, keepdims=True) + NORM_EPS) * ggla
        og_heads.append(o * _silu(rg[:, sl]))
    or_heads = []
    for h in hs:
        o = o_r[h]
        mu = jnp.mean(o, axis=-1, keepdims=True)
        oc = o - mu
        var = jnp.mean(oc * oc, axis=-1, keepdims=True)
        o = oc * lax.rsqrt(var + NORM_EPS) * gret_ref[:, hsl[h]] + bret_ref[:, hsl[h]]
        or_heads.append(o * _silu(gr_all[:, hsl[h]]))

    mixed = jnp.concatenate(og_heads + or_heads, axis=1).astype(BF16)
    half = tc // 2
    for lo_r in (0, half):
        rs = slice(lo_r, lo_r + half)
        y = _dot(mixed[rs], wout_ref[...])
        y_ref[rs, :] = _layer_norm(ALPHA * x[rs] + y, lng_ref[...], lnb_ref[...])

    @pl.when(j == nj - 1)
    def _():
        sg_ref[0] = stg_ref[...].T
        sr_ref[0] = str_ref[...]


def _mix_prompt(x, n_seq, seq_len, layer, li, win, wup, bgate, ggla, gret, bret, wout, lng, lnb, tables,
                to_cast=(), zeros_like=()):
    n, d = x.shape
    tc = min(MIX_TILE, seq_len)
    assert seq_len % tc == 0 and tc % GLA_CHUNK == 0
    nt = seq_len // tc
    cosf, sinf, mcat, dmask, qdec, kdec = tables
    c2 = lambda b, j: (0, 0)
    c3 = lambda b, j: (0, 0, 0)
    buffered = lambda spec: pl.BlockSpec(spec.block_shape, spec.index_map, pipeline_mode=pl.Buffered(1))
    n_steps = n_seq * nt
    flat = [w.reshape(-1, w.shape[-1]) for w in to_cast]
    for w in flat:
        assert w.shape[0] % (n_steps * 2 * SUBLANES) == 0, w.shape
    row_block = lambda w: pl.BlockSpec((w.shape[0] // n_steps, w.shape[1]), lambda b, j: (b * nt + j, 0))
    cast_specs = [row_block(w) for w in flat]
    zflat = [jax.ShapeDtypeStruct((z.size // z.shape[-1], z.shape[-1]), F32) for z in zeros_like]
    for z in zflat:
        assert z.shape[0] % (n_steps * SUBLANES) == 0, z.shape
    zero_specs = [row_block(z) for z in zflat]
    y, sg, sr, *side = pl.pallas_call(
        functools.partial(_mix_prompt_kernel, n_cast=len(flat), n_zero=len(zflat)),
        grid=(n_seq, nt),
        in_specs=[
            pl.BlockSpec((tc, d), lambda b, j: (b * nt + j, 0)),
            buffered(_layer_spec(win.shape, li)),
            _layer_spec(wup.shape, li),
            _layer_spec(bgate.shape, li),
            _layer_spec(ggla.shape, li),
            _layer_spec(gret.shape, li),
            _layer_spec(bret.shape, li),
            buffered(_layer_spec(wout.shape, li)),
            _layer_spec(lng.shape, layer),
            _layer_spec(lnb.shape, layer),
            pl.BlockSpec((tc, RET_DK), lambda b, j: (j, 0)),
            pl.BlockSpec((tc, RET_DK), lambda b, j: (j, 0)),
            pl.BlockSpec((2 * tc, tc), c2),
            pl.BlockSpec((RET_HEADS, tc, tc), c3),
            pl.BlockSpec((RET_HEADS, tc, RET_DK), c3),
            pl.BlockSpec((RET_HEADS, tc, RET_DK), c3),
        ] + cast_specs,
        out_specs=[
            pl.BlockSpec((tc, d), lambda b, j: (b * nt + j, 0)),
            pl.BlockSpec((1, GLA_QK, GLA_DV), lambda b, j: (b, 0, 0)),
            pl.BlockSpec((1, RET_HEADS, RET_DK, RET_DV), lambda b, j: (b, 0, 0, 0)),
        ] + cast_specs + zero_specs,
        out_shape=[
            jax.ShapeDtypeStruct((n, d), F32),
            jax.ShapeDtypeStruct((n_seq, GLA_QK, GLA_DV), F32),
            jax.ShapeDtypeStruct((n_seq, RET_HEADS, RET_DK, RET_DV), F32),
        ] + [jax.ShapeDtypeStruct(w.shape, BF16) for w in flat] + zflat,
        scratch_shapes=[
            pltpu.VMEM((GLA_DV, GLA_QK), F32),
            pltpu.VMEM((RET_HEADS, RET_DK, RET_DV), F32),
        ],
        compiler_params=pltpu.CompilerParams(
            dimension_semantics=("arbitrary", "arbitrary"), vmem_limit_bytes=VMEM_LIMIT),
        name="mix_prompt",
    )(x, win, wup, bgate, ggla, gret, bret, wout, lng, lnb, cosf, sinf, mcat, dmask, qdec, kdec, *flat)
    casted = [c.reshape(w.shape) for c, w in zip(side, to_cast)]
    zeros = [z.reshape(a.shape) for z, a in zip(side[len(flat):], zeros_like)]
    return y, sg.reshape(n_seq, GLA_HEADS, GLA_DK, GLA_DV), sr, casted, zeros


def _mix_tables(seq_len):
    tc = min(MIX_TILE, seq_len)
    pos = jnp.arange(seq_len, dtype=F32)
    cosf, sinf = _rotary_tables(pos)
    t = jnp.arange(tc)
    same_sub = (t[:, None] // GLA_SUB) == (t[None, :] // GLA_SUB)
    same_chunk = (t[:, None] // GLA_CHUNK) == (t[None, :] // GLA_CHUNK)
    m_within = same_sub & (t[None, :] <= t[:, None])
    m_before = same_chunk & ((t[None, :] // GLA_SUB) < (t[:, None] // GLA_SUB))
    mcat = jnp.concatenate([m_within, m_before], axis=0).astype(BF16)
    log_gamma = jnp.log1p(-jnp.exp2(-5.0 - jnp.arange(RET_HEADS, dtype=F32)))
    tf = t.astype(F32)
    diff = tf[:, None] - tf[None, :]
    dmask = jnp.where(diff >= 0, jnp.exp(diff[None] * log_gamma[:, None, None]), 0.0)
    qdec = jnp.exp((tf[None, :, None] + 1.0) * log_gamma[:, None, None])
    kdec = jnp.exp((tc - 1.0 - tf[None, :, None]) * log_gamma[:, None, None])
    qdec = jnp.broadcast_to(qdec, (RET_HEADS, tc, RET_DK))
    kdec = jnp.broadcast_to(kdec, (RET_HEADS, tc, RET_DK))
    return cosf, sinf, mcat, dmask, qdec, kdec


def _rotary_tables(pos):
    inv_freq = 1.0 / (ROPE_BASE ** (jnp.arange(0, RET_DK, 2, dtype=F32) / RET_DK))
    ang = pos[:, None] * inv_freq[None, :]
    cos, sin = jnp.cos(ang), jnp.sin(ang)
    return jnp.concatenate([cos, cos], axis=1), jnp.concatenate([-sin, sin], axis=1)


def _ffn_sample_kernel(x_ref, st_ref, win_ref, wdw_ref, bdw_ref, wout_ref, g_ref, b_ref,
                       y_ref, nst_ref):
    x = x_ref[...]
    u = _dot(x.astype(BF16), win_ref[...])
    wdw = wdw_ref[...]
    b1 = st_ref[:, 1, :]
    c = bdw_ref[...] + wdw[0:1] * st_ref[:, 0, :] + wdw[1:2] * b1 + wdw[2:3] * u
    h = _silu(c[:, D_FF:]) * c[:, :D_FF]
    y = _dot(h.astype(BF16), wout_ref[...])
    y_ref[...] = _layer_norm(ALPHA * x + y, g_ref[...], b_ref[...])
    nst_ref[:, 0, :] = b1
    nst_ref[:, 1, :] = u


def _sc_sample_kernel(x_ref, st_ref, win_ref, wdw_ref, bdw_ref, wout_ref, g_ref, b_ref,
                      y_ref, nst_ref):
    x = x_ref[...]
    d = x.shape[1]
    z = _dot(x.astype(BF16), win_ref[...])
    ch = z[:, d:2 * d] * z[:, 2 * d:]
    wdw = wdw_ref[...]
    b1 = st_ref[:, 1, :]
    conv = bdw_ref[...] + wdw[0:1] * st_ref[:, 0, :] + wdw[1:2] * b1 + wdw[2:3] * ch
    y = _dot((z[:, :d] * conv).astype(BF16), wout_ref[...])
    y_ref[...] = _layer_norm(ALPHA * x + y, g_ref[...], b_ref[...])
    nst_ref[:, 0, :] = b1
    nst_ref[:, 1, :] = ch


def _conv_sample_kernel(body, *refs):
    body(*refs[:8], *refs[9:])


def _conv_sample(body, name, x, states, new_states, layer, li, win, wdw, bdw, wout, g, b):
    s, d = x.shape
    whole = lambda a: pl.BlockSpec(a.shape, lambda i: (0,) * a.ndim)
    st_spec = _layer_spec(states.shape, li)
    return pl.pallas_call(
        functools.partial(_conv_sample_kernel, body),
        grid=(1,),
        in_specs=[whole(x), st_spec, _layer_spec(win.shape, li), _layer_spec(wdw.shape, li),
                  _layer_spec(bdw.shape, li), _layer_spec(wout.shape, li),
                  _layer_spec(g.shape, layer), _layer_spec(b.shape, layer),
                  pl.BlockSpec(memory_space=pl.ANY)],
        out_specs=[whole(x), st_spec],
        out_shape=[jax.ShapeDtypeStruct((s, d), F32), jax.ShapeDtypeStruct(states.shape, F32)],
        input_output_aliases={8: 1},
        compiler_params=pltpu.CompilerParams(
            dimension_semantics=("arbitrary",), vmem_limit_bytes=VMEM_LIMIT),
        name=name,
    )(x, states, win, wdw, bdw, wout, g, b, new_states)


def _mix_sample_proj_kernel(x_ref, win_ref, wup_ref, bgate_ref, cos_ref, sin_ref,
                            cols_ref, rows_ref, gates_ref):
    xb = x_ref[...].astype(BF16)

    def proj(off, width):
        return _dot(xb, win_ref[:, off:off + width])

    qg = proj(_OFF_QG, GLA_QK) * (GLA_DK ** -0.5)
    kg = proj(_OFF_KG, GLA_QK)
    lr = proj(_OFF_LR, PAD_RANK)
    gate = _dot(lr.astype(BF16), wup_ref[...]) + bgate_ref[...]
    ag = jnp.exp(_log_sigmoid(gate) * (1.0 / GLA_TAU))
    cosf = cos_ref[...]
    sinf = sin_ref[...]
    qr = [_rotary(proj(_OFF_QR + h * RET_DK, RET_DK), cosf, sinf) for h in range(RET_HEADS)]
    kr = [_rotary(proj(_OFF_KR + h * RET_DK, RET_DK), cosf, sinf) * (RET_DK ** -0.5)
          for h in range(RET_HEADS)]
    cols_ref[0:GLA_QK, :] = ag.T
    cols_ref[GLA_QK:2 * GLA_QK, :] = kg.T
    cols_ref[2 * GLA_QK:3 * GLA_QK, :] = qg.T
    base = 3 * GLA_QK
    for h in range(RET_HEADS):
        cols_ref[base + h * RET_DK:base + (h + 1) * RET_DK, :] = kr[h].T
        cols_ref[base + RET_QK + h * RET_DK:base + RET_QK + (h + 1) * RET_DK, :] = qr[h].T
    rows_ref[:, :GLA_V] = proj(_OFF_VG, GLA_V)
    rows_ref[:, GLA_V:] = proj(_OFF_VR, RET_V)
    gates_ref[:, :GLA_V] = proj(_OFF_RG, GLA_V)
    gates_ref[:, GLA_V:] = proj(_OFF_GR, RET_V)


_COL_ROWS = 3 * GLA_QK + 2 * RET_QK


def _mix_sample_state_kernel(*refs, n_stacked_in):
    cols_ref, rows_ref, sg_ref, sr_ref, gam_ref = refs[:5]
    nsg_ref, nsr_ref, o_ref = refs[5 + n_stacked_in:]
    nb = sg_ref.shape[0]
    s = cols_ref.shape[1]
    blk = pl.program_id(0)
    lane = lax.broadcasted_iota(jnp.int32, (1, s), 1)

    def body(bb, carry):
        b = blk * nb + bb
        onehot = (lane == b).astype(F32)
        col = jnp.sum(cols_ref[...] * onehot, axis=1, keepdims=True)
        vrow = rows_ref[b]
        a_g = col[0:GLA_QK]
        k_g = col[GLA_QK:2 * GLA_QK]
        q_g = col[2 * GLA_QK:3 * GLA_QK]
        base = 3 * GLA_QK
        outs = []
        for h in range(GLA_HEADS):
            rs = slice(h * GLA_DK, (h + 1) * GLA_DK)
            v_h = vrow[:, h * GLA_DV:(h + 1) * GLA_DV]
            s_new = a_g[rs] * sg_ref[bb, h] + k_g[rs] * v_h
            nsg_ref[bb, h] = s_new
            outs.append(jnp.sum(q_g[rs] * s_new, axis=0, keepdims=True))
        for h in range(RET_HEADS):
            rs = slice(h * RET_DK, (h + 1) * RET_DK)
            k_h = col[base + h * RET_DK:base + (h + 1) * RET_DK]
            q_h = col[base + RET_QK + h * RET_DK:base + RET_QK + (h + 1) * RET_DK]
            v_h = vrow[:, GLA_V + h * RET_DV:GLA_V + (h + 1) * RET_DV]
            s_new = gam_ref[h] * sr_ref[bb, h] + k_h * v_h
            nsr_ref[bb, h] = s_new
            outs.append(jnp.sum(q_h * s_new, axis=0, keepdims=True))
        o_ref[b] = jnp.concatenate(outs, axis=1)
        return carry

    lax.fori_loop(0, nb, body, 0)


def _mix_sample_out_kernel(x_ref, o_ref, gates_ref, ggla_ref, gret_ref, bret_ref, wout_ref,
                           lng_ref, lnb_ref, y_ref):
    x = x_ref[...]
    o = o_ref[...]
    gates = gates_ref[...]
    ggla = ggla_ref[...]
    heads = []
    for h in range(GLA_HEADS):
        sl = slice(h * GLA_DV, (h + 1) * GLA_DV)
        oh = o[:, sl]
        oh = oh * lax.rsqrt(jnp.mean(oh * oh, axis=-1, keepdims=True) + NORM_EPS) * ggla
        heads.append(oh * _silu(gates[:, sl]))
    for h in range(RET_HEADS):
        sl = slice(GLA_V + h * RET_DV, GLA_V + (h + 1) * RET_DV)
        ps = slice(h * RET_DV, (h + 1) * RET_DV)
        oh = o[:, sl]
        mu = jnp.mean(oh, axis=-1, keepdims=True)
        oc = oh - mu
        var = jnp.mean(oc * oc, axis=-1, keepdims=True)
        oh = oc * lax.rsqrt(var + NORM_EPS) * gret_ref[:, ps] + bret_ref[:, ps]
        heads.append(oh * _silu(gates[:, sl]))
    mixed = jnp.concatenate(heads, axis=1).astype(BF16)
    y = _dot(mixed, wout_ref[...])
    y_ref[...] = _layer_norm(ALPHA * x + y, lng_ref[...], lnb_ref[...])


def _mix_sample(x, state_gla, state_ret, layer, li, stacked, win, wup, bgate, ggla, gret, bret, wout, lng, lnb,
                cos_s, sin_s, gam):
    s, d = x.shape
    whole = lambda a: pl.BlockSpec(a.shape, lambda i: (0,) * a.ndim)
    proj_out = [
        jax.ShapeDtypeStruct((_COL_ROWS, s), F32),
        jax.ShapeDtypeStruct((s, GLA_V + RET_V), F32),
        jax.ShapeDtypeStruct((s, GLA_V + RET_V), F32),
    ]
    cols, rows, gates = pl.pallas_call(
        _mix_sample_proj_kernel,
        grid=(1,),
        in_specs=[whole(x), _layer_spec(win.shape, li), _layer_spec(wup.shape, li),
                  _layer_spec(bgate.shape, li), whole(cos_s), whole(sin_s)],
        out_specs=[whole(o) for o in proj_out],
        out_shape=proj_out,
        compiler_params=pltpu.CompilerParams(
            dimension_semantics=("arbitrary",), vmem_limit_bytes=VMEM_LIMIT),
        name="mix_sample_proj",
    )(x, win, wup, bgate, cos_s, sin_s)
    rows = rows.reshape(s, 1, GLA_V + RET_V)

    nb = SUBLANES
    assert s % nb == 0
    gla_spec = pl.BlockSpec((None, nb, GLA_HEADS, GLA_DK, GLA_DV), lambda i: (li, i, 0, 0, 0))
    ret_spec = pl.BlockSpec((None, nb, RET_HEADS, RET_DK, RET_DV), lambda i: (li, i, 0, 0, 0))
    stacked = tuple(stacked)
    nsg, nsr, o = pl.pallas_call(
        functools.partial(_mix_sample_state_kernel, n_stacked_in=len(stacked)),
        grid=(s // nb,),
        in_specs=[
            pl.BlockSpec((_COL_ROWS, s), lambda i: (0, 0)),
            pl.BlockSpec((s, 1, GLA_V + RET_V), lambda i: (0, 0, 0)),
            gla_spec,
            ret_spec,
            pl.BlockSpec((RET_HEADS, RET_DK, RET_DV), lambda i: (0, 0, 0)),
        ] + [pl.BlockSpec(memory_space=pl.ANY)] * len(stacked),
        out_specs=[
            gla_spec,
            ret_spec,
            pl.BlockSpec((s, 1, GLA_V + RET_V), lambda i: (0, 0, 0)),
        ],
        out_shape=[
            jax.ShapeDtypeStruct(state_gla.shape, F32),
            jax.ShapeDtypeStruct(state_ret.shape, F32),
            jax.ShapeDtypeStruct((s, 1, GLA_V + RET_V), F32),
        ],
        input_output_aliases={5 + k: k for k in range(len(stacked))},
        compiler_params=pltpu.CompilerParams(
            dimension_semantics=("arbitrary",), vmem_limit_bytes=VMEM_LIMIT),
        name="mix_sample_state",
    )(cols, rows, state_gla, state_ret, gam, *stacked)
    o = o.reshape(s, GLA_V + RET_V)

    y = pl.pallas_call(
        _mix_sample_out_kernel,
        grid=(1,),
        in_specs=[whole(x), whole(o), whole(gates), _layer_spec(ggla.shape, li), _layer_spec(gret.shape, li),
                  _layer_spec(bret.shape, li), _layer_spec(wout.shape, li),
                  _layer_spec(lng.shape, layer), _layer_spec(lnb.shape, layer)],
        out_specs=whole(x),
        out_shape=jax.ShapeDtypeStruct((s, d), F32),
        compiler_params=pltpu.CompilerParams(
            dimension_semantics=("arbitrary",), vmem_limit_bytes=VMEM_LIMIT),
        name="mix_sample_out",
    )(x, o, gates, ggla, gret, bret, wout, lng, lnb)
    return y, nsg, nsr


def _prep_mix_in(w):
    a = 2 * GLA_QK + GLA_V
    lr = jnp.pad(w[:, :, a:a + GLA_RANK], ((0, 0), (0, 0), (0, PAD_RANK - GLA_RANK)))
    return jnp.concatenate([w[:, :, :a], w[:, :, a + GLA_RANK:], lr], axis=2).astype(BF16)


def kernel(x_prompt, x_sample, state_gla, state_ret, state_conv, state_ffn,
           w_mix_in, w_gla_gate_up, b_gla_gate, g_gla_norm, g_ret_norm, b_ret_norm, w_mix_out,
           w_sc_in, w_sc_dw, b_sc_dw, w_sc_out, ln1_g, ln1_b,
           w_ffn_in, w_ffn_dw, b_ffn_dw, w_ffn_out, ln2_g, ln2_b):
    bp, tp, d = x_prompt.shape
    s, ts, _ = x_sample.shape
    assert ts == 1 and d == D_MODEL
    xp = x_prompt.reshape(bp * tp, d)
    xs = x_sample.reshape(s, d)

    tables = _mix_tables(tp)
    cos_s, sin_s = _rotary_tables(PAST_LEN + jnp.arange(ts, dtype=F32))
    log_gamma = jnp.log1p(-jnp.exp2(-5.0 - jnp.arange(RET_HEADS, dtype=F32)))
    gam = jnp.broadcast_to(jnp.exp(log_gamma)[:, None, None], (RET_HEADS, RET_DK, RET_DV))

    rows = lambda v: v[:, None, :]
    mix_w = (_prep_mix_in(w_mix_in),
             jnp.pad(w_gla_gate_up, ((0, 0), (0, PAD_RANK - GLA_RANK), (0, 0))).astype(BF16),
             rows(b_gla_gate), rows(g_gla_norm), rows(g_ret_norm), rows(b_ret_norm), w_mix_out.astype(BF16))
    later_w = (w_sc_in, w_sc_out, w_ffn_in, w_ffn_out)
    ln1 = (rows(ln1_g), rows(ln1_b))
    ln2 = (rows(ln2_g), rows(ln2_b))

    gla_p, ret_p, conv_p, ffn_p = [], [], [], []
    mix_s = None
    conv_s = jnp.zeros_like(state_conv)
    ffn_s = jnp.zeros_like(state_ffn)
    for layer in range(DEPTH):
        li = layer // 2
        if layer % 2 == 0:
            xp, sg, sr, casted, zeros = _mix_prompt(
                xp, bp, tp, layer, li, *mix_w, *ln1, tables,
                to_cast=later_w if layer == 0 else (), zeros_like=(state_gla, state_ret) if layer == 0 else ())
            if layer == 0:
                sc_in, sc_out, ffn_in, ffn_out = casted
                sc_w = (sc_in, w_sc_dw, rows(b_sc_dw), sc_out)
                ffn_w = (ffn_in, w_ffn_dw, rows(b_ffn_dw), ffn_out)
                mix_s = zeros
            gla_p.append(sg)
            ret_p.append(sr)
            xs, *mix_s = _mix_sample(xs, state_gla, state_ret, layer, li, mix_s, *mix_w, *ln1,
                                     cos_s, sin_s, gam)
        else:
            xp, sc = _sc_prompt(xp, tp, layer, li, *sc_w, *ln1)
            conv_p.append(sc)
            xs, conv_s = _conv_sample(_sc_sample_kernel, "sc_sample", xs, state_conv, conv_s,
                                      layer, li, *sc_w, *ln1)
        xp, sf = _ffn_prompt(xp, tp, layer, *ffn_w, *ln2)
        ffn_p.append(sf)
        xs, ffn_s = _conv_sample(_ffn_sample_kernel, "ffn_sample", xs, state_ffn, ffn_s,
                                 layer, layer, *ffn_w, *ln2)
    gla_s, ret_s = mix_s

    return (xp.reshape(bp, tp, d), xs.reshape(s, ts, d),
            jnp.stack(gla_p), gla_s, jnp.stack(ret_p), ret_s,
            jnp.stack(conv_p), conv_s, jnp.stack(ffn_p), ffn_s)
```

```python
import functools

import jax
import jax.numpy as jnp
from jax import lax
from jax.experimental import pallas as pl
from jax.experimental.pallas import tpu as pltpu

F32 = jnp.float32
BF16 = jnp.bfloat16

D_MODEL = 1024
DEPTH = 4
PAST_LEN = 16384
GLA_HEADS = 4
GLA_DK = 64
GLA_DV = 128
GLA_RANK = 16
GLA_TAU = 16.0
RET_HEADS = 4
RET_DK = 128
RET_DV = 128
CONV_W = 3
D_FF = D_MODEL * 11 // 4
ROPE_BASE = 10000.0
NORM_EPS = 1e-5
ALPHA = (2 * DEPTH) ** 0.25
GLA_QK = GLA_HEADS * GLA_DK
GLA_V = GLA_HEADS * GLA_DV
RET_QK = RET_HEADS * RET_DK
RET_V = RET_HEADS * RET_DV

LANES = 128
SUBLANES = 8
VMEM_LIMIT = 56 * 1024 * 1024

GLA_CHUNK = 64
GLA_SUB = 32
MIX_TILE = 256
ROW_TILE = 512
SC_ROW_TILE = 1024
CONV_ROWS = 32
FF_CHUNK = 768
FF_FIRST_CHUNK = 512
SC_CHUNK = 256
PAD_RANK = LANES

_OFF_QG = 0
_OFF_KG = _OFF_QG + GLA_QK
_OFF_VG = _OFF_KG + GLA_QK
_OFF_RG = _OFF_VG + GLA_V
_OFF_QR = _OFF_RG + GLA_V
_OFF_KR = _OFF_QR + RET_QK
_OFF_VR = _OFF_KR + RET_QK
_OFF_GR = _OFF_VR + RET_V
_OFF_LR = _OFF_GR + RET_V
MIX_COLS = _OFF_LR + PAD_RANK


def _dot(a, b):
    return jnp.dot(a, b, preferred_element_type=F32)


def _dot_nt(a, b):
    return lax.dot_general(a, b, (((1,), (1,)), ((), ())), preferred_element_type=F32)


def _layer_norm(x, g, b):
    mu = jnp.mean(x, axis=-1, keepdims=True)
    xc = x - mu
    var = jnp.mean(xc * xc, axis=-1, keepdims=True)
    return xc * lax.rsqrt(var + NORM_EPS) * g + b


def _silu(x):
    return x * jax.nn.sigmoid(x)


def _log_sigmoid(x):
    return jnp.minimum(x, 0.0) - jnp.log1p(jnp.exp(-jnp.abs(x)))


def _split3(x):
    hi = x.astype(BF16)
    r1 = x - hi.astype(F32)
    mid = r1.astype(BF16)
    lo = (r1 - mid.astype(F32)).astype(BF16)
    return hi, mid, lo


def _causal_conv3_blocks(rows, n_rows, prev, w, b, post):
    w0, w1, w2 = w[0:1], w[1:2], w[2:3]
    out = []
    for lo in range(0, n_rows, CONV_ROWS):
        head = prev if lo == 0 else rows(lo - SUBLANES, lo)
        s = jnp.concatenate([head, rows(lo, lo + CONV_ROWS)], axis=0)
        s1 = pltpu.roll(s, 1, 0)[SUBLANES:]
        s2 = pltpu.roll(s, 2, 0)[SUBLANES:]
        out.append(post(lo, b + w0 * s2 + w1 * s1 + w2 * s[SUBLANES:]))
    return jnp.concatenate(out, axis=0)


def _ffn_prompt_kernel(x_ref, win_ref, wdw_ref, bdw_ref, wout_ref, g_ref, b_ref,
                       y_ref, st_ref, xb_ref, carry_ref, *, tiles_per_seq):
    i = pl.program_id(0)

    @pl.when(i % tiles_per_seq == 0)
    def _():
        carry_ref[...] = jnp.zeros_like(carry_ref)

    tm = x_ref.shape[0]
    ff = wout_ref.shape[0]
    first = min(FF_FIRST_CHUNK, ff)
    bounds = [0] + list(range(first, ff, FF_CHUNK)) + [ff]
    n_chunks = len(bounds) - 1
    xb_ref[...] = x_ref[...].astype(BF16)

    def cols(ref, f):
        lo, hi = bounds[f], bounds[f + 1]
        return jnp.concatenate([ref[:, lo:hi], ref[:, ff + lo:ff + hi]], axis=1)

    def up(f):
        lo, hi = bounds[f], bounds[f + 1]
        xb = xb_ref[...]
        return jnp.concatenate([_dot(xb, win_ref[:, lo:hi]), _dot(xb, win_ref[:, ff + lo:ff + hi])], axis=1)

    def act(f, u):
        lo, hi = bounds[f], bounds[f + 1]
        fc = hi - lo
        h = _causal_conv3_blocks(
            lambda r0, r1: u[r0:r1], tm, cols(carry_ref, f), cols(wdw_ref, f), cols(bdw_ref, f),
            lambda r0, c: (_silu(c[:, fc:]) * c[:, :fc]).astype(BF16))
        for k in range(2):
            tail = u[tm - SUBLANES:, k * fc:(k + 1) * fc]
            carry_ref[:, k * ff + lo:k * ff + hi] = tail
            st_ref[0, :, k * ff + lo:k * ff + hi] = tail
        return h

    def down(f, h):
        return _dot(h, wout_ref[bounds[f]:bounds[f + 1], :])

    u = up(0)
    h_prev = None
    acc = None
    for f in range(n_chunks):
        u_next = up(f + 1) if f + 1 < n_chunks else None
        h = act(f, u)
        if h_prev is not None:
            p = down(f - 1, h_prev)
            acc = p if acc is None else acc + p
        u, h_prev = u_next, h
    half = tm // 2
    w_last = wout_ref[bounds[n_chunks - 1]:bounds[n_chunks], :]
    for lo_r in (0, half):
        rs = slice(lo_r, lo_r + half)
        tot = acc[rs] + _dot(h_prev[rs], w_last)
        y_ref[rs, :] = _layer_norm(ALPHA * x_ref[rs, :] + tot, g_ref[...], b_ref[...])


def _layer_spec(shape, layer):
    zeros = (0,) * (len(shape) - 1)
    return pl.BlockSpec((None,) + tuple(shape[1:]), lambda *_: (layer,) + zeros)


def _ffn_prompt(x, seq_len, layer, win, wdw, bdw, wout, g, b):
    n, d = x.shape
    tm = min(ROW_TILE, seq_len)
    assert seq_len % tm == 0 and n % seq_len == 0
    n_seq = n // seq_len
    tps = seq_len // tm
    ff2 = win.shape[2]
    buffered = lambda spec: pl.BlockSpec(spec.block_shape, spec.index_map, pipeline_mode=pl.Buffered(1))
    y, st = pl.pallas_call(
        functools.partial(_ffn_prompt_kernel, tiles_per_seq=tps),
        grid=(n // tm,),
        in_specs=[
            pl.BlockSpec((tm, d), lambda i: (i, 0)),
            buffered(_layer_spec(win.shape, layer)),
            _layer_spec(wdw.shape, layer),
            _layer_spec(bdw.shape, layer),
            buffered(_layer_spec(wout.shape, layer)),
            _layer_spec(g.shape, layer),
            _layer_spec(b.shape, layer),
        ],
        out_specs=[
            pl.BlockSpec((tm, d), lambda i: (i, 0)),
            pl.BlockSpec((1, SUBLANES, ff2), lambda i: (i // tps, 0, 0)),
        ],
        out_shape=[
            jax.ShapeDtypeStruct((n, d), F32),
            jax.ShapeDtypeStruct((n_seq, SUBLANES, ff2), F32),
        ],
        scratch_shapes=[
            pltpu.VMEM((tm, d), BF16),
            pltpu.VMEM((SUBLANES, ff2), F32),
        ],
        compiler_params=pltpu.CompilerParams(
            dimension_semantics=("arbitrary",), vmem_limit_bytes=VMEM_LIMIT),
        name="ffn_prompt",
    )(x, win, wdw, bdw, wout, g, b)
    return y, st[:, SUBLANES - 2:, :]


def _sc_prompt_kernel(x_ref, win_ref, wdw_ref, bdw_ref, wout_ref, g_ref, b_ref,
                      y_ref, st_ref, xb_ref, carry_ref, *, tiles_per_seq):
    i = pl.program_id(0)

    @pl.when(i % tiles_per_seq == 0)
    def _():
        carry_ref[...] = jnp.zeros_like(carry_ref)

    tm, d = x_ref.shape
    cw = SC_CHUNK
    n_chunks = d // cw
    xb_ref[...] = x_ref[...].astype(BF16)

    def up(f):
        xb = xb_ref[...]
        return jnp.concatenate([_dot(xb, win_ref[:, k * d + f * cw:k * d + (f + 1) * cw]) for k in range(3)],
                               axis=1)

    def act(f, z):
        cs = slice(f * cw, (f + 1) * cw)
        ch_rows = lambda lo, hi: z[lo:hi, cw:2 * cw] * z[lo:hi, 2 * cw:]
        m = _causal_conv3_blocks(
            ch_rows, tm, carry_ref[:, cs], wdw_ref[:, cs], bdw_ref[:, cs],
            lambda lo, conv: (z[lo:lo + CONV_ROWS, :cw] * conv).astype(BF16))
        tail = ch_rows(tm - SUBLANES, tm)
        carry_ref[:, cs] = tail
        st_ref[0, :, cs] = tail
        return m

    def down(f, m):
        return _dot(m, wout_ref[f * cw:(f + 1) * cw, :])

    z = up(0)
    m_prev = None
    acc = None
    for f in range(n_chunks):
        z_next = up(f + 1) if f + 1 < n_chunks else None
        m = act(f, z)
        if m_prev is not None:
            p = down(f - 1, m_prev)
            acc = p if acc is None else acc + p
        z, m_prev = z_next, m
    half = tm // 2
    w_last = wout_ref[(n_chunks - 1) * cw:n_chunks * cw, :]
    for lo_r in (0, half):
        rs = slice(lo_r, lo_r + half)
        tot = acc[rs] + _dot(m_prev[rs], w_last)
        y_ref[rs, :] = _layer_norm(ALPHA * x_ref[rs, :] + tot, g_ref[...], b_ref[...])


def _sc_prompt(x, seq_len, layer, li, win, wdw, bdw, wout, g, b):
    n, d = x.shape
    tm = min(SC_ROW_TILE, seq_len)
    assert seq_len % tm == 0 and n % seq_len == 0
    n_seq = n // seq_len
    tps = seq_len // tm
    buffered = lambda spec: pl.BlockSpec(spec.block_shape, spec.index_map, pipeline_mode=pl.Buffered(1))
    y, st = pl.pallas_call(
        functools.partial(_sc_prompt_kernel, tiles_per_seq=tps),
        grid=(n // tm,),
        in_specs=[
            pl.BlockSpec((tm, d), lambda i: (i, 0)),
            buffered(_layer_spec(win.shape, li)),
            _layer_spec(wdw.shape, li),
            _layer_spec(bdw.shape, li),
            buffered(_layer_spec(wout.shape, li)),
            _layer_spec(g.shape, layer),
            _layer_spec(b.shape, layer),
        ],
        out_specs=[
            pl.BlockSpec((tm, d), lambda i: (i, 0)),
            pl.BlockSpec((1, SUBLANES, d), lambda i: (i // tps, 0, 0)),
        ],
        out_shape=[
            jax.ShapeDtypeStruct((n, d), F32),
            jax.ShapeDtypeStruct((n_seq, SUBLANES, d), F32),
        ],
        scratch_shapes=[
            pltpu.VMEM((tm, d), BF16),
            pltpu.VMEM((SUBLANES, d), F32),
        ],
        compiler_params=pltpu.CompilerParams(
            dimension_semantics=("arbitrary",), vmem_limit_bytes=VMEM_LIMIT),
        name="sc_prompt",
    )(x, win, wdw, bdw, wout, g, b)
    return y, st[:, SUBLANES - 2:, :]


def _rotary(x, cosf, sinf):
    return x * cosf + pltpu.roll(x, RET_DK // 2, 1) * sinf


def _gla_tile(qg, kg, vg, wn_all, r_all, st, mask_q, masks_p, mask_bd, sblock):
    c = GLA_CHUNK
    n_chunks = qg.shape[0] // c
    n_sub = c // GLA_SUB
    pre = []
    for ci in range(n_chunks):
        sl = slice(ci * c, (ci + 1) * c)
        q_c, k_c, v_c, wn, r = qg[sl], kg[sl], vg[sl], wn_all[sl], r_all[sl]
        cum = r + wn
        tot = cum[c - 1:c, :]
        vst = jnp.concatenate([v_c[:, h * GLA_DV:(h + 1) * GLA_DV] for h in range(GLA_HEADS)], axis=0)
        pre.append(dict(q=q_c, k=k_c, r=r, cum=cum, tot=tot, qw=q_c * jnp.exp(wn), vst=vst,
                        vst_b=vst.astype(BF16)))
    scores = []
    for p in pre:
        row = []
        for i in range(n_sub):
            r_i = p["r"][i * GLA_SUB:i * GLA_SUB + 1, :]
            kt = jnp.where(sblock <= i, p["k"] * jnp.exp(r_i - p["cum"]), 0.0)
            kbig = jnp.concatenate([kt] * GLA_HEADS, axis=0).astype(BF16)
            q16 = p["qw"][i * GLA_SUB:(i + 1) * GLA_SUB]
            qs = jnp.where(mask_q, jnp.concatenate([q16] * GLA_HEADS, axis=0), 0.0).astype(BF16)
            row.append(_dot_nt(qs, kbig))
        scores.append(row)
    intra = [[_dot(jnp.where(masks_p[i], scores[ci][i], 0.0).astype(BF16), p["vst_b"])
              for i in range(n_sub)] for ci, p in enumerate(pre)]
    upds, qcss = [], []
    for p in pre:
        kd = p["k"] * jnp.exp(p["tot"] - p["cum"])
        kdbig = jnp.where(mask_bd, jnp.concatenate([kd] * GLA_HEADS, axis=0), 0.0).astype(BF16)
        upds.append(_dot(p["vst"].T.astype(BF16), kdbig))
        qc = p["q"] * jnp.exp(p["cum"])
        qcss.append(jnp.where(mask_bd, jnp.concatenate([qc] * GLA_HEADS, axis=0), 0.0).astype(BF16))
    inters = []
    for ci, p in enumerate(pre):
        inters.append(_dot_nt(qcss[ci], st.astype(BF16)))
        st = jnp.exp(p["tot"]) * st + upds[ci]
    chunks = []
    for ci in range(n_chunks):
        heads = []
        for h in range(GLA_HEADS):
            intra_h = jnp.concatenate([intra[ci][i][h * GLA_SUB:(h + 1) * GLA_SUB] for i in range(n_sub)], axis=0)
            heads.append(intra_h + inters[ci][h * c:(h + 1) * c])
        chunks.append(jnp.concatenate(heads, axis=1))
    return jnp.concatenate(chunks, axis=0), st


_MIX_N_IN = 16


def _mix_prompt_kernel(*refs, n_cast, n_zero):
    (x_ref, win_ref, wup_ref, bgate_ref, ggla_ref, gret_ref, bret_ref, wout_ref,
     lng_ref, lnb_ref, cos_ref, sin_ref, mcat_ref, dmask_ref, qdec_ref, kdec_ref) = refs[:_MIX_N_IN]
    cast_in = refs[_MIX_N_IN:_MIX_N_IN + n_cast]
    n_in = _MIX_N_IN + n_cast
    y_ref, sg_ref, sr_ref = refs[n_in:n_in + 3]
    cast_out = refs[n_in + 3:n_in + 3 + n_cast]
    zero_out = refs[n_in + 3 + n_cast:n_in + 3 + n_cast + n_zero]
    stg_ref, str_ref = refs[n_in + 3 + n_cast + n_zero:]
    for src, dst in zip(cast_in, cast_out):
        dst[...] = src[...].astype(BF16)
    for dst in zero_out:
        dst[...] = jnp.zeros_like(dst)
    j = pl.program_id(1)
    nj = pl.num_programs(1)

    @pl.when(j == 0)
    def _():
        stg_ref[...] = jnp.zeros_like(stg_ref)
        str_ref[...] = jnp.zeros_like(str_ref)

    tc = x_ref.shape[0]
    x = x_ref[...]
    xb = x.astype(BF16)

    def proj(off, width):
        return _dot(xb, win_ref[:, off:off + width])

    lr = proj(_OFF_LR, PAD_RANK)
    gate = _dot(lr.astype(BF16), wup_ref[...]) + bgate_ref[...]
    lg = _log_sigmoid(gate) * (1.0 / GLA_TAU)
    hi, mid, lo = _split3(lg)
    qr_all = proj(_OFF_QR, RET_QK)
    kr_all = proj(_OFF_KR, RET_QK)
    vr_all = proj(_OFF_VR, RET_V)
    qg = proj(_OFF_QG, GLA_QK) * (GLA_DK ** -0.5)
    kg = proj(_OFF_KG, GLA_QK)
    vg = proj(_OFF_VG, GLA_V)
    cs = _dot(mcat_ref[...], jnp.concatenate([hi, mid, lo], axis=1))
    cs = cs[:, :GLA_QK] + cs[:, GLA_QK:2 * GLA_QK] + cs[:, 2 * GLA_QK:]
    wn_all = cs[:tc]
    r_all = cs[tc:]

    cosf = cos_ref[...]
    sinf = sin_ref[...]
    hs = range(RET_HEADS)
    hsl = [slice(h * RET_DK, (h + 1) * RET_DK) for h in hs]
    qhs = [_rotary(qr_all[:, hsl[h]], cosf, sinf) for h in hs]
    khs = [_rotary(kr_all[:, hsl[h]], cosf, sinf) * (RET_DK ** -0.5) for h in hs]
    vhs = [vr_all[:, hsl[h]].astype(BF16) for h in hs]
    scs = [_dot_nt(qhs[h].astype(BF16), khs[h].astype(BF16)) for h in hs]
    s_hs = [str_ref[h] for h in hs]
    inter_r = [_dot((qhs[h] * qdec_ref[h]).astype(BF16), s_hs[h].astype(BF16)) for h in hs]
    upd_r = [_dot((khs[h] * kdec_ref[h]).T.astype(BF16), vhs[h]) for h in hs]
    o_r = [_dot((scs[h] * dmask_ref[h]).astype(BF16), vhs[h]) + inter_r[h] for h in hs]
    for h in hs:
        str_ref[h] = qdec_ref[h][tc - 1:tc, :] * s_hs[h] + upd_r[h]
    gr_all = proj(_OFF_GR, RET_V)
    rg = proj(_OFF_RG, GLA_V)


    c = GLA_CHUNK
    assert c == GLA_DK
    hq = GLA_HEADS * GLA_SUB
    log2 = lambda n: n.bit_length() - 1
    row64 = lax.broadcasted_iota(jnp.int32, (hq, GLA_QK), 0)
    lane64 = lax.broadcasted_iota(jnp.int32, (hq, GLA_QK), 1)
    lane_head = lax.shift_right_logical(lane64, log2(c))
    row_head = lax.shift_right_logical(row64, log2(GLA_SUB))
    mask_q = lane_head == row_head
    masks_p = [mask_q & ((lane64 & (c - 1)) <= (row64 & (GLA_SUB - 1)) + i * GLA_SUB)
               for i in range(c // GLA_SUB)]
    rowb = lax.broadcasted_iota(jnp.int32, (GLA_HEADS * c, GLA_QK), 0)
    laneb = lax.broadcasted_iota(jnp.int32, (GLA_HEADS * c, GLA_QK), 1)
    mask_bd = lax.shift_right_logical(rowb, log2(c)) == lax.shift_right_logical(laneb, log2(c))
    sblock = lax.shift_right_logical(lax.broadcasted_iota(jnp.int32, (c, GLA_QK), 0), log2(GLA_SUB))

    og, st = _gla_tile(qg, kg, vg, wn_all, r_all, stg_ref[...], mask_q, masks_p, mask_bd, sblock)
    stg_ref[...] = st
    ggla = ggla_ref[...]
    og_heads = []
    for h in range(GLA_HEADS):
        sl = slice(h * GLA_DV, (h + 1) * GLA_DV)
        o = og[:, sl]
        o = o * lax.rsqrt(jnp.mean(o * o, axis=-1, keepdims=True) + NORM_EPS) * ggla
        og_heads.append(o * _silu(rg[:, sl]))
    or_heads = []
    for h in hs:
        o = o_r[h]
        mu = jnp.mean(o, axis=-1, keepdims=True)
        oc = o - mu
        var = jnp.mean(oc * oc, axis=-1, keepdims=True)
        o = oc * lax.rsqrt(var + NORM_EPS) * gret_ref[:, hsl[h]] + bret_ref[:, hsl[h]]
        or_heads.append(o * _silu(gr_all[:, hsl[h]]))

    mixed = jnp.concatenate(og_heads + or_heads, axis=1).astype(BF16)
    half = tc // 2
    for lo_r in (0, half):
        rs = slice(lo_r, lo_r + half)
        y = _dot(mixed[rs], wout_ref[...])
        y_ref[rs, :] = _layer_norm(ALPHA * x[rs] + y, lng_ref[...], lnb_ref[...])

    @pl.when(j == nj - 1)
    def _():
        sg_ref[0] = stg_ref[...].T
        sr_ref[0] = str_ref[...]


def _mix_prompt(x, n_seq, seq_len, layer, li, win, wup, bgate, ggla, gret, bret, wout, lng, lnb, tables,
                to_cast=(), zeros_like=()):
    n, d = x.shape
    tc = min(MIX_TILE, seq_len)
    assert seq_len % tc == 0 and tc % GLA_CHUNK == 0
    nt = seq_len // tc
    cosf, sinf, mcat, dmask, qdec, kdec = tables
    c2 = lambda b, j: (0, 0)
    c3 = lambda b, j: (0, 0, 0)
    buffered = lambda spec: pl.BlockSpec(spec.block_shape, spec.index_map, pipeline_mode=pl.Buffered(1))
    n_steps = n_seq * nt
    flat = [w.reshape(-1, w.shape[-1]) for w in to_cast]
    for w in flat:
        assert w.shape[0] % (n_steps * 2 * SUBLANES) == 0, w.shape
    row_block = lambda w: pl.BlockSpec((w.shape[0] // n_steps, w.shape[1]), lambda b, j: (b * nt + j, 0))
    cast_specs = [row_block(w) for w in flat]
    zflat = [jax.ShapeDtypeStruct((z.size // z.shape[-1], z.shape[-1]), F32) for z in zeros_like]
    for z in zflat:
        assert z.shape[0] % (n_steps * SUBLANES) == 0, z.shape
    zero_specs = [row_block(z) for z in zflat]
    y, sg, sr, *side = pl.pallas_call(
        functools.partial(_mix_prompt_kernel, n_cast=len(flat), n_zero=len(zflat)),
        grid=(n_seq, nt),
        in_specs=[
            pl.BlockSpec((tc, d), lambda b, j: (b * nt + j, 0)),
            buffered(_layer_spec(win.shape, li)),
            _layer_spec(wup.shape, li),
            _layer_spec(bgate.shape, li),
            _layer_spec(ggla.shape, li),
            _layer_spec(gret.shape, li),
            _layer_spec(bret.shape, li),
            buffered(_layer_spec(wout.shape, li)),
            _layer_spec(lng.shape, layer),
            _layer_spec(lnb.shape, layer),
            pl.BlockSpec((tc, RET_DK), lambda b, j: (j, 0)),
            pl.BlockSpec((tc, RET_DK), lambda b, j: (j, 0)),
            pl.BlockSpec((2 * tc, tc), c2),
            pl.BlockSpec((RET_HEADS, tc, tc), c3),
            pl.BlockSpec((RET_HEADS, tc, RET_DK), c3),
            pl.BlockSpec((RET_HEADS, tc, RET_DK), c3),
        ] + cast_specs,
        out_specs=[
            pl.BlockSpec((tc, d), lambda b, j: (b * nt + j, 0)),
            pl.BlockSpec((1, GLA_QK, GLA_DV), lambda b, j: (b, 0, 0)),
            pl.BlockSpec((1, RET_HEADS, RET_DK, RET_DV), lambda b, j: (b, 0, 0, 0)),
        ] + cast_specs + zero_specs,
        out_shape=[
            jax.ShapeDtypeStruct((n, d), F32),
            jax.ShapeDtypeStruct((n_seq, GLA_QK, GLA_DV), F32),
            jax.ShapeDtypeStruct((n_seq, RET_HEADS, RET_DK, RET_DV), F32),
        ] + [jax.ShapeDtypeStruct(w.shape, BF16) for w in flat] + zflat,
        scratch_shapes=[
            pltpu.VMEM((GLA_DV, GLA_QK), F32),
            pltpu.VMEM((RET_HEADS, RET_DK, RET_DV), F32),
        ],
        compiler_params=pltpu.CompilerParams(
            dimension_semantics=("arbitrary", "arbitrary"), vmem_limit_bytes=VMEM_LIMIT),
        name="mix_prompt",
    )(x, win, wup, bgate, ggla, gret, bret, wout, lng, lnb, cosf, sinf, mcat, dmask, qdec, kdec, *flat)
    casted = [c.reshape(w.shape) for c, w in zip(side, to_cast)]
    zeros = [z.reshape(a.shape) for z, a in zip(side[len(flat):], zeros_like)]
    return y, sg.reshape(n_seq, GLA_HEADS, GLA_DK, GLA_DV), sr, casted, zeros


def _mix_tables(seq_len):
    tc = min(MIX_TILE, seq_len)
    pos = jnp.arange(seq_len, dtype=F32)
    cosf, sinf = _rotary_tables(pos)
    t = jnp.arange(tc)
    same_sub = (t[:, None] // GLA_SUB) == (t[None, :] // GLA_SUB)
    same_chunk = (t[:, None] // GLA_CHUNK) == (t[None, :] // GLA_CHUNK)
    m_within = same_sub & (t[None, :] <= t[:, None])
    m_before = same_chunk & ((t[None, :] // GLA_SUB) < (t[:, None] // GLA_SUB))
    mcat = jnp.concatenate([m_within, m_before], axis=0).astype(BF16)
    log_gamma = jnp.log1p(-jnp.exp2(-5.0 - jnp.arange(RET_HEADS, dtype=F32)))
    tf = t.astype(F32)
    diff = tf[:, None] - tf[None, :]
    dmask = jnp.where(diff >= 0, jnp.exp(diff[None] * log_gamma[:, None, None]), 0.0)
    qdec = jnp.exp((tf[None, :, None] + 1.0) * log_gamma[:, None, None])
    kdec = jnp.exp((tc - 1.0 - tf[None, :, None]) * log_gamma[:, None, None])
    qdec = jnp.broadcast_to(qdec, (RET_HEADS, tc, RET_DK))
    kdec = jnp.broadcast_to(kdec, (RET_HEADS, tc, RET_DK))
    return cosf, sinf, mcat, dmask, qdec, kdec


def _rotary_tables(pos):
    inv_freq = 1.0 / (ROPE_BASE ** (jnp.arange(0, RET_DK, 2, dtype=F32) / RET_DK))
    ang = pos[:, None] * inv_freq[None, :]
    cos, sin = jnp.cos(ang), jnp.sin(ang)
    return jnp.concatenate([cos, cos], axis=1), jnp.concatenate([-sin, sin], axis=1)


def _ffn_sample_kernel(x_ref, st_ref, win_ref, wdw_ref, bdw_ref, wout_ref, g_ref, b_ref,
                       y_ref, nst_ref):
    x = x_ref[...]
    u = _dot(x.astype(BF16), win_ref[...])
    wdw = wdw_ref[...]
    b1 = st_ref[:, 1, :]
    c = bdw_ref[...] + wdw[0:1] * st_ref[:, 0, :] + wdw[1:2] * b1 + wdw[2:3] * u
    h = _silu(c[:, D_FF:]) * c[:, :D_FF]
    y = _dot(h.astype(BF16), wout_ref[...])
    y_ref[...] = _layer_norm(ALPHA * x + y, g_ref[...], b_ref[...])
    nst_ref[:, 0, :] = b1
    nst_ref[:, 1, :] = u


def _sc_sample_kernel(x_ref, st_ref, win_ref, wdw_ref, bdw_ref, wout_ref, g_ref, b_ref,
                      y_ref, nst_ref):
    x = x_ref[...]
    d = x.shape[1]
    z = _dot(x.astype(BF16), win_ref[...])
    ch = z[:, d:2 * d] * z[:, 2 * d:]
    wdw = wdw_ref[...]
    b1 = st_ref[:, 1, :]
    conv = bdw_ref[...] + wdw[0:1] * st_ref[:, 0, :] + wdw[1:2] * b1 + wdw[2:3] * ch
    y = _dot((z[:, :d] * conv).astype(BF16), wout_ref[...])
    y_ref[...] = _layer_norm(ALPHA * x + y, g_ref[...], b_ref[...])
    nst_ref[:, 0, :] = b1
    nst_ref[:, 1, :] = ch


def _conv_sample_kernel(body, *refs):
    body(*refs[:8], *refs[9:])


def _conv_sample(body, name, x, states, new_states, layer, li, win, wdw, bdw, wout, g, b):
    s, d = x.shape
    whole = lambda a: pl.BlockSpec(a.shape, lambda i: (0,) * a.ndim)
    st_spec = _layer_spec(states.shape, li)
    return pl.pallas_call(
        functools.partial(_conv_sample_kernel, body),
        grid=(1,),
        in_specs=[whole(x), st_spec, _layer_spec(win.shape, li), _layer_spec(wdw.shape, li),
                  _layer_spec(bdw.shape, li), _layer_spec(wout.shape, li),
                  _layer_spec(g.shape, layer), _layer_spec(b.shape, layer),
                  pl.BlockSpec(memory_space=pl.ANY)],
        out_specs=[whole(x), st_spec],
        out_shape=[jax.ShapeDtypeStruct((s, d), F32), jax.ShapeDtypeStruct(states.shape, F32)],
        input_output_aliases={8: 1},
        compiler_params=pltpu.CompilerParams(
            dimension_semantics=("arbitrary",), vmem_limit_bytes=VMEM_LIMIT),
        name=name,
    )(x, states, win, wdw, bdw, wout, g, b, new_states)


def _mix_sample_proj_kernel(x_ref, win_ref, wup_ref, bgate_ref, cos_ref, sin_ref,
                            cols_ref, rows_ref, gates_ref):
    xb = x_ref[...].astype(BF16)

    def proj(off, width):
        return _dot(xb, win_ref[:, off:off + width])

    qg = proj(_OFF_QG, GLA_QK) * (GLA_DK ** -0.5)
    kg = proj(_OFF_KG, GLA_QK)
    lr = proj(_OFF_LR, PAD_RANK)
    gate = _dot(lr.astype(BF16), wup_ref[...]) + bgate_ref[...]
    ag = jnp.exp(_log_sigmoid(gate) * (1.0 / GLA_TAU))
    cosf = cos_ref[...]
    sinf = sin_ref[...]
    qr = [_rotary(proj(_OFF_QR + h * RET_DK, RET_DK), cosf, sinf) for h in range(RET_HEADS)]
    kr = [_rotary(proj(_OFF_KR + h * RET_DK, RET_DK), cosf, sinf) * (RET_DK ** -0.5)
          for h in range(RET_HEADS)]
    cols_ref[0:GLA_QK, :] = ag.T
    cols_ref[GLA_QK:2 * GLA_QK, :] = kg.T
    cols_ref[2 * GLA_QK:3 * GLA_QK, :] = qg.T
    base = 3 * GLA_QK
    for h in range(RET_HEADS):
        cols_ref[base + h * RET_DK:base + (h + 1) * RET_DK, :] = kr[h].T
        cols_ref[base + RET_QK + h * RET_DK:base + RET_QK + (h + 1) * RET_DK, :] = qr[h].T
    rows_ref[:, :GLA_V] = proj(_OFF_VG, GLA_V)
    rows_ref[:, GLA_V:] = proj(_OFF_VR, RET_V)
    gates_ref[:, :GLA_V] = proj(_OFF_RG, GLA_V)
    gates_ref[:, GLA_V:] = proj(_OFF_GR, RET_V)


_COL_ROWS = 3 * GLA_QK + 2 * RET_QK


def _mix_sample_state_kernel(*refs, n_stacked_in):
    cols_ref, rows_ref, sg_ref, sr_ref, gam_ref = refs[:5]
    nsg_ref, nsr_ref, o_ref = refs[5 + n_stacked_in:]
    nb = sg_ref.shape[0]
    s = cols_ref.shape[1]
    blk = pl.program_id(0)
    lane = lax.broadcasted_iota(jnp.int32, (1, s), 1)

    def body(bb, carry):
        b = blk * nb + bb
        onehot = (lane == b).astype(F32)
        col = jnp.sum(cols_ref[...] * onehot, axis=1, keepdims=True)
        vrow = rows_ref[b]
        a_g = col[0:GLA_QK]
        k_g = col[GLA_QK:2 * GLA_QK]
        q_g = col[2 * GLA_QK:3 * GLA_QK]
        base = 3 * GLA_QK
        outs = []
        for h in range(GLA_HEADS):
            rs = slice(h * GLA_DK, (h + 1) * GLA_DK)
            v_h = vrow[:, h * GLA_DV:(h + 1) * GLA_DV]
            s_new = a_g[rs] * sg_ref[bb, h] + k_g[rs] * v_h
            nsg_ref[bb, h] = s_new
            outs.append(jnp.sum(q_g[rs] * s_new, axis=0, keepdims=True))
        for h in range(RET_HEADS):
            rs = slice(h * RET_DK, (h + 1) * RET_DK)
            k_h = col[base + h * RET_DK:base + (h + 1) * RET_DK]
            q_h = col[base + RET_QK + h * RET_DK:base + RET_QK + (h + 1) * RET_DK]
            v_h = vrow[:, GLA_V + h * RET_DV:GLA_V + (h + 1) * RET_DV]
            s_new = gam_ref[h] * sr_ref[bb, h] + k_h * v_h
            nsr_ref[bb, h] = s_new
            outs.append(jnp.sum(q_h * s_new, axis=0, keepdims=True))
        o_ref[b] = jnp.concatenate(outs, axis=1)
        return carry

    lax.fori_loop(0, nb, body, 0)


def _mix_sample_out_kernel(x_ref, o_ref, gates_ref, ggla_ref, gret_ref, bret_ref, wout_ref,
                           lng_ref, lnb_ref, y_ref):
    x = x_ref[...]
    o = o_ref[...]
    gates = gates_ref[...]
    ggla = ggla_ref[...]
    heads = []
    for h in range(GLA_HEADS):
        sl = slice(h * GLA_DV, (h + 1) * GLA_DV)
        oh = o[:, sl]
        oh = oh * lax.rsqrt(jnp.mean(oh * oh, axis=-1, keepdims=True) + NORM_EPS) * ggla
        heads.append(oh * _silu(gates[:, sl]))
    for h in range(RET_HEADS):
        sl = slice(GLA_V + h * RET_DV, GLA_V + (h + 1) * RET_DV)
        ps = slice(h * RET_DV, (h + 1) * RET_DV)
        oh = o[:, sl]
        mu = jnp.mean(oh, axis=-1, keepdims=True)
        oc = oh - mu
        var = jnp.mean(oc * oc, axis=-1, keepdims=True)
        oh = oc * lax.rsqrt(var + NORM_EPS) * gret_ref[:, ps] + bret_ref[:, ps]
        heads.append(oh * _silu(gates[:, sl]))
    mixed = jnp.concatenate(heads, axis=1).astype(BF16)
    y = _dot(mixed, wout_ref[...])
    y_ref[...] = _layer_norm(ALPHA * x + y, lng_ref[...], lnb_ref[...])


def _mix_sample(x, state_gla, state_ret, layer, li, stacked, win, wup, bgate, ggla, gret, bret, wout, lng, lnb,
                cos_s, sin_s, gam):
    s, d = x.shape
    whole = lambda a: pl.BlockSpec(a.shape, lambda i: (0,) * a.ndim)
    proj_out = [
        jax.ShapeDtypeStruct((_COL_ROWS, s), F32),
        jax.ShapeDtypeStruct((s, GLA_V + RET_V), F32),
        jax.ShapeDtypeStruct((s, GLA_V + RET_V), F32),
    ]
    cols, rows, gates = pl.pallas_call(
        _mix_sample_proj_kernel,
        grid=(1,),
        in_specs=[whole(x), _layer_spec(win.shape, li), _layer_spec(wup.shape, li),
                  _layer_spec(bgate.shape, li), whole(cos_s), whole(sin_s)],
        out_specs=[whole(o) for o in proj_out],
        out_shape=proj_out,
        compiler_params=pltpu.CompilerParams(
            dimension_semantics=("arbitrary",), vmem_limit_bytes=VMEM_LIMIT),
        name="mix_sample_proj",
    )(x, win, wup, bgate, cos_s, sin_s)
    rows = rows.reshape(s, 1, GLA_V + RET_V)

    nb = 2 * SUBLANES
    assert s % nb == 0
    gla_spec = pl.BlockSpec((None, nb, GLA_HEADS, GLA_DK, GLA_DV), lambda i: (li, i, 0, 0, 0))
    ret_spec = pl.BlockSpec((None, nb, RET_HEADS, RET_DK, RET_DV), lambda i: (li, i, 0, 0, 0))
    stacked = tuple(stacked)
    nsg, nsr, o = pl.pallas_call(
        functools.partial(_mix_sample_state_kernel, n_stacked_in=len(stacked)),
        grid=(s // nb,),
        in_specs=[
            pl.BlockSpec((_COL_ROWS, s), lambda i: (0, 0)),
            pl.BlockSpec((s, 1, GLA_V + RET_V), lambda i: (0, 0, 0)),
            gla_spec,
            ret_spec,
            pl.BlockSpec((RET_HEADS, RET_DK, RET_DV), lambda i: (0, 0, 0)),
        ] + [pl.BlockSpec(memory_space=pl.ANY)] * len(stacked),
        out_specs=[
            gla_spec,
            ret_spec,
            pl.BlockSpec((s, 1, GLA_V + RET_V), lambda i: (0, 0, 0)),
        ],
        out_shape=[
            jax.ShapeDtypeStruct(state_gla.shape, F32),
            jax.ShapeDtypeStruct(state_ret.shape, F32),
            jax.ShapeDtypeStruct((s, 1, GLA_V + RET_V), F32),
        ],
        input_output_aliases={5 + k: k for k in range(len(stacked))},
        compiler_params=pltpu.CompilerParams(
            dimension_semantics=("arbitrary",), vmem_limit_bytes=VMEM_LIMIT),
        name="mix_sample_state",
    )(cols, rows, state_gla, state_ret, gam, *stacked)
    o = o.reshape(s, GLA_V + RET_V)

    y = pl.pallas_call(
        _mix_sample_out_kernel,
        grid=(1,),
        in_specs=[whole(x), whole(o), whole(gates), _layer_spec(ggla.shape, li), _layer_spec(gret.shape, li),
                  _layer_spec(bret.shape, li), _layer_spec(wout.shape, li),
                  _layer_spec(lng.shape, layer), _layer_spec(lnb.shape, layer)],
        out_specs=whole(x),
        out_shape=jax.ShapeDtypeStruct((s, d), F32),
        compiler_params=pltpu.CompilerParams(
            dimension_semantics=("arbitrary",), vmem_limit_bytes=VMEM_LIMIT),
        name="mix_sample_out",
    )(x, o, gates, ggla, gret, bret, wout, lng, lnb)
    return y, nsg, nsr


def _prep_mix_in(w):
    a = 2 * GLA_QK + GLA_V
    lr = jnp.pad(w[:, :, a:a + GLA_RANK], ((0, 0), (0, 0), (0, PAD_RANK - GLA_RANK)))
    return jnp.concatenate([w[:, :, :a], w[:, :, a + GLA_RANK:], lr], axis=2).astype(BF16)


def kernel(x_prompt, x_sample, state_gla, state_ret, state_conv, state_ffn,
           w_mix_in, w_gla_gate_up, b_gla_gate, g_gla_norm, g_ret_norm, b_ret_norm, w_mix_out,
           w_sc_in, w_sc_dw, b_sc_dw, w_sc_out, ln1_g, ln1_b,
           w_ffn_in, w_ffn_dw, b_ffn_dw, w_ffn_out, ln2_g, ln2_b):
    bp, tp, d = x_prompt.shape
    s, ts, _ = x_sample.shape
    assert ts == 1 and d == D_MODEL
    xp = x_prompt.reshape(bp * tp, d)
    xs = x_sample.reshape(s, d)

    tables = _mix_tables(tp)
    cos_s, sin_s = _rotary_tables(PAST_LEN + jnp.arange(ts, dtype=F32))
    log_gamma = jnp.log1p(-jnp.exp2(-5.0 - jnp.arange(RET_HEADS, dtype=F32)))
    gam = jnp.broadcast_to(jnp.exp(log_gamma)[:, None, None], (RET_HEADS, RET_DK, RET_DV))

    rows = lambda v: v[:, None, :]
    mix_w = (_prep_mix_in(w_mix_in),
             jnp.pad(w_gla_gate_up, ((0, 0), (0, PAD_RANK - GLA_RANK), (0, 0))).astype(BF16),
             rows(b_gla_gate), rows(g_gla_norm), rows(g_ret_norm), rows(b_ret_norm), w_mix_out.astype(BF16))
    later_w = (w_sc_in, w_sc_out, w_ffn_in, w_ffn_out)
    ln1 = (rows(ln1_g), rows(ln1_b))
    ln2 = (rows(ln2_g), rows(ln2_b))

    gla_p, ret_p, conv_p, ffn_p = [], [], [], []
    mix_s = None
    conv_s = jnp.zeros_like(state_conv)
    ffn_s = jnp.zeros_like(state_ffn)
    for layer in range(DEPTH):
        li = layer // 2
        if layer % 2 == 0:
            xp, sg, sr, casted, zeros = _mix_prompt(
                xp, bp, tp, layer, li, *mix_w, *ln1, tables,
                to_cast=later_w if layer == 0 else (), zeros_like=(state_gla, state_ret) if layer == 0 else ())
            if layer == 0:
                sc_in, sc_out, ffn_in, ffn_out = casted
                sc_w = (sc_in, w_sc_dw, rows(b_sc_dw), sc_out)
                ffn_w = (ffn_in, w_ffn_dw, rows(b_ffn_dw), ffn_out)
                mix_s = zeros
            gla_p.append(sg)
            ret_p.append(sr)
            xs, *mix_s = _mix_sample(xs, state_gla, state_ret, layer, li, mix_s, *mix_w, *ln1,
                                     cos_s, sin_s, gam)
        else:
            xp, sc = _sc_prompt(xp, tp, layer, li, *sc_w, *ln1)
            conv_p.append(sc)
            xs, conv_s = _conv_sample(_sc_sample_kernel, "sc_sample", xs, state_conv, conv_s,
                                      layer, li, *sc_w, *ln1)
        xp, sf = _ffn_prompt(xp, tp, layer, *ffn_w, *ln2)
        ffn_p.append(sf)
        xs, ffn_s = _conv_sample(_ffn_sample_kernel, "ffn_sample", xs, state_ffn, ffn_s,
                                 layer, layer, *ffn_w, *ln2)
    gla_s, ret_s = mix_s

    return (xp.reshape(bp, tp, d), xs.reshape(s, ts, d),
            jnp.stack(gla_p), gla_s, jnp.stack(ret_p), ret_s,
            jnp.stack(conv_p), conv_s, jnp.stack(ffn_p), ffn_s)
```
